```python
import math
import jax, jax.numpy as jnp
from jax import lax
import numpy as np

D_MODEL = 2048
BATCH = 4
SEQ = 4096
DEPTH = 1
DEC_BATCH = 4
DEC_SEQ = 2048
PAST_LEN = 128

D_MIX = D_MODEL
CHUNK = 128
GMLP_HEADS = 8
GMLP_WIDTH = D_MIX // 2
GMLP_HEAD_DIM = GMLP_WIDTH // GMLP_HEADS
RET_HEADS = 8
RET_WIDTH = D_MIX - GMLP_WIDTH
RET_HEAD_DIM = RET_WIDTH // RET_HEADS
PROJ_WIDTH = 2 * GMLP_WIDTH + 4 * RET_WIDTH
ROPE_BASE = 10000.0
N_GROUPS = 4
EXPERTS_PER_GROUP = 8
N_EXPERTS = N_GROUPS * EXPERTS_PER_GROUP
TOP_K = 2
D_EXPERT = D_MODEL // 4
MOE_BLOCK = 128
EPS = 1e-6

kernel_name = "hymba_gmlp_retention_hmoe_encoder"


def rms_norm(x, g):
    xf = x.astype(jnp.float32)
    y = xf * lax.rsqrt(jnp.mean(xf * xf, axis=-1, keepdims=True) + EPS)
    return (y * g.astype(jnp.float32)).astype(x.dtype)


def rotary(x, positions):
    half = x.shape[-1] // 2
    inv = ROPE_BASE ** (-jnp.arange(half, dtype=jnp.float32) / half)
    ang = positions[:, None] * inv[None, :]
    cos = jnp.cos(ang)[None, :, None, :]
    sin = jnp.sin(ang)[None, :, None, :]
    x1, x2 = x[..., :half], x[..., half:]
    return jnp.concatenate([x1 * cos - x2 * sin, x1 * sin + x2 * cos], axis=-1).astype(x.dtype)


def spatial_gating(u, v, g_v, w_spatial, b_spatial):
    B, S, H, c = u.shape
    u = jax.nn.gelu(u)
    v = rms_norm(jax.nn.gelu(v), g_v)
    vc = v.reshape(B, S // CHUNK, CHUNK, H, c)
    mixed = jnp.einsum('hij,bnjhc->bnihc', w_spatial, vc) + b_spatial.T[None, None, :, :, None]
    return u * mixed.reshape(B, S, H, c)


def to_chunks(x):
    B, S, H, d = x.shape
    return x.reshape(B, S // CHUNK, CHUNK, H, d).transpose(0, 3, 1, 2, 4)


def flip_seq(x):
    return x[:, :, ::-1, ::-1]


def retention_direction(q, k, v, log_gamma, include_diag):
    pos = jnp.arange(CHUNK, dtype=jnp.float32)
    diff = pos[:, None] - pos[None, :]
    mask = (diff >= 0) if include_diag else (diff > 0)
    lg = log_gamma[:, None, None]
    intra_decay = jnp.where(mask, jnp.exp(jnp.where(mask, diff, 0.0) * lg), 0.0)
    scores = jnp.einsum('bhnid,bhnjd->bhnij', q, k) * intra_decay[None, :, None]
    inner = jnp.einsum('bhnij,bhnje->bhnie', scores, v)
    k_dec = k * jnp.exp((CHUNK - 1.0 - pos)[None, :] * log_gamma[:, None])[None, :, None, :, None]
    kv = jnp.einsum('bhncd,bhnce->bhnde', k_dec, v)
    chunk_decay = jnp.exp(CHUNK * log_gamma)[None, :, None, None]

    def step(state, kv_n):
        return state * chunk_decay + kv_n, state

    _, prev = lax.scan(step, jnp.zeros_like(kv[:, :, 0]), jnp.moveaxis(kv, 2, 0))
    prev = jnp.moveaxis(prev, 0, 2)
    q_dec = q * jnp.exp((pos + 1.0)[None, :] * log_gamma[:, None])[None, :, None, :, None]
    cross = jnp.einsum('bhncd,bhnde->bhnce', q_dec, prev)
    return inner + cross


def token_mix(h, w_in, gmlp_norm_v, gmlp_w_spatial, gmlp_b_spatial, ret_decay_fwd, ret_decay_bwd, ret_norm, w_out):
    B, S, _ = h.shape
    proj = jnp.einsum('bsd,de->bse', h, w_in)
    o = 0
    u = proj[..., o:o + GMLP_WIDTH]; o += GMLP_WIDTH
    gv = proj[..., o:o + GMLP_WIDTH]; o += GMLP_WIDTH
    q = proj[..., o:o + RET_WIDTH]; o += RET_WIDTH
    k = proj[..., o:o + RET_WIDTH]; o += RET_WIDTH
    v = proj[..., o:o + RET_WIDTH]; o += RET_WIDTH
    gate = proj[..., o:o + RET_WIDTH]

    a_out = spatial_gating(u.reshape(B, S, GMLP_HEADS, GMLP_HEAD_DIM), gv.reshape(B, S, GMLP_HEADS, GMLP_HEAD_DIM),
                           gmlp_norm_v, gmlp_w_spatial, gmlp_b_spatial).reshape(B, S, GMLP_WIDTH)

    pos = jnp.arange(S, dtype=jnp.float32)
    qh = rotary(q.reshape(B, S, RET_HEADS, RET_HEAD_DIM), pos)
    kh = rotary(k.reshape(B, S, RET_HEADS, RET_HEAD_DIM), pos) * (RET_HEAD_DIM ** -0.5)
    vh = v.reshape(B, S, RET_HEADS, RET_HEAD_DIM)
    qc, kc, vc = to_chunks(qh), to_chunks(kh), to_chunks(vh)
    lg_f = jax.nn.log_sigmoid(ret_decay_fwd.astype(jnp.float32))
    lg_b = jax.nn.log_sigmoid(ret_decay_bwd.astype(jnp.float32))
    fwd = retention_direction(qc, kc, vc, lg_f, True)
    bwd = flip_seq(retention_direction(flip_seq(qc), flip_seq(kc), flip_seq(vc), lg_b, False))
    ret = (fwd + bwd).transpose(0, 2, 3, 1, 4).reshape(B, S, RET_HEADS, RET_HEAD_DIM)
    ret = rms_norm(ret, ret_norm) * jax.nn.silu(gate.reshape(B, S, RET_HEADS, RET_HEAD_DIM))
    b_out = ret.reshape(B, S, RET_WIDTH).astype(h.dtype)

    mixed = jnp.concatenate([a_out, b_out], axis=-1)
    return jnp.einsum('bse,ed->bsd', mixed, w_out)


def hierarchical_moe(h, w_router_group, b_router_group, w_router_expert, b_router_expert,
                     w_expert_gate, w_expert_up, w_expert_down):
    B, S, D = h.shape
    T = B * S
    ht = h.reshape(T, D)
    logits_g = jnp.einsum('td,dg->tg', ht, w_router_group).astype(jnp.float32) + b_router_group.astype(jnp.float32)
    p_g = jax.nn.softmax(logits_g, axis=-1)
    g_idx = jnp.argmax(logits_g, axis=-1).astype(jnp.int32)
    p_sel = jnp.take_along_axis(p_g, g_idx[:, None], axis=1)[:, 0]
    logits_e = (jnp.einsum('td,de->te', ht, w_router_expert).astype(jnp.float32)
                + b_router_expert.astype(jnp.float32)).reshape(T, N_GROUPS, EXPERTS_PER_GROUP)
    le_sel = jnp.take_along_axis(logits_e, g_idx[:, None, None], axis=1)[:, 0]
    top_val, top_i = lax.top_k(le_sel, TOP_K)
    gate = p_sel[:, None] * jax.nn.softmax(top_val, axis=-1)
    expert_idx = g_idx[:, None] * EXPERTS_PER_GROUP + top_i.astype(jnp.int32)

    A = T * TOP_K
    flat_e = expert_idx.reshape(A)
    flat_tok = jnp.repeat(jnp.arange(T, dtype=jnp.int32), TOP_K)
    flat_w = gate.reshape(A)
    order = jnp.argsort(flat_e, stable=True)
    sorted_e = flat_e[order]
    counts = jnp.bincount(flat_e, length=N_EXPERTS).astype(jnp.int32)
    starts = jnp.cumsum(counts) - counts
    padded = ((counts + MOE_BLOCK - 1) // MOE_BLOCK) * MOE_BLOCK
    pad_ends = jnp.cumsum(padded)
    pad_starts = pad_ends - padded
    dest = pad_starts[sorted_e] + (jnp.arange(A, dtype=jnp.int32) - starts[sorted_e])
    n_blocks = -(-A // MOE_BLOCK) + N_EXPERTS
    buf_tok = jnp.full((n_blocks * MOE_BLOCK,), T, dtype=jnp.int32).at[dest].set(flat_tok[order])
    block_start = jnp.arange(n_blocks, dtype=jnp.int32) * MOE_BLOCK
    block_expert = jnp.clip(jnp.searchsorted(pad_ends, block_start, side='right'), 0, N_EXPERTS - 1).astype(jnp.int32)
    h_pad = jnp.concatenate([ht, jnp.zeros((1, D), ht.dtype)], axis=0)
    xb = h_pad[buf_tok].reshape(n_blocks, MOE_BLOCK, D)

    def expert_block(args):
        xblk, e = args
        hid = jax.nn.silu(xblk @ w_expert_gate[e]) * (xblk @ w_expert_up[e])
        return hid @ w_expert_down[e]

    yb = lax.map(expert_block, (xb, block_expert)).reshape(n_blocks * MOE_BLOCK, D)
    contrib = yb[dest] * flat_w[order][:, None]
    out = jax.ops.segment_sum(contrib, flat_tok[order], num_segments=T)
    return out.astype(h.dtype).reshape(B, S, D)


def trunk(x, norm_mix, w_in, gmlp_norm_v, gmlp_w_spatial, gmlp_b_spatial, ret_decay_fwd, ret_decay_bwd,
          ret_norm, w_out, norm_ffn, w_router_group, b_router_group, w_router_expert, b_router_expert,
          w_expert_gate, w_expert_up, w_expert_down, norm_final):
    for l in range(DEPTH):
        h = rms_norm(x, norm_mix[l])
        x = x + token_mix(h, w_in[l], gmlp_norm_v[l], gmlp_w_spatial[l], gmlp_b_spatial[l],
                          ret_decay_fwd[l], ret_decay_bwd[l], ret_norm[l], w_out[l]).astype(x.dtype)
        h = rms_norm(x, norm_ffn[l])
        x = x + hierarchical_moe(h, w_router_group[l], b_router_group[l], w_router_expert[l], b_router_expert[l],
                                 w_expert_gate[l], w_expert_up[l], w_expert_down[l])
    return rms_norm(x, norm_final)


def setup_inputs(seed: int = 0) -> dict:
    key = jax.random.key(seed)
    ks = jax.random.split(key, 24)
    f32 = jnp.float32

    def nrm(k, shape, scale):
        return jax.random.normal(k, shape, f32) * scale

    gamma_ref = 1.0 - 2.0 ** (-5.0 - jnp.arange(RET_HEADS, dtype=f32))
    logit_ref = jnp.log(gamma_ref / (1.0 - gamma_ref))
    return {
        "x_prompt": nrm(ks[0], (BATCH, SEQ, D_MODEL), 1.0),
        "x_sample": nrm(ks[1], (DEC_BATCH, DEC_SEQ, D_MODEL), 1.0),
        "norm_mix": 1.0 + nrm(ks[2], (DEPTH, D_MODEL), 0.02),
        "w_in": nrm(ks[3], (DEPTH, D_MODEL, PROJ_WIDTH), D_MODEL ** -0.5),
        "gmlp_norm_v": 1.0 + nrm(ks[4], (DEPTH, GMLP_HEADS, GMLP_HEAD_DIM), 0.02),
        "gmlp_w_spatial": nrm(ks[5], (DEPTH, GMLP_HEADS, CHUNK, CHUNK), CHUNK ** -0.5),
        "gmlp_b_spatial": 1.0 + nrm(ks[6], (DEPTH, GMLP_HEADS, CHUNK), 0.1),
        "ret_decay_fwd": logit_ref[None, :] + nrm(ks[7], (DEPTH, RET_HEADS), 0.1),
        "ret_decay_bwd": logit_ref[None, :] + nrm(ks[8], (DEPTH, RET_HEADS), 0.1),
        "ret_norm": 1.0 + nrm(ks[9], (DEPTH, RET_HEADS, RET_HEAD_DIM), 0.02),
        "w_out": nrm(ks[10], (DEPTH, D_MIX, D_MODEL), D_MIX ** -0.5),
        "norm_ffn": 1.0 + nrm(ks[11], (DEPTH, D_MODEL), 0.02),
        "w_router_group": nrm(ks[12], (DEPTH, D_MODEL, N_GROUPS), D_MODEL ** -0.5),
        "b_router_group": nrm(ks[13], (DEPTH, N_GROUPS), 0.01),
        "w_router_expert": nrm(ks[14], (DEPTH, D_MODEL, N_EXPERTS), D_MODEL ** -0.5),
        "b_router_expert": nrm(ks[15], (DEPTH, N_EXPERTS), 0.01),
        "w_expert_gate": nrm(ks[16], (DEPTH, N_EXPERTS, D_MODEL, D_EXPERT), D_MODEL ** -0.5),
        "w_expert_up": nrm(ks[17], (DEPTH, N_EXPERTS, D_MODEL, D_EXPERT), D_MODEL ** -0.5),
        "w_expert_down": nrm(ks[18], (DEPTH, N_EXPERTS, D_EXPERT, D_MODEL), D_EXPERT ** -0.5),
        "norm_final": 1.0 + nrm(ks[19], (D_MODEL,), 0.02),
    }


def reference(x_prompt, x_sample, norm_mix, w_in, gmlp_norm_v, gmlp_w_spatial, gmlp_b_spatial, ret_decay_fwd,
              ret_decay_bwd, ret_norm, w_out, norm_ffn, w_router_group, b_router_group, w_router_expert,
              b_router_expert, w_expert_gate, w_expert_up, w_expert_down, norm_final):
    y_prompt = trunk(x_prompt, norm_mix, w_in, gmlp_norm_v, gmlp_w_spatial, gmlp_b_spatial, ret_decay_fwd,
                     ret_decay_bwd, ret_norm, w_out, norm_ffn, w_router_group, b_router_group, w_router_expert,
                     b_router_expert, w_expert_gate, w_expert_up, w_expert_down, norm_final)
    y_sample = trunk(x_sample, norm_mix, w_in, gmlp_norm_v, gmlp_w_spatial, gmlp_b_spatial, ret_decay_fwd,
                     ret_decay_bwd, ret_norm, w_out, norm_ffn, w_router_group, b_router_group, w_router_expert,
                     b_router_expert, w_expert_gate, w_expert_up, w_expert_down, norm_final)
    return (y_prompt, y_sample)
```

```python
import functools
import math

import jax
import jax.numpy as jnp
from jax import lax
from jax.experimental import pallas as pl
from jax.experimental.pallas import tpu as pltpu

F32 = jnp.float32
BF16 = jnp.bfloat16

D_MODEL = 2048
CHUNK = 128
HEADS = 8
HEAD_DIM = 128
SECTION = HEADS * HEAD_DIM
N_SECTIONS = 6
PROJ_WIDTH = N_SECTIONS * SECTION
ROPE_BASE = 10000.0
N_GROUPS = 4
EXPERTS_PER_GROUP = 8
N_EXPERTS = N_GROUPS * EXPERTS_PER_GROUP
D_EXPERT = D_MODEL // 4
EPS = 1e-6
LANES = 128
ROW_TILES = D_MODEL // LANES
EXPERT_BLOCK = 256
VMEM_LIMIT = 56 * 1024 * 1024


def _rms(x, w):
    ms = jnp.mean(x * x, axis=-1, keepdims=True)
    return x * lax.rsqrt(ms + EPS) * w


def _proj_kernel(x_ref, nw_ref, w_ref, cos_ref, sin_ref, gvw_ref, o_ref, h_ref):
    j = pl.program_id(1)

    @pl.when(j == 0)
    def _():
        h_ref[...] = _rms(x_ref[...], nw_ref[...]).astype(BF16)

    acc = jnp.dot(h_ref[...], w_ref[...], preferred_element_type=F32)

    def slabs():
        for g in range(HEADS):
            yield slice(g * HEAD_DIM, (g + 1) * HEAD_DIM)

    def rotary(seg):
        return seg * cos_ref[...] + pltpu.roll(seg, HEAD_DIM // 2, axis=1) * sin_ref[...]

    @pl.when(j == 0)
    def _():
        for sl in slabs():
            o_ref[:, sl] = jax.nn.gelu(acc[:, sl]).astype(BF16)

    @pl.when(j == 1)
    def _():
        for sl in slabs():
            o_ref[:, sl] = _rms(jax.nn.gelu(acc[:, sl]), gvw_ref[:, sl]).astype(BF16)

    @pl.when(j == 2)
    def _():
        for sl in slabs():
            o_ref[:, sl] = rotary(acc[:, sl]).astype(BF16)

    @pl.when(j == 3)
    def _():
        for sl in slabs():
            o_ref[:, sl] = (rotary(acc[:, sl]) * (HEAD_DIM ** -0.5)).astype(BF16)

    @pl.when(j == 4)
    def _():
        o_ref[...] = acc.astype(BF16)

    @pl.when(j == 5)
    def _():
        for sl in slabs():
            o_ref[:, sl] = jax.nn.silu(acc[:, sl]).astype(BF16)


def _proj(x, norm_w, w_in, cos_t, sin_t, gv_w, tm, blocks_p, per_seq_p, per_seq_s):
    t = x.shape[0]

    def pos_map(i, j):
        return (jnp.where(i < blocks_p, i % per_seq_p, i % per_seq_s), 0)

    return pl.pallas_call(
        _proj_kernel,
        grid=(t // tm, N_SECTIONS),
        in_specs=[
            pl.BlockSpec((tm, D_MODEL), lambda i, j: (i, 0)),
            pl.BlockSpec((1, D_MODEL), lambda i, j: (0, 0)),
            pl.BlockSpec((D_MODEL, SECTION), lambda i, j: (0, j)),
            pl.BlockSpec((tm, HEAD_DIM), pos_map),
            pl.BlockSpec((tm, HEAD_DIM), pos_map),
            pl.BlockSpec((1, SECTION), lambda i, j: (0, 0)),
        ],
        out_specs=pl.BlockSpec((tm, SECTION), lambda i, j: (i, j)),
        out_shape=jax.ShapeDtypeStruct((t, PROJ_WIDTH), BF16),
        scratch_shapes=[pltpu.VMEM((tm, D_MODEL), BF16)],
        compiler_params=pltpu.CompilerParams(
            dimension_semantics=("arbitrary", "arbitrary"), vmem_limit_bytes=VMEM_LIMIT),
        name="proj",
    )(x, norm_w, w_in, cos_t, sin_t, gv_w)


def _seq_edge(c, chunks_p, n_p, n_s, last):
    pos = jnp.where(c < chunks_p, c % n_p, (c - chunks_p) % n_s)
    edge = jnp.where(c < chunks_p, n_p - 1, n_s - 1) if last else 0
    return pos == edge


def _states_kernel(lgf_ref, lgb_ref, kf_ref, vf_ref, kb_ref, vb_ref, sf_ref, sb_ref, stf, stb,
                   *, n_chunks, chunks_p, n_p, n_s):
    s = pl.program_id(0)

    @pl.when(_seq_edge(s, chunks_p, n_p, n_s, last=False))
    def _():
        stf[...] = jnp.zeros_like(stf)

    @pl.when(_seq_edge(n_chunks - 1 - s, chunks_p, n_p, n_s, last=True))
    def _():
        stb[...] = jnp.zeros_like(stb)

    row = lax.broadcasted_iota(jnp.int32, (CHUNK, HEAD_DIM), 0).astype(F32)
    tn = (((0,), (0,)), ((), ()))
    for h in range(HEADS):
        sl = slice(h * HEAD_DIM, (h + 1) * HEAD_DIM)
        lgf = lgf_ref[h]
        lgb = lgb_ref[h]
        kd = (kf_ref[:, sl].astype(F32) * jnp.exp((CHUNK - 1.0 - row) * lgf)).astype(BF16)
        kv = lax.dot_general(kd, vf_ref[:, sl], tn, preferred_element_type=F32)
        sf_ref[0, h] = stf[h].astype(BF16)
        stf[h] = stf[h] * jnp.exp(CHUNK * lgf) + kv
        kd = (kb_ref[:, sl].astype(F32) * jnp.exp(row * lgb)).astype(BF16)
        kv = lax.dot_general(kd, vb_ref[:, sl], tn, preferred_element_type=F32)
        sb_ref[0, h] = stb[h].astype(BF16)
        stb[h] = stb[h] * jnp.exp(CHUNK * lgb) + kv


def _states(proj, lgf, lgb, chunks_p, n_p, n_s):
    n_chunks = proj.shape[0] // CHUNK
    last = n_chunks - 1
    smem = pl.BlockSpec(memory_space=pltpu.SMEM)
    st_shape = jax.ShapeDtypeStruct((n_chunks, HEADS, HEAD_DIM, HEAD_DIM), BF16)
    return pl.pallas_call(
        functools.partial(_states_kernel, n_chunks=n_chunks, chunks_p=chunks_p, n_p=n_p, n_s=n_s),
        grid=(n_chunks,),
        in_specs=[
            smem, smem,
            pl.BlockSpec((CHUNK, SECTION), lambda s: (s, 3)),
            pl.BlockSpec((CHUNK, SECTION), lambda s: (s, 4)),
            pl.BlockSpec((CHUNK, SECTION), lambda s: (last - s, 3)),
            pl.BlockSpec((CHUNK, SECTION), lambda s: (last - s, 4)),
        ],
        out_specs=[
            pl.BlockSpec((1, HEADS, HEAD_DIM, HEAD_DIM), lambda s: (s, 0, 0, 0)),
            pl.BlockSpec((1, HEADS, HEAD_DIM, HEAD_DIM), lambda s: (last - s, 0, 0, 0)),
        ],
        out_shape=[st_shape, st_shape],
        scratch_shapes=[pltpu.VMEM((HEADS, HEAD_DIM, HEAD_DIM), F32),
                        pltpu.VMEM((HEADS, HEAD_DIM, HEAD_DIM), F32)],
        compiler_params=pltpu.CompilerParams(dimension_semantics=("arbitrary",)),
        name="states",
    )(lgf, lgb, proj, proj, proj, proj)


def _mix_kernel(lgf_ref, lgb_ref, u_ref, gv_ref, q_ref, k_ref, v_ref, g_ref, sf_ref, sb_ref,
                ws_ref, bs_ref, rn_ref, o_ref, dec_ref, qdf_ref, qdb_ref):
    @pl.when(pl.program_id(0) == 0)
    def _():
        row = lax.broadcasted_iota(jnp.int32, (CHUNK, CHUNK), 0).astype(F32)
        col = lax.broadcasted_iota(jnp.int32, (CHUNK, CHUNK), 1).astype(F32)
        diff = row - col
        for h in range(HEADS):
            lgf = lgf_ref[h]
            lgb = lgb_ref[h]
            dec_ref[h] = jnp.where(diff >= 0, jnp.exp(jnp.maximum(diff, 0.0) * lgf),
                                   jnp.exp(jnp.maximum(-diff, 0.0) * lgb))
            qdf_ref[h] = jnp.exp((row + 1.0) * lgf)
            qdb_ref[h] = jnp.exp((CHUNK - row) * lgb)

    nt = (((1,), (1,)), ((), ()))
    for h in range(HEADS):
        sl = slice(h * HEAD_DIM, (h + 1) * HEAD_DIM)
        mixed = jnp.dot(ws_ref[h], gv_ref[:, sl], preferred_element_type=F32) + bs_ref[h]
        o_ref[:, sl] = (u_ref[:, sl].astype(F32) * mixed).astype(BF16)
        q = q_ref[:, sl]
        v = v_ref[:, sl]
        scores = lax.dot_general(q, k_ref[:, sl], nt, preferred_element_type=F32)
        ret = jnp.dot((scores * dec_ref[h]).astype(BF16), v, preferred_element_type=F32)
        ret += jnp.dot(q, sf_ref[0, h], preferred_element_type=F32) * qdf_ref[h]
        ret += jnp.dot(q, sb_ref[0, h], preferred_element_type=F32) * qdb_ref[h]
        out = _rms(ret, rn_ref[:, sl]) * g_ref[:, sl].astype(F32)
        o_ref[:, SECTION + h * HEAD_DIM:SECTION + (h + 1) * HEAD_DIM] = out.astype(BF16)


def _mix(proj, sf, sb, lgf, lgb, ws, bs, rn):
    t = proj.shape[0]
    smem = pl.BlockSpec(memory_space=pltpu.SMEM)

    def sec(j):
        return pl.BlockSpec((CHUNK, SECTION), lambda c, j=j: (c, j))

    st_spec = pl.BlockSpec((1, HEADS, HEAD_DIM, HEAD_DIM), lambda c: (c, 0, 0, 0))
    tab = pltpu.VMEM((HEADS, CHUNK, CHUNK), F32)
    return pl.pallas_call(
        _mix_kernel,
        grid=(t // CHUNK,),
        in_specs=[
            smem, smem, sec(0), sec(1), sec(2), sec(3), sec(4), sec(5), st_spec, st_spec,
            pl.BlockSpec((HEADS, CHUNK, CHUNK), lambda c: (0, 0, 0)),
            pl.BlockSpec((HEADS, CHUNK, 1), lambda c: (0, 0, 0)),
            pl.BlockSpec((1, SECTION), lambda c: (0, 0)),
        ],
        out_specs=pl.BlockSpec((CHUNK, 2 * SECTION), lambda c: (c, 0)),
        out_shape=jax.ShapeDtypeStruct((t, 2 * SECTION), BF16),
        scratch_shapes=[tab, tab, tab],
        compiler_params=pltpu.CompilerParams(dimension_semantics=("arbitrary",)),
        name="mix",
    )(lgf, lgb, proj, proj, proj, proj, proj, proj, sf, sb, ws, bs, rn)


def _post_kernel(x_ref, m_ref, wo_ref, nw_ref, wr_ref, br_ref, x1_ref, h3_ref, route_ref, cnt_ref,
                 *, tm):
    i = pl.program_id(0)

    @pl.when(i == 0)
    def _():
        cnt_ref[...] = jnp.zeros_like(cnt_ref)

    x1 = x_ref[...] + jnp.dot(m_ref[...], wo_ref[...], preferred_element_type=F32)
    x1_ref[...] = x1
    h = _rms(x1, nw_ref[...])
    for c in range(ROW_TILES):
        h3_ref[pl.ds(c, tm, stride=ROW_TILES), :] = h[:, c * LANES:(c + 1) * LANES]

    logits = jnp.dot(h.astype(BF16), wr_ref[...], preferred_element_type=F32) + br_ref[...]
    lane = lax.broadcasted_iota(jnp.int32, (tm, LANES), 1).astype(F32)
    neg = jnp.float32(-jnp.inf)

    def first_max(vals):
        m = jnp.max(vals, axis=-1, keepdims=True)
        idx = jnp.min(jnp.where(vals == m, lane, float(LANES)), axis=-1, keepdims=True)
        return m, idx

    lg = jnp.where((lane >= N_EXPERTS) & (lane < N_EXPERTS + N_GROUPS), logits, neg)
    gmax, gidx = first_max(lg)
    p_sel = 1.0 / jnp.sum(jnp.exp(lg - gmax), axis=-1, keepdims=True)
    lo = (gidx - N_EXPERTS) * EXPERTS_PER_GROUP
    le = jnp.where((lane >= lo) & (lane < lo + EXPERTS_PER_GROUP), logits, neg)
    top1, i1 = first_max(le)
    top2, i2 = first_max(jnp.where(lane == i1, neg, le))
    e2 = jnp.exp(top2 - top1)
    w1 = p_sel / (1.0 + e2)
    w2 = p_sel * e2 / (1.0 + e2)

    hit1 = lane == i1
    hit2 = lane == i2
    onehot = jnp.where(hit1 | hit2, 1.0, 0.0).astype(BF16)
    r = lax.broadcasted_iota(jnp.int32, (tm, tm), 0)
    c = lax.broadcasted_iota(jnp.int32, (tm, tm), 1)
    lower = jnp.where(c < r, 1.0, 0.0).astype(BF16)
    before = jnp.dot(lower, onehot, preferred_element_type=F32) + cnt_ref[...]
    rank1 = jnp.sum(jnp.where(hit1, before, 0.0), axis=-1, keepdims=True)
    rank2 = jnp.sum(jnp.where(hit2, before, 0.0), axis=-1, keepdims=True)
    cnt_ref[...] += jnp.sum(onehot.astype(F32), axis=0, keepdims=True)

    route = jnp.where(lane == 0, i1, 0.0)
    route = jnp.where(lane == 1, i2, route)
    route = jnp.where(lane == 2, rank1, route)
    route = jnp.where(lane == 3, rank2, route)
    route = jnp.where(lane == 4, w1, route)
    route = jnp.where(lane == 5, w2, route)
    route_ref[...] = route


def _post(x, mixed, w_out, norm_w, wr, br, tm):
    t = x.shape[0]
    return pl.pallas_call(
        functools.partial(_post_kernel, tm=tm),
        grid=(t // tm,),
        in_specs=[
            pl.BlockSpec((tm, D_MODEL), lambda i: (i, 0)),
            pl.BlockSpec((tm, D_MODEL), lambda i: (i, 0)),
            pl.BlockSpec((D_MODEL, D_MODEL), lambda i: (0, 0)),
            pl.BlockSpec((1, D_MODEL), lambda i: (0, 0)),
            pl.BlockSpec((D_MODEL, LANES), lambda i: (0, 0)),
            pl.BlockSpec((1, LANES), lambda i: (0, 0)),
        ],
        out_specs=[
            pl.BlockSpec((tm, D_MODEL), lambda i: (i, 0)),
            pl.BlockSpec((tm * ROW_TILES, LANES), lambda i: (i, 0)),
            pl.BlockSpec((tm, LANES), lambda i: (i, 0)),
            pl.BlockSpec((1, LANES), lambda i: (0, 0)),
        ],
        out_shape=[
            jax.ShapeDtypeStruct((t, D_MODEL), F32),
            jax.ShapeDtypeStruct((t * ROW_TILES, LANES), F32),
            jax.ShapeDtypeStruct((t, LANES), F32),
            jax.ShapeDtypeStruct((1, LANES), F32),
        ],
        compiler_params=pltpu.CompilerParams(
            dimension_semantics=("arbitrary",), vmem_limit_bytes=VMEM_LIMIT),
        name="post",
    )(x, mixed, w_out, norm_w, wr, br)


def _dispatch_kernel(d0_ref, d1_ref, zst_ref, zcnt_ref, h3_ref, xs_ref, zbuf, sem, *, t, window):
    def row_copy(src, dst_row):
        return pltpu.make_async_copy(src, xs_ref.at[dst_row], sem)

    def wait_rows(n):
        def body(_, carry):
            row_copy(h3_ref.at[0], 0).wait()
            return carry
        lax.fori_loop(0, n, body, 0)

    w = pl.program_id(0)

    @pl.when(w == 0)
    def _():
        zbuf[...] = jnp.zeros_like(zbuf)

        def zero_expert(e, carry):
            start = zst_ref[e]

            def body(r, c):
                row_copy(zbuf, start + r).start()
                return c
            lax.fori_loop(0, zcnt_ref[e], body, 0)
            wait_rows(zcnt_ref[e])
            return carry
        lax.fori_loop(0, N_EXPERTS, zero_expert, 0)

    def issue(r, carry):
        tok = w * window + r
        row_copy(h3_ref.at[tok], d0_ref[tok]).start()
        row_copy(h3_ref.at[tok], d1_ref[tok]).start()
        return carry
    lax.fori_loop(0, window, issue, 0)

    @pl.when(w > 0)
    def _():
        wait_rows(2 * window)

    @pl.when(w == t // window - 1)
    def _():
        wait_rows(2 * window)


def _dispatch(h3, dest0, dest1, zstart, zcount, n_rows, window):
    t = dest0.shape[0]
    return pl.pallas_call(
        functools.partial(_dispatch_kernel, t=t, window=window),
        grid_spec=pltpu.PrefetchScalarGridSpec(
            num_scalar_prefetch=4,
            grid=(t // window,),
            in_specs=[pl.BlockSpec(memory_space=pl.ANY)],
            out_specs=pl.BlockSpec(memory_space=pl.ANY),
            scratch_shapes=[pltpu.VMEM((ROW_TILES, LANES), F32), pltpu.SemaphoreType.DMA(())],
        ),
        out_shape=jax.ShapeDtypeStruct((n_rows, ROW_TILES, LANES), F32),
        compiler_params=pltpu.CompilerParams(dimension_semantics=("arbitrary",)),
        name="dispatch",
    )(dest0, dest1, zstart, zcount, h3)


def _experts_kernel(be_ref, nu_ref, xs_ref, wg_ref, wu_ref, wd_ref, y_ref):
    @pl.when(pl.program_id(0) < nu_ref[0])
    def _():
        blk = EXPERT_BLOCK
        x = jnp.concatenate(
            [xs_ref[pl.ds(c, blk, stride=ROW_TILES), :] for c in range(ROW_TILES)], axis=1
        ).astype(BF16)
        gate = jnp.dot(x, wg_ref[0], preferred_element_type=F32)
        up = jnp.dot(x, wu_ref[0], preferred_element_type=F32)
        hid = (jax.nn.silu(gate) * up).astype(BF16)
        y = jnp.dot(hid, wd_ref[0], preferred_element_type=F32)
        for c in range(ROW_TILES):
            y_ref[pl.ds(c, blk, stride=ROW_TILES), :] = y[:, c * LANES:(c + 1) * LANES]


def _experts(xs_flat, block_expert, n_used, wg, wu, wd):
    n_blocks = block_expert.shape[0]
    rows = EXPERT_BLOCK * ROW_TILES

    def blk(b, be, nu):
        return jnp.minimum(b, nu[0] - 1)

    return pl.pallas_call(
        _experts_kernel,
        grid_spec=pltpu.PrefetchScalarGridSpec(
            num_scalar_prefetch=2,
            grid=(n_blocks,),
            in_specs=[
                pl.BlockSpec((rows, LANES), lambda b, be, nu: (blk(b, be, nu), 0)),
                pl.BlockSpec((1, D_MODEL, D_EXPERT), lambda b, be, nu: (be[blk(b, be, nu)], 0, 0)),
                pl.BlockSpec((1, D_MODEL, D_EXPERT), lambda b, be, nu: (be[blk(b, be, nu)], 0, 0)),
                pl.BlockSpec((1, D_EXPERT, D_MODEL), lambda b, be, nu: (be[blk(b, be, nu)], 0, 0)),
            ],
            out_specs=pl.BlockSpec((rows, LANES), lambda b, be, nu: (blk(b, be, nu), 0)),
        ),
        out_shape=jax.ShapeDtypeStruct(xs_flat.shape, F32),
        compiler_params=pltpu.CompilerParams(
            dimension_semantics=("arbitrary",), vmem_limit_bytes=VMEM_LIMIT),
        name="experts",
    )(block_expert, n_used, xs_flat, wg, wu, wd)


def _combine_kernel(d0_ref, d1_ref, x1_ref, route_ref, nw_ref, y_ref, o_ref, buf0, buf1, sem, *, tm):
    base = pl.program_id(0) * tm

    def row_copy(src_row, buf, r):
        return pltpu.make_async_copy(
            y_ref.at[src_row], buf.at[pl.ds(pl.multiple_of(r * ROW_TILES, ROW_TILES), ROW_TILES)], sem)

    def issue(r, carry):
        row_copy(d0_ref[base + r], buf0, r).start()
        row_copy(d1_ref[base + r], buf1, r).start()
        return carry
    lax.fori_loop(0, tm, issue, 0)

    def wait(r, carry):
        row_copy(0, buf0, r).wait()
        row_copy(0, buf1, r).wait()
        return carry
    lax.fori_loop(0, tm, wait, 0)

    def rows(buf):
        return jnp.concatenate(
            [buf[pl.ds(c, tm, stride=ROW_TILES), :] for c in range(ROW_TILES)], axis=1)

    w1 = route_ref[:, 4:5]
    w2 = route_ref[:, 5:6]
    x2 = x1_ref[...] + (rows(buf0) * w1 + rows(buf1) * w2)
    o_ref[...] = _rms(x2, nw_ref[...])


def _combine(x1, route, norm_w, y3, dest0, dest1, tm):
    t = x1.shape[0]
    return pl.pallas_call(
        functools.partial(_combine_kernel, tm=tm),
        grid_spec=pltpu.PrefetchScalarGridSpec(
            num_scalar_prefetch=2,
            grid=(t // tm,),
            in_specs=[
                pl.BlockSpec((tm, D_MODEL), lambda i, d0, d1: (i, 0)),
                pl.BlockSpec((tm, LANES), lambda i, d0, d1: (i, 0)),
                pl.BlockSpec((1, D_MODEL), lambda i, d0, d1: (0, 0)),
                pl.BlockSpec(memory_space=pl.ANY),
            ],
            out_specs=pl.BlockSpec((tm, D_MODEL), lambda i, d0, d1: (i, 0)),
            scratch_shapes=[pltpu.VMEM((tm * ROW_TILES, LANES), F32),
                            pltpu.VMEM((tm * ROW_TILES, LANES), F32),
                            pltpu.SemaphoreType.DMA(())],
        ),
        out_shape=jax.ShapeDtypeStruct((t, D_MODEL), F32),
        compiler_params=pltpu.CompilerParams(dimension_semantics=("arbitrary",)),
        name="combine",
    )(dest0, dest1, x1, route, norm_w, y3)


def _rope_tables(seq_len):
    half = HEAD_DIM // 2
    inv = ROPE_BASE ** (-jnp.arange(half, dtype=F32) / half)
    ang = jnp.arange(seq_len, dtype=F32)[:, None] * inv[None, :]
    cos = jnp.cos(ang)
    sin = jnp.sin(ang)
    return jnp.concatenate([cos, cos], axis=1), jnp.concatenate([-sin, sin], axis=1)


def _tile(limit, *sizes):
    return min(limit, math.gcd(*sizes))


def kernel(x_prompt, x_sample, norm_mix, w_in, gmlp_norm_v, gmlp_w_spatial, gmlp_b_spatial, ret_decay_fwd, ret_decay_bwd, ret_norm, w_out, norm_ffn, w_router_group, b_router_group, w_router_expert, b_router_expert, w_expert_gate, w_expert_up, w_expert_down, norm_final):
    assert norm_mix.shape[0] == 1, "single-layer block"
    bp, sp, d = x_prompt.shape
    bs, ss, _ = x_sample.shape
    assert d == D_MODEL and sp % CHUNK == 0 and ss % CHUNK == 0
    tp, ts = bp * sp, bs * ss
    t = tp + ts
    x = jnp.concatenate([x_prompt.reshape(tp, d), x_sample.reshape(ts, d)], axis=0)

    tm = _tile(1024, sp, ss)
    cos_t, sin_t = _rope_tables(max(sp, ss))
    proj = _proj(x, norm_mix, w_in[0].astype(BF16), cos_t, sin_t,
                 gmlp_norm_v.reshape(1, SECTION), tm, tp // tm, sp // tm, ss // tm)

    lgf = jax.nn.log_sigmoid(ret_decay_fwd[0].astype(F32))
    lgb = jax.nn.log_sigmoid(ret_decay_bwd[0].astype(F32))
    sf, sb = _states(proj, lgf, lgb, tp // CHUNK, sp // CHUNK, ss // CHUNK)
    mixed = _mix(proj, sf, sb, lgf, lgb, gmlp_w_spatial[0].astype(BF16),
                 gmlp_b_spatial[0][:, :, None], ret_norm.reshape(1, SECTION))

    pad = LANES - N_EXPERTS - N_GROUPS
    wr = jnp.concatenate([w_router_expert[0], w_router_group[0], jnp.zeros((d, pad), F32)], axis=1)
    br = jnp.concatenate([b_router_expert[0], b_router_group[0], jnp.zeros((pad,), F32)])[None, :]
    tm = _tile(256, t)
    x1, h3, route, cnt = _post(x, mixed, w_out[0].astype(BF16), norm_ffn, wr.astype(BF16), br, tm)

    counts = cnt[0, :N_EXPERTS].astype(jnp.int32)
    padded = ((counts + EXPERT_BLOCK - 1) // EXPERT_BLOCK) * EXPERT_BLOCK
    pad_ends = jnp.cumsum(padded)
    pad_starts = pad_ends - padded
    e0 = route[:, 0].astype(jnp.int32)
    e1 = route[:, 1].astype(jnp.int32)
    dest0 = pad_starts[e0] + route[:, 2].astype(jnp.int32)
    dest1 = pad_starts[e1] + route[:, 3].astype(jnp.int32)
    n_blocks = -(-(2 * t) // EXPERT_BLOCK) + N_EXPERTS
    block_start = jnp.arange(n_blocks, dtype=jnp.int32) * EXPERT_BLOCK
    block_expert = jnp.clip(jnp.searchsorted(pad_ends, block_start, side='right'),
                            0, N_EXPERTS - 1).astype(jnp.int32)
    n_used = (pad_ends[-1:] // EXPERT_BLOCK).astype(jnp.int32)

    xs = _dispatch(h3.reshape(t, ROW_TILES, LANES), dest0, dest1, pad_starts + counts,
                   padded - counts, n_blocks * EXPERT_BLOCK, _tile(256, t))
    y = _experts(xs.reshape(n_blocks * EXPERT_BLOCK * ROW_TILES, LANES), block_expert, n_used,
                 w_expert_gate[0].astype(BF16), w_expert_up[0].astype(BF16),
                 w_expert_down[0].astype(BF16))
    out = _combine(x1, route, norm_final[None, :], y.reshape(n_blocks * EXPERT_BLOCK, ROW_TILES, LANES),
                   dest0, dest1, _tile(256, t))
    return out[:tp].reshape(bp, sp, d), out[tp:].reshape(bs, ss, d)
```

```python
import functools
import math

import jax
import jax.numpy as jnp
from jax import lax
from jax.experimental import pallas as pl
from jax.experimental.pallas import tpu as pltpu

F32 = jnp.float32
BF16 = jnp.bfloat16

D_MODEL = 2048
CHUNK = 128
HEADS = 8
HEAD_DIM = 128
SECTION = HEADS * HEAD_DIM
N_SECTIONS = 6
PROJ_WIDTH = N_SECTIONS * SECTION
ROPE_BASE = 10000.0
N_GROUPS = 4
EXPERTS_PER_GROUP = 8
N_EXPERTS = N_GROUPS * EXPERTS_PER_GROUP
D_EXPERT = D_MODEL // 4
EPS = 1e-6
LANES = 128
ROW_TILES = D_MODEL // LANES
EXPERT_BLOCK = 256
VMEM_LIMIT = 56 * 1024 * 1024


def _rms(x, w):
    ms = jnp.mean(x * x, axis=-1, keepdims=True)
    return x * lax.rsqrt(ms + EPS) * w


def _proj_kernel(x_ref, nw_ref, w_ref, cos_ref, sin_ref, gvw_ref, o_ref, h_ref):
    j = pl.program_id(1)

    @pl.when(j == 0)
    def _():
        h_ref[...] = _rms(x_ref[...], nw_ref[...]).astype(BF16)

    acc = jnp.dot(h_ref[...], w_ref[...], preferred_element_type=F32)

    def slabs():
        for g in range(HEADS):
            yield slice(g * HEAD_DIM, (g + 1) * HEAD_DIM)

    def rotary(seg):
        return seg * cos_ref[...] + pltpu.roll(seg, HEAD_DIM // 2, axis=1) * sin_ref[...]

    @pl.when(j == 0)
    def _():
        for sl in slabs():
            o_ref[:, sl] = jax.nn.gelu(acc[:, sl]).astype(BF16)

    @pl.when(j == 1)
    def _():
        for sl in slabs():
            o_ref[:, sl] = _rms(jax.nn.gelu(acc[:, sl]), gvw_ref[:, sl]).astype(BF16)

    @pl.when(j == 2)
    def _():
        for sl in slabs():
            o_ref[:, sl] = rotary(acc[:, sl]).astype(BF16)

    @pl.when(j == 3)
    def _():
        for sl in slabs():
            o_ref[:, sl] = (rotary(acc[:, sl]) * (HEAD_DIM ** -0.5)).astype(BF16)

    @pl.when(j == 4)
    def _():
        o_ref[...] = acc.astype(BF16)

    @pl.when(j == 5)
    def _():
        for sl in slabs():
            o_ref[:, sl] = jax.nn.silu(acc[:, sl]).astype(BF16)


def _proj(x, norm_w, w_in, cos_t, sin_t, gv_w, tm, blocks_p, per_seq_p, per_seq_s):
    t = x.shape[0]

    def pos_map(i, j):
        return (jnp.where(i < blocks_p, i % per_seq_p, i % per_seq_s), 0)

    return pl.pallas_call(
        _proj_kernel,
        grid=(t // tm, N_SECTIONS),
        in_specs=[
            pl.BlockSpec((tm, D_MODEL), lambda i, j: (i, 0)),
            pl.BlockSpec((1, D_MODEL), lambda i, j: (0, 0)),
            pl.BlockSpec((D_MODEL, SECTION), lambda i, j: (0, j)),
            pl.BlockSpec((tm, HEAD_DIM), pos_map),
            pl.BlockSpec((tm, HEAD_DIM), pos_map),
            pl.BlockSpec((1, SECTION), lambda i, j: (0, 0)),
        ],
        out_specs=pl.BlockSpec((tm, SECTION), lambda i, j: (i, j)),
        out_shape=jax.ShapeDtypeStruct((t, PROJ_WIDTH), BF16),
        scratch_shapes=[pltpu.VMEM((tm, D_MODEL), BF16)],
        compiler_params=pltpu.CompilerParams(
            dimension_semantics=("arbitrary", "arbitrary"), vmem_limit_bytes=VMEM_LIMIT),
        name="proj",
    )(x, norm_w, w_in, cos_t, sin_t, gv_w)


def _seq_edge(c, chunks_p, n_p, n_s, last):
    pos = jnp.where(c < chunks_p, c % n_p, (c - chunks_p) % n_s)
    edge = jnp.where(c < chunks_p, n_p - 1, n_s - 1) if last else 0
    return pos == edge


def _states_kernel(lgf_ref, lgb_ref, kf_ref, vf_ref, kb_ref, vb_ref, sf_ref, sb_ref, stf, stb,
                   *, n_chunks, chunks_p, n_p, n_s):
    s = pl.program_id(0)

    @pl.when(_seq_edge(s, chunks_p, n_p, n_s, last=False))
    def _():
        stf[...] = jnp.zeros_like(stf)

    @pl.when(_seq_edge(n_chunks - 1 - s, chunks_p, n_p, n_s, last=True))
    def _():
        stb[...] = jnp.zeros_like(stb)

    row = lax.broadcasted_iota(jnp.int32, (CHUNK, HEAD_DIM), 0).astype(F32)
    tn = (((0,), (0,)), ((), ()))
    for h in range(HEADS):
        sl = slice(h * HEAD_DIM, (h + 1) * HEAD_DIM)
        lgf = lgf_ref[h]
        lgb = lgb_ref[h]
        kd = (kf_ref[:, sl].astype(F32) * jnp.exp((CHUNK - 1.0 - row) * lgf)).astype(BF16)
        kv = lax.dot_general(kd, vf_ref[:, sl], tn, preferred_element_type=F32)
        sf_ref[0, h] = stf[h].astype(BF16)
        stf[h] = stf[h] * jnp.exp(CHUNK * lgf) + kv
        kd = (kb_ref[:, sl].astype(F32) * jnp.exp(row * lgb)).astype(BF16)
        kv = lax.dot_general(kd, vb_ref[:, sl], tn, preferred_element_type=F32)
        sb_ref[0, h] = stb[h].astype(BF16)
        stb[h] = stb[h] * jnp.exp(CHUNK * lgb) + kv


def _states(proj, lgf, lgb, chunks_p, n_p, n_s):
    n_chunks = proj.shape[0] // CHUNK
    last = n_chunks - 1
    smem = pl.BlockSpec(memory_space=pltpu.SMEM)
    st_shape = jax.ShapeDtypeStruct((n_chunks, HEADS, HEAD_DIM, HEAD_DIM), BF16)
    return pl.pallas_call(
        functools.partial(_states_kernel, n_chunks=n_chunks, chunks_p=chunks_p, n_p=n_p, n_s=n_s),
        grid=(n_chunks,),
        in_specs=[
            smem, smem,
            pl.BlockSpec((CHUNK, SECTION), lambda s: (s, 3)),
            pl.BlockSpec((CHUNK, SECTION), lambda s: (s, 4)),
            pl.BlockSpec((CHUNK, SECTION), lambda s: (last - s, 3)),
            pl.BlockSpec((CHUNK, SECTION), lambda s: (last - s, 4)),
        ],
        out_specs=[
            pl.BlockSpec((1, HEADS, HEAD_DIM, HEAD_DIM), lambda s: (s, 0, 0, 0)),
            pl.BlockSpec((1, HEADS, HEAD_DIM, HEAD_DIM), lambda s: (last - s, 0, 0, 0)),
        ],
        out_shape=[st_shape, st_shape],
        scratch_shapes=[pltpu.VMEM((HEADS, HEAD_DIM, HEAD_DIM), F32),
                        pltpu.VMEM((HEADS, HEAD_DIM, HEAD_DIM), F32)],
        compiler_params=pltpu.CompilerParams(dimension_semantics=("arbitrary",)),
        name="states",
    )(lgf, lgb, proj, proj, proj, proj)


def _mix_kernel(lgf_ref, lgb_ref, u_ref, gv_ref, q_ref, k_ref, v_ref, g_ref, sf_ref, sb_ref,
                ws_ref, bs_ref, rn_ref, o_ref, dec_ref, qdf_ref, qdb_ref):
    @pl.when(pl.program_id(0) == 0)
    def _():
        row = lax.broadcasted_iota(jnp.int32, (CHUNK, CHUNK), 0).astype(F32)
        col = lax.broadcasted_iota(jnp.int32, (CHUNK, CHUNK), 1).astype(F32)
        diff = row - col
        for h in range(HEADS):
            lgf = lgf_ref[h]
            lgb = lgb_ref[h]
            dec_ref[h] = jnp.where(diff >= 0, jnp.exp(jnp.maximum(diff, 0.0) * lgf),
                                   jnp.exp(jnp.maximum(-diff, 0.0) * lgb))
            qdf_ref[h] = jnp.exp((row + 1.0) * lgf)
            qdb_ref[h] = jnp.exp((CHUNK - row) * lgb)

    nt = (((1,), (1,)), ((), ()))
    for h in range(HEADS):
        sl = slice(h * HEAD_DIM, (h + 1) * HEAD_DIM)
        mixed = jnp.dot(ws_ref[h], gv_ref[:, sl], preferred_element_type=F32) + bs_ref[h]
        o_ref[:, sl] = (u_ref[:, sl].astype(F32) * mixed).astype(BF16)
        q = q_ref[:, sl]
        v = v_ref[:, sl]
        scores = lax.dot_general(q, k_ref[:, sl], nt, preferred_element_type=F32)
        ret = jnp.dot((scores * dec_ref[h]).astype(BF16), v, preferred_element_type=F32)
        ret += jnp.dot(q, sf_ref[0, h], preferred_element_type=F32) * qdf_ref[h]
        ret += jnp.dot(q, sb_ref[0, h], preferred_element_type=F32) * qdb_ref[h]
        out = _rms(ret, rn_ref[:, sl]) * g_ref[:, sl].astype(F32)
        o_ref[:, SECTION + h * HEAD_DIM:SECTION + (h + 1) * HEAD_DIM] = out.astype(BF16)


def _mix(proj, sf, sb, lgf, lgb, ws, bs, rn):
    t = proj.shape[0]
    smem = pl.BlockSpec(memory_space=pltpu.SMEM)

    def sec(j):
        return pl.BlockSpec((CHUNK, SECTION), lambda c, j=j: (c, j))

    st_spec = pl.BlockSpec((1, HEADS, HEAD_DIM, HEAD_DIM), lambda c: (c, 0, 0, 0))
    tab = pltpu.VMEM((HEADS, CHUNK, CHUNK), F32)
    return pl.pallas_call(
        _mix_kernel,
        grid=(t // CHUNK,),
        in_specs=[
            smem, smem, sec(0), sec(1), sec(2), sec(3), sec(4), sec(5), st_spec, st_spec,
            pl.BlockSpec((HEADS, CHUNK, CHUNK), lambda c: (0, 0, 0)),
            pl.BlockSpec((HEADS, CHUNK, 1), lambda c: (0, 0, 0)),
            pl.BlockSpec((1, SECTION), lambda c: (0, 0)),
        ],
        out_specs=pl.BlockSpec((CHUNK, 2 * SECTION), lambda c: (c, 0)),
        out_shape=jax.ShapeDtypeStruct((t, 2 * SECTION), BF16),
        scratch_shapes=[tab, tab, tab],
        compiler_params=pltpu.CompilerParams(dimension_semantics=("arbitrary",)),
        name="mix",
    )(lgf, lgb, proj, proj, proj, proj, proj, proj, sf, sb, ws, bs, rn)


def _post_kernel(x_ref, m_ref, wo_ref, nw_ref, wr_ref, br_ref, x1_ref, h3_ref, route_ref, cnt_ref,
                 *, tm):
    i = pl.program_id(0)

    @pl.when(i == 0)
    def _():
        cnt_ref[...] = jnp.zeros_like(cnt_ref)

    x1 = x_ref[...] + jnp.dot(m_ref[...], wo_ref[...], preferred_element_type=F32)
    x1_ref[...] = x1
    h = _rms(x1, nw_ref[...])
    for c in range(ROW_TILES):
        h3_ref[pl.ds(c, tm, stride=ROW_TILES), :] = h[:, c * LANES:(c + 1) * LANES]

    logits = jnp.dot(h.astype(BF16), wr_ref[...], preferred_element_type=F32) + br_ref[...]
    lane = lax.broadcasted_iota(jnp.int32, (tm, LANES), 1).astype(F32)
    neg = jnp.float32(-jnp.inf)

    def first_max(vals):
        m = jnp.max(vals, axis=-1, keepdims=True)
        idx = jnp.min(jnp.where(vals == m, lane, float(LANES)), axis=-1, keepdims=True)
        return m, idx

    lg = jnp.where((lane >= N_EXPERTS) & (lane < N_EXPERTS + N_GROUPS), logits, neg)
    gmax, gidx = first_max(lg)
    p_sel = 1.0 / jnp.sum(jnp.exp(lg - gmax), axis=-1, keepdims=True)
    lo = (gidx - N_EXPERTS) * EXPERTS_PER_GROUP
    le = jnp.where((lane >= lo) & (lane < lo + EXPERTS_PER_GROUP), logits, neg)
    top1, i1 = first_max(le)
    top2, i2 = first_max(jnp.where(lane == i1, neg, le))
    e2 = jnp.exp(top2 - top1)
    w1 = p_sel / (1.0 + e2)
    w2 = p_sel * e2 / (1.0 + e2)

    hit1 = lane == i1
    hit2 = lane == i2
    onehot = jnp.where(hit1 | hit2, 1.0, 0.0).astype(BF16)
    r = lax.broadcasted_iota(jnp.int32, (tm, tm), 0)
    c = lax.broadcasted_iota(jnp.int32, (tm, tm), 1)
    lower = jnp.where(c < r, 1.0, 0.0).astype(BF16)
    before = jnp.dot(lower, onehot, preferred_element_type=F32) + cnt_ref[...]
    rank1 = jnp.sum(jnp.where(hit1, before, 0.0), axis=-1, keepdims=True)
    rank2 = jnp.sum(jnp.where(hit2, before, 0.0), axis=-1, keepdims=True)
    cnt_ref[...] += jnp.sum(onehot.astype(F32), axis=0, keepdims=True)

    route = jnp.where(lane == 0, i1, 0.0)
    route = jnp.where(lane == 1, i2, route)
    route = jnp.where(lane == 2, rank1, route)
    route = jnp.where(lane == 3, rank2, route)
    route = jnp.where(lane == 4, w1, route)
    route = jnp.where(lane == 5, w2, route)
    route_ref[...] = route


def _post(x, mixed, w_out, norm_w, wr, br, tm):
    t = x.shape[0]
    return pl.pallas_call(
        functools.partial(_post_kernel, tm=tm),
        grid=(t // tm,),
        in_specs=[
            pl.BlockSpec((tm, D_MODEL), lambda i: (i, 0)),
            pl.BlockSpec((tm, D_MODEL), lambda i: (i, 0)),
            pl.BlockSpec((D_MODEL, D_MODEL), lambda i: (0, 0)),
            pl.BlockSpec((1, D_MODEL), lambda i: (0, 0)),
            pl.BlockSpec((D_MODEL, LANES), lambda i: (0, 0)),
            pl.BlockSpec((1, LANES), lambda i: (0, 0)),
        ],
        out_specs=[
            pl.BlockSpec((tm, D_MODEL), lambda i: (i, 0)),
            pl.BlockSpec((tm * ROW_TILES, LANES), lambda i: (i, 0)),
            pl.BlockSpec((tm, LANES), lambda i: (i, 0)),
            pl.BlockSpec((1, LANES), lambda i: (0, 0)),
        ],
        out_shape=[
            jax.ShapeDtypeStruct((t, D_MODEL), F32),
            jax.ShapeDtypeStruct((t * ROW_TILES, LANES), F32),
            jax.ShapeDtypeStruct((t, LANES), F32),
            jax.ShapeDtypeStruct((1, LANES), F32),
        ],
        compiler_params=pltpu.CompilerParams(
            dimension_semantics=("arbitrary",), vmem_limit_bytes=VMEM_LIMIT),
        name="post",
    )(x, mixed, w_out, norm_w, wr, br)


def _dispatch_kernel(d0_ref, d1_ref, zst_ref, zcnt_ref, h3_ref, xs_ref, zbuf, sem, *, window):
    def row_copy(src, dst_row):
        return pltpu.make_async_copy(src, xs_ref.at[dst_row], sem)

    def wait_rows(n):
        def body(_, carry):
            row_copy(zbuf, 0).wait()
            return carry
        lax.fori_loop(0, n, body, 0)

    w = pl.program_id(0)

    @pl.when(w == 0)
    def _():
        zbuf[...] = jnp.zeros_like(zbuf)

        def zero_expert(e, carry):
            start = zst_ref[e]

            def body(r, c):
                row_copy(zbuf, start + r).start()
                return c
            lax.fori_loop(0, zcnt_ref[e], body, 0)
            wait_rows(zcnt_ref[e])
            return carry
        lax.fori_loop(0, N_EXPERTS, zero_expert, 0)

    def issue(r, carry):
        tok = w * window + r
        src = h3_ref.at[pl.ds(pl.multiple_of(r * ROW_TILES, ROW_TILES), ROW_TILES)]
        row_copy(src, d0_ref[tok]).start()
        row_copy(src, d1_ref[tok]).start()
        return carry
    lax.fori_loop(0, window, issue, 0)
    wait_rows(2 * window)


def _dispatch(h3_flat, dest0, dest1, zstart, zcount, n_rows, window):
    t = dest0.shape[0]
    return pl.pallas_call(
        functools.partial(_dispatch_kernel, window=window),
        grid_spec=pltpu.PrefetchScalarGridSpec(
            num_scalar_prefetch=4,
            grid=(t // window,),
            in_specs=[pl.BlockSpec((window * ROW_TILES, LANES), lambda w, *_: (w, 0))],
            out_specs=pl.BlockSpec(memory_space=pl.ANY),
            scratch_shapes=[pltpu.VMEM((ROW_TILES, LANES), F32), pltpu.SemaphoreType.DMA(())],
        ),
        out_shape=jax.ShapeDtypeStruct((n_rows, ROW_TILES, LANES), F32),
        compiler_params=pltpu.CompilerParams(dimension_semantics=("arbitrary",)),
        name="dispatch",
    )(dest0, dest1, zstart, zcount, h3_flat)


def _experts_kernel(be_ref, nu_ref, xs_ref, wg_ref, wu_ref, wd_ref, y_ref):
    @pl.when(pl.program_id(0) < nu_ref[0])
    def _():
        blk = EXPERT_BLOCK
        x = jnp.concatenate(
            [xs_ref[pl.ds(c, blk, stride=ROW_TILES), :] for c in range(ROW_TILES)], axis=1
        ).astype(BF16)
        gate = jnp.dot(x, wg_ref[0], preferred_element_type=F32)
        up = jnp.dot(x, wu_ref[0], preferred_element_type=F32)
        hid = (jax.nn.silu(gate) * up).astype(BF16)
        y = jnp.dot(hid, wd_ref[0], preferred_element_type=F32)
        for c in range(ROW_TILES):
            y_ref[pl.ds(c, blk, stride=ROW_TILES), :] = y[:, c * LANES:(c + 1) * LANES]


def _experts(xs_flat, block_expert, n_used, wg, wu, wd):
    n_blocks = block_expert.shape[0]
    rows = EXPERT_BLOCK * ROW_TILES

    def blk(b, be, nu):
        return jnp.minimum(b, nu[0] - 1)

    return pl.pallas_call(
        _experts_kernel,
        grid_spec=pltpu.PrefetchScalarGridSpec(
            num_scalar_prefetch=2,
            grid=(n_blocks,),
            in_specs=[
                pl.BlockSpec((rows, LANES), lambda b, be, nu: (blk(b, be, nu), 0)),
                pl.BlockSpec((1, D_MODEL, D_EXPERT), lambda b, be, nu: (be[blk(b, be, nu)], 0, 0)),
                pl.BlockSpec((1, D_MODEL, D_EXPERT), lambda b, be, nu: (be[blk(b, be, nu)], 0, 0)),
                pl.BlockSpec((1, D_EXPERT, D_MODEL), lambda b, be, nu: (be[blk(b, be, nu)], 0, 0)),
            ],
            out_specs=pl.BlockSpec((rows, LANES), lambda b, be, nu: (blk(b, be, nu), 0)),
        ),
        out_shape=jax.ShapeDtypeStruct(xs_flat.shape, F32),
        compiler_params=pltpu.CompilerParams(
            dimension_semantics=("arbitrary",), vmem_limit_bytes=VMEM_LIMIT),
        name="experts",
    )(block_expert, n_used, xs_flat, wg, wu, wd)


def _combine_kernel(d0_ref, d1_ref, x1_ref, route_ref, nw_ref, y_ref, o_ref, buf0, buf1, sem, *, tm):
    base = pl.program_id(0) * tm

    def row_copy(src_row, buf, r):
        return pltpu.make_async_copy(
            y_ref.at[src_row], buf.at[pl.ds(pl.multiple_of(r * ROW_TILES, ROW_TILES), ROW_TILES)], sem)

    def issue(r, carry):
        row_copy(d0_ref[base + r], buf0, r).start()
        row_copy(d1_ref[base + r], buf1, r).start()
        return carry
    lax.fori_loop(0, tm, issue, 0)

    def wait(r, carry):
        row_copy(0, buf0, r).wait()
        row_copy(0, buf1, r).wait()
        return carry
    lax.fori_loop(0, tm, wait, 0)

    def rows(buf):
        return jnp.concatenate(
            [buf[pl.ds(c, tm, stride=ROW_TILES), :] for c in range(ROW_TILES)], axis=1)

    w1 = route_ref[:, 4:5]
    w2 = route_ref[:, 5:6]
    x2 = x1_ref[...] + (rows(buf0) * w1 + rows(buf1) * w2)
    o_ref[...] = _rms(x2, nw_ref[...])


def _combine(x1, route, norm_w, y3, dest0, dest1, tm):
    t = x1.shape[0]
    return pl.pallas_call(
        functools.partial(_combine_kernel, tm=tm),
        grid_spec=pltpu.PrefetchScalarGridSpec(
            num_scalar_prefetch=2,
            grid=(t // tm,),
            in_specs=[
                pl.BlockSpec((tm, D_MODEL), lambda i, d0, d1: (i, 0)),
                pl.BlockSpec((tm, LANES), lambda i, d0, d1: (i, 0)),
                pl.BlockSpec((1, D_MODEL), lambda i, d0, d1: (0, 0)),
                pl.BlockSpec(memory_space=pl.ANY),
            ],
            out_specs=pl.BlockSpec((tm, D_MODEL), lambda i, d0, d1: (i, 0)),
            scratch_shapes=[pltpu.VMEM((tm * ROW_TILES, LANES), F32),
                            pltpu.VMEM((tm * ROW_TILES, LANES), F32),
                            pltpu.SemaphoreType.DMA(())],
        ),
        out_shape=jax.ShapeDtypeStruct((t, D_MODEL), F32),
        compiler_params=pltpu.CompilerParams(dimension_semantics=("arbitrary",)),
        name="combine",
    )(dest0, dest1, x1, route, norm_w, y3)


def _rope_tables(seq_len):
    half = HEAD_DIM // 2
    inv = ROPE_BASE ** (-jnp.arange(half, dtype=F32) / half)
    ang = jnp.arange(seq_len, dtype=F32)[:, None] * inv[None, :]
    cos = jnp.cos(ang)
    sin = jnp.sin(ang)
    return jnp.concatenate([cos, cos], axis=1), jnp.concatenate([-sin, sin], axis=1)


def _tile(limit, *sizes):
    return min(limit, math.gcd(*sizes))


def kernel(x_prompt, x_sample, norm_mix, w_in, gmlp_norm_v, gmlp_w_spatial, gmlp_b_spatial, ret_decay_fwd, ret_decay_bwd, ret_norm, w_out, norm_ffn, w_router_group, b_router_group, w_router_expert, b_router_expert, w_expert_gate, w_expert_up, w_expert_down, norm_final):
    assert norm_mix.shape[0] == 1, "single-layer block"
    bp, sp, d = x_prompt.shape
    bs, ss, _ = x_sample.shape
    assert d == D_MODEL and sp % CHUNK == 0 and ss % CHUNK == 0
    tp, ts = bp * sp, bs * ss
    t = tp + ts
    x = jnp.concatenate([x_prompt.reshape(tp, d), x_sample.reshape(ts, d)], axis=0)

    tm = _tile(1024, sp, ss)
    cos_t, sin_t = _rope_tables(max(sp, ss))
    proj = _proj(x, norm_mix, w_in[0].astype(BF16), cos_t, sin_t,
                 gmlp_norm_v.reshape(1, SECTION), tm, tp // tm, sp // tm, ss // tm)

    lgf = jax.nn.log_sigmoid(ret_decay_fwd[0].astype(F32))
    lgb = jax.nn.log_sigmoid(ret_decay_bwd[0].astype(F32))
    sf, sb = _states(proj, lgf, lgb, tp // CHUNK, sp // CHUNK, ss // CHUNK)
    mixed = _mix(proj, sf, sb, lgf, lgb, gmlp_w_spatial[0].astype(BF16),
                 gmlp_b_spatial[0][:, :, None], ret_norm.reshape(1, SECTION))

    pad = LANES - N_EXPERTS - N_GROUPS
    wr = jnp.concatenate([w_router_expert[0], w_router_group[0], jnp.zeros((d, pad), F32)], axis=1)
    br = jnp.concatenate([b_router_expert[0], b_router_group[0], jnp.zeros((pad,), F32)])[None, :]
    tm = _tile(256, t)
    x1, h3, route, cnt = _post(x, mixed, w_out[0].astype(BF16), norm_ffn, wr.astype(BF16), br, tm)

    counts = cnt[0, :N_EXPERTS].astype(jnp.int32)
    padded = ((counts + EXPERT_BLOCK - 1) // EXPERT_BLOCK) * EXPERT_BLOCK
    pad_ends = jnp.cumsum(padded)
    pad_starts = pad_ends - padded
    e0 = route[:, 0].astype(jnp.int32)
    e1 = route[:, 1].astype(jnp.int32)
    dest0 = pad_starts[e0] + route[:, 2].astype(jnp.int32)
    dest1 = pad_starts[e1] + route[:, 3].astype(jnp.int32)
    n_blocks = -(-(2 * t) // EXPERT_BLOCK) + N_EXPERTS
    block_start = jnp.arange(n_blocks, dtype=jnp.int32) * EXPERT_BLOCK
    block_expert = jnp.clip(jnp.searchsorted(pad_ends, block_start, side='right'),
                            0, N_EXPERTS - 1).astype(jnp.int32)
    n_used = (pad_ends[-1:] // EXPERT_BLOCK).astype(jnp.int32)

    xs = _dispatch(h3, dest0, dest1, pad_starts + counts, padded - counts,
                   n_blocks * EXPERT_BLOCK, _tile(512, t))
    y = _experts(xs.reshape(n_blocks * EXPERT_BLOCK * ROW_TILES, LANES), block_expert, n_used,
                 w_expert_gate[0].astype(BF16), w_expert_up[0].astype(BF16),
                 w_expert_down[0].astype(BF16))
    out = _combine(x1, route, norm_final[None, :], y.reshape(n_blocks * EXPERT_BLOCK, ROW_TILES, LANES),
                   dest0, dest1, _tile(256, t))
    return out[:tp].reshape(bp, sp, d), out[tp:].reshape(bs, ss, d)
```

```python
import functools
import math

import jax
import jax.numpy as jnp
from jax import lax
from jax.experimental import pallas as pl
from jax.experimental.pallas import tpu as pltpu

F32 = jnp.float32
BF16 = jnp.bfloat16
I32 = jnp.int32

D_MODEL = 2048
CHUNK = 128
HEADS = 8
HEAD_DIM = 128
SECTION = HEADS * HEAD_DIM
N_SECTIONS = 6
PROJ_WIDTH = N_SECTIONS * SECTION
ROPE_BASE = 10000.0
N_GROUPS = 4
EXPERTS_PER_GROUP = 8
N_EXPERTS = N_GROUPS * EXPERTS_PER_GROUP
TOP_K = 2
D_EXPERT = D_MODEL // 4
EPS = 1e-6
LANES = 128
SUBLANES = 8
ROW_TILES = D_MODEL // LANES
EXPERT_BLOCK = 256
DMA_UNROLL = 8
VMEM_LIMIT = 56 * 1024 * 1024


def _rms(x, w):
    ms = jnp.mean(x * x, axis=-1, keepdims=True)
    return x * lax.rsqrt(ms + EPS) * w


def _proj_kernel(x_ref, nw_ref, w_ref, cos_ref, sin_ref, gvw_ref, prev_ref, o_ref, h_ref):
    del prev_ref
    j = pl.program_id(1)

    @pl.when(j == 0)
    def _():
        h_ref[...] = _rms(x_ref[...], nw_ref[...]).astype(BF16)

    acc = jnp.dot(h_ref[...], w_ref[...], preferred_element_type=F32)

    def slabs():
        for g in range(HEADS):
            yield slice(g * HEAD_DIM, (g + 1) * HEAD_DIM)

    def rotary(seg):
        return seg * cos_ref[...] + pltpu.roll(seg, HEAD_DIM // 2, axis=1) * sin_ref[...]

    @pl.when(j == 0)
    def _():
        for sl in slabs():
            o_ref[:, sl] = jax.nn.gelu(acc[:, sl]).astype(BF16)

    @pl.when(j == 1)
    def _():
        for sl in slabs():
            o_ref[:, sl] = _rms(jax.nn.gelu(acc[:, sl]), gvw_ref[:, sl]).astype(BF16)

    @pl.when(j == 2)
    def _():
        for sl in slabs():
            o_ref[:, sl] = rotary(acc[:, sl]).astype(BF16)

    @pl.when(j == 3)
    def _():
        for sl in slabs():
            o_ref[:, sl] = (rotary(acc[:, sl]) * (HEAD_DIM ** -0.5)).astype(BF16)

    @pl.when(j == 4)
    def _():
        o_ref[...] = acc.astype(BF16)

    @pl.when(j == 5)
    def _():
        for sl in slabs():
            o_ref[:, sl] = jax.nn.silu(acc[:, sl]).astype(BF16)


def _proj(x, norm_w, w_in, cos_t, sin_t, gv_w, prev, t_total, row_block0, tm, per_seq):
    if prev is None:
        prev = jnp.zeros((SUBLANES, LANES), BF16)
        aliases = {}
    else:
        aliases = {6: 0}
    return pl.pallas_call(
        _proj_kernel,
        grid=(x.shape[0] // tm, N_SECTIONS),
        in_specs=[
            pl.BlockSpec((tm, D_MODEL), lambda i, j: (i, 0)),
            pl.BlockSpec((1, D_MODEL), lambda i, j: (0, 0)),
            pl.BlockSpec((D_MODEL, SECTION), lambda i, j: (0, j)),
            pl.BlockSpec((tm, HEAD_DIM), lambda i, j: (i % per_seq, 0)),
            pl.BlockSpec((tm, HEAD_DIM), lambda i, j: (i % per_seq, 0)),
            pl.BlockSpec((1, SECTION), lambda i, j: (0, 0)),
            pl.BlockSpec(memory_space=pl.ANY),
        ],
        out_specs=pl.BlockSpec((tm, SECTION), lambda i, j: (i + row_block0, j)),
        out_shape=jax.ShapeDtypeStruct((t_total, PROJ_WIDTH), BF16),
        scratch_shapes=[pltpu.VMEM((tm, D_MODEL), BF16)],
        input_output_aliases=aliases,
        compiler_params=pltpu.CompilerParams(
            dimension_semantics=("arbitrary", "arbitrary"), vmem_limit_bytes=VMEM_LIMIT),
        name="proj",
    )(x, norm_w, w_in, cos_t, sin_t, gv_w, prev)


def _seq_edge(c, chunks_p, n_p, n_s, last):
    pos = jnp.where(c < chunks_p, c % n_p, (c - chunks_p) % n_s)
    edge = jnp.where(c < chunks_p, n_p - 1, n_s - 1) if last else 0
    return pos == edge


def _states_kernel(lgf_ref, lgb_ref, kf_ref, vf_ref, kb_ref, vb_ref, sf_ref, sb_ref, stf, stb,
                   *, n_chunks, chunks_p, n_p, n_s):
    s = pl.program_id(0)

    @pl.when(_seq_edge(s, chunks_p, n_p, n_s, last=False))
    def _():
        stf[...] = jnp.zeros_like(stf)

    @pl.when(_seq_edge(n_chunks - 1 - s, chunks_p, n_p, n_s, last=True))
    def _():
        stb[...] = jnp.zeros_like(stb)

    row = lax.broadcasted_iota(jnp.int32, (CHUNK, HEAD_DIM), 0).astype(F32)
    tn = (((0,), (0,)), ((), ()))
    for h in range(HEADS):
        sl = slice(h * HEAD_DIM, (h + 1) * HEAD_DIM)
        lgf = lgf_ref[h]
        lgb = lgb_ref[h]
        kd = (kf_ref[:, sl].astype(F32) * jnp.exp((CHUNK - 1.0 - row) * lgf)).astype(BF16)
        kv = lax.dot_general(kd, vf_ref[:, sl], tn, preferred_element_type=F32)
        sf_ref[0, h] = stf[h].astype(BF16)
        stf[h] = stf[h] * jnp.exp(CHUNK * lgf) + kv
        kd = (kb_ref[:, sl].astype(F32) * jnp.exp(row * lgb)).astype(BF16)
        kv = lax.dot_general(kd, vb_ref[:, sl], tn, preferred_element_type=F32)
        sb_ref[0, h] = stb[h].astype(BF16)
        stb[h] = stb[h] * jnp.exp(CHUNK * lgb) + kv


def _states(proj, lgf, lgb, chunks_p, n_p, n_s):
    n_chunks = proj.shape[0] // CHUNK
    last = n_chunks - 1
    smem = pl.BlockSpec(memory_space=pltpu.SMEM)
    st_shape = jax.ShapeDtypeStruct((n_chunks, HEADS, HEAD_DIM, HEAD_DIM), BF16)
    return pl.pallas_call(
        functools.partial(_states_kernel, n_chunks=n_chunks, chunks_p=chunks_p, n_p=n_p, n_s=n_s),
        grid=(n_chunks,),
        in_specs=[
            smem, smem,
            pl.BlockSpec((CHUNK, SECTION), lambda s: (s, 3)),
            pl.BlockSpec((CHUNK, SECTION), lambda s: (s, 4)),
            pl.BlockSpec((CHUNK, SECTION), lambda s: (last - s, 3)),
            pl.BlockSpec((CHUNK, SECTION), lambda s: (last - s, 4)),
        ],
        out_specs=[
            pl.BlockSpec((1, HEADS, HEAD_DIM, HEAD_DIM), lambda s: (s, 0, 0, 0)),
            pl.BlockSpec((1, HEADS, HEAD_DIM, HEAD_DIM), lambda s: (last - s, 0, 0, 0)),
        ],
        out_shape=[st_shape, st_shape],
        scratch_shapes=[pltpu.VMEM((HEADS, HEAD_DIM, HEAD_DIM), F32),
                        pltpu.VMEM((HEADS, HEAD_DIM, HEAD_DIM), F32)],
        compiler_params=pltpu.CompilerParams(dimension_semantics=("arbitrary",)),
        name="states",
    )(lgf, lgb, proj, proj, proj, proj)


def _mix_kernel(lgf_ref, lgb_ref, u_ref, gv_ref, q_ref, k_ref, v_ref, g_ref, sf_ref, sb_ref,
                ws_ref, bs_ref, rn_ref, o_ref, dec_ref, qdf_ref, qdb_ref):
    @pl.when(pl.program_id(0) == 0)
    def _():
        row = lax.broadcasted_iota(jnp.int32, (CHUNK, CHUNK), 0).astype(F32)
        col = lax.broadcasted_iota(jnp.int32, (CHUNK, CHUNK), 1).astype(F32)
        diff = row - col
        for h in range(HEADS):
            lgf = lgf_ref[h]
            lgb = lgb_ref[h]
            dec_ref[h] = jnp.where(diff >= 0, jnp.exp(jnp.maximum(diff, 0.0) * lgf),
                                   jnp.exp(jnp.maximum(-diff, 0.0) * lgb))
            qdf_ref[h] = jnp.exp((row + 1.0) * lgf)
            qdb_ref[h] = jnp.exp((CHUNK - row) * lgb)

    nt = (((1,), (1,)), ((), ()))
    for h in range(HEADS):
        sl = slice(h * HEAD_DIM, (h + 1) * HEAD_DIM)
        mixed = jnp.dot(ws_ref[h], gv_ref[:, sl], preferred_element_type=F32) + bs_ref[h]
        o_ref[:, sl] = (u_ref[:, sl].astype(F32) * mixed).astype(BF16)
        q = q_ref[:, sl]
        v = v_ref[:, sl]
        scores = lax.dot_general(q, k_ref[:, sl], nt, preferred_element_type=F32)
        ret = jnp.dot((scores * dec_ref[h]).astype(BF16), v, preferred_element_type=F32)
        ret += jnp.dot(q, sf_ref[0, h], preferred_element_type=F32) * qdf_ref[h]
        ret += jnp.dot(q, sb_ref[0, h], preferred_element_type=F32) * qdb_ref[h]
        out = _rms(ret, rn_ref[:, sl]) * g_ref[:, sl].astype(F32)
        o_ref[:, SECTION + h * HEAD_DIM:SECTION + (h + 1) * HEAD_DIM] = out.astype(BF16)


def _mix(proj, sf, sb, lgf, lgb, ws, bs, rn):
    t = proj.shape[0]
    smem = pl.BlockSpec(memory_space=pltpu.SMEM)

    def sec(j):
        return pl.BlockSpec((CHUNK, SECTION), lambda c, j=j: (c, j))

    st_spec = pl.BlockSpec((1, HEADS, HEAD_DIM, HEAD_DIM), lambda c: (c, 0, 0, 0))
    tab = pltpu.VMEM((HEADS, CHUNK, CHUNK), F32)
    return pl.pallas_call(
        _mix_kernel,
        grid=(t // CHUNK,),
        in_specs=[
            smem, smem, sec(0), sec(1), sec(2), sec(3), sec(4), sec(5), st_spec, st_spec,
            pl.BlockSpec((HEADS, CHUNK, CHUNK), lambda c: (0, 0, 0)),
            pl.BlockSpec((HEADS, CHUNK, 1), lambda c: (0, 0, 0)),
            pl.BlockSpec((1, SECTION), lambda c: (0, 0)),
        ],
        out_specs=pl.BlockSpec((CHUNK, 2 * SECTION), lambda c: (c, 0)),
        out_shape=jax.ShapeDtypeStruct((t, 2 * SECTION), BF16),
        scratch_shapes=[tab, tab, tab],
        compiler_params=pltpu.CompilerParams(dimension_semantics=("arbitrary",)),
        name="mix",
    )(lgf, lgb, proj, proj, proj, proj, proj, proj, sf, sb, ws, bs, rn)


def _post_kernel(xp_ref, xs_ref, m_ref, wo_ref, nw_ref, wr_ref, br_ref,
                 x1_ref, h3_ref, route_ref, rt_ref, cnt_ref, *, tm, blocks_p):
    i = pl.program_id(0)

    @pl.when(i == 0)
    def _():
        cnt_ref[...] = jnp.zeros_like(cnt_ref)

    x = jnp.where(i < blocks_p, xp_ref[...], xs_ref[...])
    x1 = x + jnp.dot(m_ref[...], wo_ref[...], preferred_element_type=F32)
    x1_ref[...] = x1
    h = _rms(x1, nw_ref[...])
    for c in range(ROW_TILES):
        h3_ref[:, c, :] = h[:, c * LANES:(c + 1) * LANES]

    logits = jnp.dot(h.astype(BF16), wr_ref[...], preferred_element_type=F32) + br_ref[...]
    lane = lax.broadcasted_iota(jnp.int32, (tm, LANES), 1).astype(F32)
    neg = jnp.float32(-jnp.inf)

    def first_max(vals):
        m = jnp.max(vals, axis=-1, keepdims=True)
        idx = jnp.min(jnp.where(vals == m, lane, float(LANES)), axis=-1, keepdims=True)
        return m, idx

    lg = jnp.where((lane >= N_EXPERTS) & (lane < N_EXPERTS + N_GROUPS), logits, neg)
    gmax, gidx = first_max(lg)
    p_sel = 1.0 / jnp.sum(jnp.exp(lg - gmax), axis=-1, keepdims=True)
    lo = (gidx - N_EXPERTS) * EXPERTS_PER_GROUP
    le = jnp.where((lane >= lo) & (lane < lo + EXPERTS_PER_GROUP), logits, neg)
    top1, i1 = first_max(le)
    top2, i2 = first_max(jnp.where(lane == i1, neg, le))
    e2 = jnp.exp(top2 - top1)
    w1 = p_sel / (1.0 + e2)
    w2 = p_sel * e2 / (1.0 + e2)

    hit1 = lane == i1
    hit2 = lane == i2
    onehot = jnp.where(hit1 | hit2, 1.0, 0.0).astype(BF16)
    r = lax.broadcasted_iota(jnp.int32, (tm, tm), 0)
    c = lax.broadcasted_iota(jnp.int32, (tm, tm), 1)
    lower = jnp.where(c < r, 1.0, 0.0).astype(BF16)
    before = jnp.dot(lower, onehot, preferred_element_type=F32) + cnt_ref[...]
    rank1 = jnp.sum(jnp.where(hit1, before, 0.0), axis=-1, keepdims=True)
    rank2 = jnp.sum(jnp.where(hit2, before, 0.0), axis=-1, keepdims=True)
    cnt_ref[...] += jnp.sum(onehot.astype(F32), axis=0, keepdims=True)

    route = jnp.where(lane == 0, i1, 0.0)
    route = jnp.where(lane == 1, i2, route)
    route = jnp.where(lane == 2, rank1, route)
    route = jnp.where(lane == 3, rank2, route)
    route = jnp.where(lane == 4, w1, route)
    route = jnp.where(lane == 5, w2, route)
    route_ref[...] = route
    rt_ref[...] = jnp.transpose(route)[0:SUBLANES, :]


def _post(xp, xs, mixed, w_out, norm_w, wr, br, tm):
    t = mixed.shape[0]
    blocks_p = xp.shape[0] // tm
    return pl.pallas_call(
        functools.partial(_post_kernel, tm=tm, blocks_p=blocks_p),
        grid=(t // tm,),
        in_specs=[
            pl.BlockSpec((tm, D_MODEL), lambda i: (jnp.minimum(i, blocks_p - 1), 0)),
            pl.BlockSpec((tm, D_MODEL), lambda i: (jnp.maximum(i - blocks_p, 0), 0)),
            pl.BlockSpec((tm, D_MODEL), lambda i: (i, 0)),
            pl.BlockSpec((D_MODEL, D_MODEL), lambda i: (0, 0)),
            pl.BlockSpec((1, D_MODEL), lambda i: (0, 0)),
            pl.BlockSpec((D_MODEL, LANES), lambda i: (0, 0)),
            pl.BlockSpec((1, LANES), lambda i: (0, 0)),
        ],
        out_specs=[
            pl.BlockSpec((tm, D_MODEL), lambda i: (i, 0)),
            pl.BlockSpec((tm, ROW_TILES, LANES), lambda i: (i, 0, 0)),
            pl.BlockSpec((tm, LANES), lambda i: (i, 0)),
            pl.BlockSpec((SUBLANES, tm), lambda i: (0, i)),
            pl.BlockSpec((1, LANES), lambda i: (0, 0)),
        ],
        out_shape=[
            jax.ShapeDtypeStruct((t, D_MODEL), F32),
            jax.ShapeDtypeStruct((t, ROW_TILES, LANES), F32),
            jax.ShapeDtypeStruct((t, LANES), F32),
            jax.ShapeDtypeStruct((SUBLANES, t), F32),
            jax.ShapeDtypeStruct((1, LANES), F32),
        ],
        compiler_params=pltpu.CompilerParams(
            dimension_semantics=("arbitrary",), vmem_limit_bytes=VMEM_LIMIT),
        name="post",
    )(xp, xs, mixed, w_out, norm_w, wr, br)


def _plan_kernel(cnt_ref, rt_ref, dest_ref, be_ref, nu_ref, ps_ref, *, n_blocks):
    def per_expert(e, first):
        n = (cnt_ref[e] + (EXPERT_BLOCK - 1)) // EXPERT_BLOCK
        ps_ref[e] = first * EXPERT_BLOCK

        def fill(j, carry):
            be_ref[first + j] = e
            return carry
        lax.fori_loop(0, n, fill, 0)
        return first + n
    n_used = lax.fori_loop(0, N_EXPERTS, per_expert, 0)
    nu_ref[0] = n_used

    def fill_tail(j, carry):
        be_ref[j] = N_EXPERTS - 1
        return carry
    lax.fori_loop(n_used, n_blocks, fill_tail, 0)

    for k in range(TOP_K):
        e = rt_ref[k:k + 1, :]
        base = jnp.zeros_like(e)
        for x in range(N_EXPERTS):
            base = jnp.where(e == float(x), ps_ref[x].astype(F32), base)
        dest_ref[k:k + 1, :] = (base + rt_ref[TOP_K + k:TOP_K + k + 1, :]).astype(I32)


def _plan(counts, route_t, n_blocks):
    t = route_t.shape[1]
    smem = pl.BlockSpec(memory_space=pltpu.SMEM)
    return pl.pallas_call(
        functools.partial(_plan_kernel, n_blocks=n_blocks),
        grid=(1,),
        in_specs=[smem, pl.BlockSpec((SUBLANES, t), lambda i: (0, 0))],
        out_specs=[pl.BlockSpec((TOP_K, t), lambda i: (0, 0)), smem, smem],
        out_shape=[
            jax.ShapeDtypeStruct((TOP_K, t), I32),
            jax.ShapeDtypeStruct((n_blocks,), I32),
            jax.ShapeDtypeStruct((1,), I32),
        ],
        scratch_shapes=[pltpu.SMEM((N_EXPERTS,), I32)],
        compiler_params=pltpu.CompilerParams(dimension_semantics=("arbitrary",)),
        name="plan",
    )(counts, route_t)


def _invert_kernel(dest_ref, slot_ref, *, n_assign, n_slots):
    def init(i, carry):
        for u in range(DMA_UNROLL):
            slot_ref[i * DMA_UNROLL + u] = n_assign
        return carry
    lax.fori_loop(0, n_slots // DMA_UNROLL, init, 0)

    def fill(i, carry):
        for u in range(DMA_UNROLL):
            a = i * DMA_UNROLL + u
            slot_ref[dest_ref[a]] = a
        return carry
    lax.fori_loop(0, n_assign // DMA_UNROLL, fill, 0)


def _invert(dest_flat, n_slots):
    n_assign = dest_flat.shape[0]
    return pl.pallas_call(
        functools.partial(_invert_kernel, n_assign=n_assign, n_slots=n_slots),
        grid_spec=pltpu.PrefetchScalarGridSpec(
            num_scalar_prefetch=1,
            grid=(1,),
            in_specs=[],
            out_specs=pl.BlockSpec(memory_space=pltpu.SMEM),
        ),
        out_shape=jax.ShapeDtypeStruct((n_slots,), I32),
        compiler_params=pltpu.CompilerParams(dimension_semantics=("arbitrary",)),
        name="invert",
    )(dest_flat)


def _experts_kernel(be_ref, nu_ref, slot_ref, h3_ref, wg_ref, wu_ref, wd_ref, y_ref,
                    xbuf, ybuf, wgb, wub, wdb, gsem, ssem, *, n_tok):
    b = pl.program_id(0)
    n_used = nu_ref[0]
    blk = EXPERT_BLOCK

    def gather_copy(tok, slot, r):
        return pltpu.make_async_copy(h3_ref.at[tok], xbuf.at[slot, r], gsem.at[slot])

    def scatter_copy(dst, slot, r):
        return pltpu.make_async_copy(ybuf.at[slot, r], y_ref.at[dst], ssem.at[slot])

    def rows(fn):
        def body(i, carry):
            for u in range(DMA_UNROLL):
                fn(i * DMA_UNROLL + u)
            return carry
        lax.fori_loop(0, blk // DMA_UNROLL, body, 0)

    def gather_start(block, slot):
        def one(r):
            a = slot_ref[block * blk + r]
            tok = a - jnp.where(a >= n_tok, n_tok, 0) - jnp.where(a >= TOP_K * n_tok, n_tok, 0)
            gather_copy(tok, slot, r).start()
        rows(one)

    def scatter_start(block, slot):
        def one(r):
            a = slot_ref[block * blk + r]
            dst = jnp.where(a >= TOP_K * n_tok, TOP_K * n_tok + slot * blk + r, a)
            scatter_copy(dst, slot, r).start()
        rows(one)

    def gather_wait(slot):
        rows(lambda r: gather_copy(0, slot, r).wait())

    def scatter_wait(slot):
        rows(lambda r: scatter_copy(0, slot, r).wait())

    @pl.when(b < n_used)
    def _():
        slot = b % 2

        @pl.when(b == 0)
        def _():
            gather_start(0, 0)

        @pl.when(b + 1 < n_used)
        def _():
            gather_start(b + 1, 1 - slot)

        @pl.when((b == 0) | (be_ref[b] != be_ref[jnp.maximum(b - 1, 0)]))
        def _():
            wgb[...] = wg_ref[0].astype(BF16)
            wub[...] = wu_ref[0].astype(BF16)
            wdb[...] = wd_ref[0].astype(BF16)

        gather_wait(slot)

        @pl.when(b >= 2)
        def _():
            scatter_wait(slot)

        x = jnp.concatenate([xbuf[slot, :, c, :] for c in range(ROW_TILES)], axis=1).astype(BF16)
        gate = jnp.dot(x, wgb[...], preferred_element_type=F32)
        up = jnp.dot(x, wub[...], preferred_element_type=F32)
        hid = (jax.nn.silu(gate) * up).astype(BF16)
        y = jnp.dot(hid, wdb[...], preferred_element_type=F32)
        for c in range(ROW_TILES):
            ybuf[slot, :, c, :] = y[:, c * LANES:(c + 1) * LANES]
        scatter_start(b, slot)

        @pl.when(b == n_used - 1)
        def _():
            scatter_wait(slot)

            @pl.when(b >= 1)
            def _():
                scatter_wait(1 - slot)


def _experts(h3, block_expert, n_used, slot_a, wg, wu, wd):
    n_blocks = block_expert.shape[0]
    n_tok = h3.shape[0]

    def wmap(b, be, nu, sl):
        return (be[jnp.minimum(b, nu[0] - 1)], 0, 0)

    buf = pltpu.VMEM((2, EXPERT_BLOCK, ROW_TILES, LANES), F32)
    return pl.pallas_call(
        functools.partial(_experts_kernel, n_tok=n_tok),
        grid_spec=pltpu.PrefetchScalarGridSpec(
            num_scalar_prefetch=3,
            grid=(n_blocks,),
            in_specs=[
                pl.BlockSpec(memory_space=pl.ANY),
                pl.BlockSpec((1, D_MODEL, D_EXPERT), wmap),
                pl.BlockSpec((1, D_MODEL, D_EXPERT), wmap),
                pl.BlockSpec((1, D_EXPERT, D_MODEL), wmap),
            ],
            out_specs=pl.BlockSpec(memory_space=pl.ANY),
            scratch_shapes=[
                buf, buf,
                pltpu.VMEM((D_MODEL, D_EXPERT), BF16),
                pltpu.VMEM((D_MODEL, D_EXPERT), BF16),
                pltpu.VMEM((D_EXPERT, D_MODEL), BF16),
                pltpu.SemaphoreType.DMA((2,)),
                pltpu.SemaphoreType.DMA((2,)),
            ],
        ),
        out_shape=jax.ShapeDtypeStruct((TOP_K * n_tok + 2 * EXPERT_BLOCK, ROW_TILES, LANES), F32),
        compiler_params=pltpu.CompilerParams(
            dimension_semantics=("arbitrary",), vmem_limit_bytes=VMEM_LIMIT),
        name="experts",
    )(block_expert, n_used, slot_a, h3, wg, wu, wd)


def _final_kernel(x1_ref, route_ref, nw_ref, y0_ref, y1_ref, op_ref, os_ref, *, blocks_p):
    i = pl.program_id(0)

    def rows(ref):
        return jnp.concatenate([ref[:, c, :] for c in range(ROW_TILES)], axis=1)

    w1 = route_ref[:, 4:5]
    w2 = route_ref[:, 5:6]
    x2 = x1_ref[...] + (rows(y0_ref) * w1 + rows(y1_ref) * w2)
    out = _rms(x2, nw_ref[...])

    @pl.when(i < blocks_p)
    def _():
        op_ref[...] = out

    @pl.when(i >= blocks_p)
    def _():
        os_ref[...] = out


def _final(x1, route, norm_w, y, tp, tm):
    t = x1.shape[0]
    blocks_p = tp // tm
    blocks_t = t // tm
    return pl.pallas_call(
        functools.partial(_final_kernel, blocks_p=blocks_p),
        grid=(blocks_t,),
        in_specs=[
            pl.BlockSpec((tm, D_MODEL), lambda i: (i, 0)),
            pl.BlockSpec((tm, LANES), lambda i: (i, 0)),
            pl.BlockSpec((1, D_MODEL), lambda i: (0, 0)),
            pl.BlockSpec((tm, ROW_TILES, LANES), lambda i: (i, 0, 0)),
            pl.BlockSpec((tm, ROW_TILES, LANES), lambda i: (i + blocks_t, 0, 0)),
        ],
        out_specs=[
            pl.BlockSpec((tm, D_MODEL), lambda i: (jnp.minimum(i, blocks_p - 1), 0)),
            pl.BlockSpec((tm, D_MODEL), lambda i: (jnp.maximum(i - blocks_p, 0), 0)),
        ],
        out_shape=[
            jax.ShapeDtypeStruct((tp, D_MODEL), F32),
            jax.ShapeDtypeStruct((t - tp, D_MODEL), F32),
        ],
        compiler_params=pltpu.CompilerParams(dimension_semantics=("arbitrary",)),
        name="final",
    )(x1, route, norm_w, y, y)


def _rope_tables(seq_len):
    half = HEAD_DIM // 2
    inv = ROPE_BASE ** (-jnp.arange(half, dtype=F32) / half)
    ang = jnp.arange(seq_len, dtype=F32)[:, None] * inv[None, :]
    cos = jnp.cos(ang)
    sin = jnp.sin(ang)
    return jnp.concatenate([cos, cos], axis=1), jnp.concatenate([-sin, sin], axis=1)


def _tile(limit, *sizes):
    return min(limit, math.gcd(*sizes))


def kernel(x_prompt, x_sample, norm_mix, w_in, gmlp_norm_v, gmlp_w_spatial, gmlp_b_spatial, ret_decay_fwd, ret_decay_bwd, ret_norm, w_out, norm_ffn, w_router_group, b_router_group, w_router_expert, b_router_expert, w_expert_gate, w_expert_up, w_expert_down, norm_final):
    assert norm_mix.shape[0] == 1, "single-layer block"
    bp, sp, d = x_prompt.shape
    bs, ss, _ = x_sample.shape
    assert d == D_MODEL and sp % CHUNK == 0 and ss % CHUNK == 0
    tp, ts = bp * sp, bs * ss
    t = tp + ts
    xp = x_prompt.reshape(tp, d)
    xs = x_sample.reshape(ts, d)

    tm = _tile(1024, sp, ss)
    cos_t, sin_t = _rope_tables(max(sp, ss))
    w_in_b = w_in[0].astype(BF16)
    gv_w = gmlp_norm_v.reshape(1, SECTION)
    proj = _proj(xp, norm_mix, w_in_b, cos_t, sin_t, gv_w, None, t, 0, tm, sp // tm)
    proj = _proj(xs, norm_mix, w_in_b, cos_t, sin_t, gv_w, proj, t, tp // tm, tm, ss // tm)

    lgf = jax.nn.log_sigmoid(ret_decay_fwd[0].astype(F32))
    lgb = jax.nn.log_sigmoid(ret_decay_bwd[0].astype(F32))
    sf, sb = _states(proj, lgf, lgb, tp // CHUNK, sp // CHUNK, ss // CHUNK)
    mixed = _mix(proj, sf, sb, lgf, lgb, gmlp_w_spatial[0].astype(BF16),
                 gmlp_b_spatial[0][:, :, None], ret_norm.reshape(1, SECTION))

    pad = LANES - N_EXPERTS - N_GROUPS
    wr = jnp.concatenate([w_router_expert[0], w_router_group[0], jnp.zeros((d, pad), F32)], axis=1)
    br = jnp.concatenate([b_router_expert[0], b_router_group[0], jnp.zeros((pad,), F32)])[None, :]
    tm = _tile(256, tp, ts)
    x1, h3, route, route_t, cnt = _post(xp, xs, mixed, w_out[0].astype(BF16), norm_ffn,
                                        wr.astype(BF16), br, tm)

    n_blocks = -(-(TOP_K * t) // EXPERT_BLOCK) + N_EXPERTS
    dest, block_expert, n_used = _plan(cnt[0, :N_EXPERTS].astype(I32), route_t, n_blocks)
    slot_a = _invert(dest.reshape(TOP_K * t), n_blocks * EXPERT_BLOCK)
    y = _experts(h3, block_expert, n_used, slot_a, w_expert_gate[0], w_expert_up[0], w_expert_down[0])

    out_p, out_s = _final(x1, route, norm_final[None, :], y, tp, tm)
    return out_p.reshape(bp, sp, d), out_s.reshape(bs, ss, d)
```

```python
import functools
import math

import jax
import jax.numpy as jnp
from jax import lax
from jax.experimental import pallas as pl
from jax.experimental.pallas import tpu as pltpu

F32 = jnp.float32
BF16 = jnp.bfloat16
I32 = jnp.int32

D_MODEL = 2048
CHUNK = 128
HEADS = 8
HEAD_DIM = 128
SECTION = HEADS * HEAD_DIM
N_SECTIONS = 6
PROJ_WIDTH = N_SECTIONS * SECTION
ROPE_BASE = 10000.0
N_GROUPS = 4
EXPERTS_PER_GROUP = 8
N_EXPERTS = N_GROUPS * EXPERTS_PER_GROUP
TOP_K = 2
D_EXPERT = D_MODEL // 4
EPS = 1e-6
LANES = 128
SUBLANES = 8
ROW_TILES = D_MODEL // LANES
EXPERT_BLOCK = 256
DMA_UNROLL = 8
VMEM_LIMIT = 56 * 1024 * 1024


def _rms(x, w):
    ms = jnp.mean(x * x, axis=-1, keepdims=True)
    return x * lax.rsqrt(ms + EPS) * w


def _proj_kernel(x_ref, nw_ref, w_ref, cos_ref, sin_ref, gvw_ref, prev_ref, o_ref, h_ref):
    del prev_ref
    j = pl.program_id(1)

    @pl.when(j == 0)
    def _():
        h_ref[...] = _rms(x_ref[...], nw_ref[...]).astype(BF16)

    acc = jnp.dot(h_ref[...], w_ref[...], preferred_element_type=F32)

    def slabs():
        for g in range(HEADS):
            yield slice(g * HEAD_DIM, (g + 1) * HEAD_DIM)

    def rotary(seg):
        return seg * cos_ref[...] + pltpu.roll(seg, HEAD_DIM // 2, axis=1) * sin_ref[...]

    @pl.when(j == 0)
    def _():
        for sl in slabs():
            o_ref[:, sl] = jax.nn.gelu(acc[:, sl]).astype(BF16)

    @pl.when(j == 1)
    def _():
        for sl in slabs():
            o_ref[:, sl] = _rms(jax.nn.gelu(acc[:, sl]), gvw_ref[:, sl]).astype(BF16)

    @pl.when(j == 2)
    def _():
        for sl in slabs():
            o_ref[:, sl] = rotary(acc[:, sl]).astype(BF16)

    @pl.when(j == 3)
    def _():
        for sl in slabs():
            o_ref[:, sl] = (rotary(acc[:, sl]) * (HEAD_DIM ** -0.5)).astype(BF16)

    @pl.when(j == 4)
    def _():
        o_ref[...] = acc.astype(BF16)

    @pl.when(j == 5)
    def _():
        for sl in slabs():
            o_ref[:, sl] = jax.nn.silu(acc[:, sl]).astype(BF16)


def _proj(x, norm_w, w_in, cos_t, sin_t, gv_w, prev, t_total, row_block0, tm, per_seq):
    if prev is None:
        prev = jnp.zeros((SUBLANES, LANES), BF16)
        aliases = {}
    else:
        aliases = {6: 0}
    return pl.pallas_call(
        _proj_kernel,
        grid=(x.shape[0] // tm, N_SECTIONS),
        in_specs=[
            pl.BlockSpec((tm, D_MODEL), lambda i, j: (i, 0)),
            pl.BlockSpec((1, D_MODEL), lambda i, j: (0, 0)),
            pl.BlockSpec((D_MODEL, SECTION), lambda i, j: (0, j)),
            pl.BlockSpec((tm, HEAD_DIM), lambda i, j: (i % per_seq, 0)),
            pl.BlockSpec((tm, HEAD_DIM), lambda i, j: (i % per_seq, 0)),
            pl.BlockSpec((1, SECTION), lambda i, j: (0, 0)),
            pl.BlockSpec(memory_space=pl.ANY),
        ],
        out_specs=pl.BlockSpec((tm, SECTION), lambda i, j: (i + row_block0, j)),
        out_shape=jax.ShapeDtypeStruct((t_total, PROJ_WIDTH), BF16),
        scratch_shapes=[pltpu.VMEM((tm, D_MODEL), BF16)],
        input_output_aliases=aliases,
        compiler_params=pltpu.CompilerParams(
            dimension_semantics=("arbitrary", "arbitrary"), vmem_limit_bytes=VMEM_LIMIT),
        name="proj",
    )(x, norm_w, w_in, cos_t, sin_t, gv_w, prev)


def _seq_edge(c, chunks_p, n_p, n_s, last):
    pos = jnp.where(c < chunks_p, c % n_p, (c - chunks_p) % n_s)
    edge = jnp.where(c < chunks_p, n_p - 1, n_s - 1) if last else 0
    return pos == edge


def _states_kernel(lgf_ref, lgb_ref, kf_ref, vf_ref, kb_ref, vb_ref, sf_ref, sb_ref, stf, stb,
                   *, n_chunks, chunks_p, n_p, n_s):
    s = pl.program_id(0)

    @pl.when(_seq_edge(s, chunks_p, n_p, n_s, last=False))
    def _():
        stf[...] = jnp.zeros_like(stf)

    @pl.when(_seq_edge(n_chunks - 1 - s, chunks_p, n_p, n_s, last=True))
    def _():
        stb[...] = jnp.zeros_like(stb)

    row = lax.broadcasted_iota(jnp.int32, (CHUNK, HEAD_DIM), 0).astype(F32)
    tn = (((0,), (0,)), ((), ()))
    for h in range(HEADS):
        sl = slice(h * HEAD_DIM, (h + 1) * HEAD_DIM)
        lgf = lgf_ref[h]
        lgb = lgb_ref[h]
        kd = (kf_ref[:, sl].astype(F32) * jnp.exp((CHUNK - 1.0 - row) * lgf)).astype(BF16)
        kv = lax.dot_general(kd, vf_ref[:, sl], tn, preferred_element_type=F32)
        sf_ref[0, h] = stf[h].astype(BF16)
        stf[h] = stf[h] * jnp.exp(CHUNK * lgf) + kv
        kd = (kb_ref[:, sl].astype(F32) * jnp.exp(row * lgb)).astype(BF16)
        kv = lax.dot_general(kd, vb_ref[:, sl], tn, preferred_element_type=F32)
        sb_ref[0, h] = stb[h].astype(BF16)
        stb[h] = stb[h] * jnp.exp(CHUNK * lgb) + kv


def _states(proj, lgf, lgb, chunks_p, n_p, n_s):
    n_chunks = proj.shape[0] // CHUNK
    last = n_chunks - 1
    smem = pl.BlockSpec(memory_space=pltpu.SMEM)
    st_shape = jax.ShapeDtypeStruct((n_chunks, HEADS, HEAD_DIM, HEAD_DIM), BF16)
    return pl.pallas_call(
        functools.partial(_states_kernel, n_chunks=n_chunks, chunks_p=chunks_p, n_p=n_p, n_s=n_s),
        grid=(n_chunks,),
        in_specs=[
            smem, smem,
            pl.BlockSpec((CHUNK, SECTION), lambda s: (s, 3)),
            pl.BlockSpec((CHUNK, SECTION), lambda s: (s, 4)),
            pl.BlockSpec((CHUNK, SECTION), lambda s: (last - s, 3)),
            pl.BlockSpec((CHUNK, SECTION), lambda s: (last - s, 4)),
        ],
        out_specs=[
            pl.BlockSpec((1, HEADS, HEAD_DIM, HEAD_DIM), lambda s: (s, 0, 0, 0)),
            pl.BlockSpec((1, HEADS, HEAD_DIM, HEAD_DIM), lambda s: (last - s, 0, 0, 0)),
        ],
        out_shape=[st_shape, st_shape],
        scratch_shapes=[pltpu.VMEM((HEADS, HEAD_DIM, HEAD_DIM), F32),
                        pltpu.VMEM((HEADS, HEAD_DIM, HEAD_DIM), F32)],
        compiler_params=pltpu.CompilerParams(dimension_semantics=("arbitrary",)),
        name="states",
    )(lgf, lgb, proj, proj, proj, proj)


def _mix_kernel(lgf_ref, lgb_ref, u_ref, gv_ref, q_ref, k_ref, v_ref, g_ref, sf_ref, sb_ref,
                ws_ref, bs_ref, rn_ref, o_ref, dec_ref, qdf_ref, qdb_ref):
    @pl.when(pl.program_id(0) == 0)
    def _():
        row = lax.broadcasted_iota(jnp.int32, (CHUNK, CHUNK), 0).astype(F32)
        col = lax.broadcasted_iota(jnp.int32, (CHUNK, CHUNK), 1).astype(F32)
        diff = row - col
        for h in range(HEADS):
            lgf = lgf_ref[h]
            lgb = lgb_ref[h]
            dec_ref[h] = jnp.where(diff >= 0, jnp.exp(jnp.maximum(diff, 0.0) * lgf),
                                   jnp.exp(jnp.maximum(-diff, 0.0) * lgb))
            qdf_ref[h] = jnp.exp((row + 1.0) * lgf)
            qdb_ref[h] = jnp.exp((CHUNK - row) * lgb)

    nt = (((1,), (1,)), ((), ()))
    for h in range(HEADS):
        sl = slice(h * HEAD_DIM, (h + 1) * HEAD_DIM)
        mixed = jnp.dot(ws_ref[h], gv_ref[:, sl], preferred_element_type=F32) + bs_ref[h]
        o_ref[:, sl] = (u_ref[:, sl].astype(F32) * mixed).astype(BF16)
        q = q_ref[:, sl]
        v = v_ref[:, sl]
        scores = lax.dot_general(q, k_ref[:, sl], nt, preferred_element_type=F32)
        ret = jnp.dot((scores * dec_ref[h]).astype(BF16), v, preferred_element_type=F32)
        ret += jnp.dot(q, sf_ref[0, h], preferred_element_type=F32) * qdf_ref[h]
        ret += jnp.dot(q, sb_ref[0, h], preferred_element_type=F32) * qdb_ref[h]
        out = _rms(ret, rn_ref[:, sl]) * g_ref[:, sl].astype(F32)
        o_ref[:, SECTION + h * HEAD_DIM:SECTION + (h + 1) * HEAD_DIM] = out.astype(BF16)


def _mix(proj, sf, sb, lgf, lgb, ws, bs, rn):
    t = proj.shape[0]
    smem = pl.BlockSpec(memory_space=pltpu.SMEM)

    def sec(j):
        return pl.BlockSpec((CHUNK, SECTION), lambda c, j=j: (c, j))

    st_spec = pl.BlockSpec((1, HEADS, HEAD_DIM, HEAD_DIM), lambda c: (c, 0, 0, 0))
    tab = pltpu.VMEM((HEADS, CHUNK, CHUNK), F32)
    return pl.pallas_call(
        _mix_kernel,
        grid=(t // CHUNK,),
        in_specs=[
            smem, smem, sec(0), sec(1), sec(2), sec(3), sec(4), sec(5), st_spec, st_spec,
            pl.BlockSpec((HEADS, CHUNK, CHUNK), lambda c: (0, 0, 0)),
            pl.BlockSpec((HEADS, CHUNK, 1), lambda c: (0, 0, 0)),
            pl.BlockSpec((1, SECTION), lambda c: (0, 0)),
        ],
        out_specs=pl.BlockSpec((CHUNK, 2 * SECTION), lambda c: (c, 0)),
        out_shape=jax.ShapeDtypeStruct((t, 2 * SECTION), BF16),
        scratch_shapes=[tab, tab, tab],
        compiler_params=pltpu.CompilerParams(dimension_semantics=("arbitrary",)),
        name="mix",
    )(lgf, lgb, proj, proj, proj, proj, proj, proj, sf, sb, ws, bs, rn)


def _post_kernel(xp_ref, xs_ref, m_ref, wo_ref, nw_ref, wr_ref, br_ref,
                 x1_ref, h3_ref, route_ref, rt_ref, cnt_ref, *, tm, blocks_p):
    i = pl.program_id(0)

    @pl.when(i == 0)
    def _():
        cnt_ref[...] = jnp.zeros_like(cnt_ref)

    x = jnp.where(i < blocks_p, xp_ref[...], xs_ref[...])
    x1 = x + jnp.dot(m_ref[...], wo_ref[...], preferred_element_type=F32)
    x1_ref[...] = x1
    h = _rms(x1, nw_ref[...])
    for c in range(ROW_TILES):
        h3_ref[pl.ds(c, tm, stride=ROW_TILES), :] = h[:, c * LANES:(c + 1) * LANES]

    logits = jnp.dot(h.astype(BF16), wr_ref[...], preferred_element_type=F32) + br_ref[...]
    lane = lax.broadcasted_iota(jnp.int32, (tm, LANES), 1).astype(F32)
    neg = jnp.float32(-jnp.inf)

    def first_max(vals):
        m = jnp.max(vals, axis=-1, keepdims=True)
        idx = jnp.min(jnp.where(vals == m, lane, float(LANES)), axis=-1, keepdims=True)
        return m, idx

    lg = jnp.where((lane >= N_EXPERTS) & (lane < N_EXPERTS + N_GROUPS), logits, neg)
    gmax, gidx = first_max(lg)
    p_sel = 1.0 / jnp.sum(jnp.exp(lg - gmax), axis=-1, keepdims=True)
    lo = (gidx - N_EXPERTS) * EXPERTS_PER_GROUP
    le = jnp.where((lane >= lo) & (lane < lo + EXPERTS_PER_GROUP), logits, neg)
    top1, i1 = first_max(le)
    top2, i2 = first_max(jnp.where(lane == i1, neg, le))
    e2 = jnp.exp(top2 - top1)
    w1 = p_sel / (1.0 + e2)
    w2 = p_sel * e2 / (1.0 + e2)

    hit1 = lane == i1
    hit2 = lane == i2
    onehot = jnp.where(hit1 | hit2, 1.0, 0.0).astype(BF16)
    r = lax.broadcasted_iota(jnp.int32, (tm, tm), 0)
    c = lax.broadcasted_iota(jnp.int32, (tm, tm), 1)
    lower = jnp.where(c < r, 1.0, 0.0).astype(BF16)
    before = jnp.dot(lower, onehot, preferred_element_type=F32) + cnt_ref[...]
    rank1 = jnp.sum(jnp.where(hit1, before, 0.0), axis=-1, keepdims=True)
    rank2 = jnp.sum(jnp.where(hit2, before, 0.0), axis=-1, keepdims=True)
    cnt_ref[...] += jnp.sum(onehot.astype(F32), axis=0, keepdims=True)

    route = jnp.where(lane == 0, i1, 0.0)
    route = jnp.where(lane == 1, i2, route)
    route = jnp.where(lane == 2, rank1, route)
    route = jnp.where(lane == 3, rank2, route)
    route = jnp.where(lane == 4, w1, route)
    route = jnp.where(lane == 5, w2, route)
    route_ref[...] = route
    rt_ref[...] = jnp.transpose(route)[0:SUBLANES, :]


def _post(xp, xs, mixed, w_out, norm_w, wr, br, tm):
    t = mixed.shape[0]
    blocks_p = xp.shape[0] // tm
    return pl.pallas_call(
        functools.partial(_post_kernel, tm=tm, blocks_p=blocks_p),
        grid=(t // tm,),
        in_specs=[
            pl.BlockSpec((tm, D_MODEL), lambda i: (jnp.minimum(i, blocks_p - 1), 0)),
            pl.BlockSpec((tm, D_MODEL), lambda i: (jnp.maximum(i - blocks_p, 0), 0)),
            pl.BlockSpec((tm, D_MODEL), lambda i: (i, 0)),
            pl.BlockSpec((D_MODEL, D_MODEL), lambda i: (0, 0)),
            pl.BlockSpec((1, D_MODEL), lambda i: (0, 0)),
            pl.BlockSpec((D_MODEL, LANES), lambda i: (0, 0)),
            pl.BlockSpec((1, LANES), lambda i: (0, 0)),
        ],
        out_specs=[
            pl.BlockSpec((tm, D_MODEL), lambda i: (i, 0)),
            pl.BlockSpec((tm * ROW_TILES, LANES), lambda i: (i, 0)),
            pl.BlockSpec((tm, LANES), lambda i: (i, 0)),
            pl.BlockSpec((SUBLANES, tm), lambda i: (0, i)),
            pl.BlockSpec((1, LANES), lambda i: (0, 0)),
        ],
        out_shape=[
            jax.ShapeDtypeStruct((t, D_MODEL), F32),
            jax.ShapeDtypeStruct((t * ROW_TILES, LANES), F32),
            jax.ShapeDtypeStruct((t, LANES), F32),
            jax.ShapeDtypeStruct((SUBLANES, t), F32),
            jax.ShapeDtypeStruct((1, LANES), F32),
        ],
        compiler_params=pltpu.CompilerParams(
            dimension_semantics=("arbitrary",), vmem_limit_bytes=VMEM_LIMIT),
        name="post",
    )(xp, xs, mixed, w_out, norm_w, wr, br)


def _plan_kernel(cnt_ref, rt_ref, dest_ref, be_ref, nu_ref, zst_ref, zcn_ref, ps_ref, *, n_blocks):
    def per_expert(e, first):
        n = (cnt_ref[e] + (EXPERT_BLOCK - 1)) // EXPERT_BLOCK
        ps_ref[e] = first * EXPERT_BLOCK
        zst_ref[e] = first * EXPERT_BLOCK + cnt_ref[e]
        zcn_ref[e] = n * EXPERT_BLOCK - cnt_ref[e]

        def fill(j, carry):
            be_ref[first + j] = e
            return carry
        lax.fori_loop(0, n, fill, 0)
        return first + n
    n_used = lax.fori_loop(0, N_EXPERTS, per_expert, 0)
    nu_ref[0] = n_used

    def fill_tail(j, carry):
        be_ref[j] = N_EXPERTS - 1
        return carry
    lax.fori_loop(n_used, n_blocks, fill_tail, 0)

    for k in range(TOP_K):
        e = rt_ref[k:k + 1, :]
        base = jnp.zeros_like(e)
        for x in range(N_EXPERTS):
            base = jnp.where(e == float(x), ps_ref[x].astype(F32), base)
        dest_ref[k:k + 1, :] = (base + rt_ref[TOP_K + k:TOP_K + k + 1, :]).astype(I32)


def _plan(counts, route_t, n_blocks):
    t = route_t.shape[1]
    smem = pl.BlockSpec(memory_space=pltpu.SMEM)
    return pl.pallas_call(
        functools.partial(_plan_kernel, n_blocks=n_blocks),
        grid=(1,),
        in_specs=[smem, pl.BlockSpec((SUBLANES, t), lambda i: (0, 0))],
        out_specs=[pl.BlockSpec((TOP_K, t), lambda i: (0, 0)), smem, smem, smem, smem],
        out_shape=[
            jax.ShapeDtypeStruct((TOP_K, t), I32),
            jax.ShapeDtypeStruct((n_blocks,), I32),
            jax.ShapeDtypeStruct((1,), I32),
            jax.ShapeDtypeStruct((N_EXPERTS,), I32),
            jax.ShapeDtypeStruct((N_EXPERTS,), I32),
        ],
        scratch_shapes=[pltpu.SMEM((N_EXPERTS,), I32)],
        compiler_params=pltpu.CompilerParams(dimension_semantics=("arbitrary",)),
        name="plan",
    )(counts, route_t)


def _invert_kernel(dest_ref, zst_ref, zcn_ref, tok_ref, row_ref, *, n_tok):
    n_assign = TOP_K * n_tok

    def pad_expert(e, carry):
        start = zst_ref[e]

        def one(j, c):
            p = start + j
            tok_ref[p] = 0
            row_ref[p] = n_assign + (p & (2 * EXPERT_BLOCK - 1))
            return c
        lax.fori_loop(0, zcn_ref[e], one, 0)
        return carry
    lax.fori_loop(0, N_EXPERTS, pad_expert, 0)

    for k in range(TOP_K):
        def fill(i, carry, k=k):
            for u in range(DMA_UNROLL):
                tok = i * DMA_UNROLL + u
                p = dest_ref[k * n_tok + tok]
                tok_ref[p] = tok
                row_ref[p] = k * n_tok + tok
            return carry
        lax.fori_loop(0, n_tok // DMA_UNROLL, fill, 0)


def _invert(dest_flat, pad_start, pad_count, n_tok, n_slots):
    assert EXPERT_BLOCK & (EXPERT_BLOCK - 1) == 0
    smem = pl.BlockSpec(memory_space=pltpu.SMEM)
    return pl.pallas_call(
        functools.partial(_invert_kernel, n_tok=n_tok),
        grid_spec=pltpu.PrefetchScalarGridSpec(
            num_scalar_prefetch=3,
            grid=(1,),
            in_specs=[],
            out_specs=[smem, smem],
        ),
        out_shape=[jax.ShapeDtypeStruct((n_slots,), I32), jax.ShapeDtypeStruct((n_slots,), I32)],
        compiler_params=pltpu.CompilerParams(dimension_semantics=("arbitrary",)),
        name="invert",
    )(dest_flat, pad_start, pad_count)


def _experts_kernel(be_ref, nu_ref, tok_ref, row_ref, h3_ref, wg_ref, wu_ref, wd_ref, y_ref,
                    xb0, xb1, yb0, yb1, wgb, wub, wdb, gsem, ssem):
    b = pl.program_id(0)
    n_used = nu_ref[0]
    blk = EXPERT_BLOCK

    def slab(buf, r):
        return buf.at[pl.ds(pl.multiple_of(r * ROW_TILES, ROW_TILES), ROW_TILES)]

    def gather_copy(tok, buf, r, par):
        return pltpu.make_async_copy(h3_ref.at[tok], slab(buf, r), gsem.at[par])

    def scatter_copy(dst, buf, r, par):
        return pltpu.make_async_copy(slab(buf, r), y_ref.at[dst], ssem.at[par])

    def rows(fn):
        def body(i, carry):
            for u in range(DMA_UNROLL):
                fn(i * DMA_UNROLL + u)
            return carry
        lax.fori_loop(0, blk // DMA_UNROLL, body, 0)

    def gather_start(block, buf, par):
        rows(lambda r: gather_copy(tok_ref[block * blk + r], buf, r, par).start())

    def scatter_start(block, buf, par):
        rows(lambda r: scatter_copy(row_ref[block * blk + r], buf, r, par).start())

    def gather_wait(buf, par):
        rows(lambda r: gather_copy(0, buf, r, par).wait())

    def scatter_wait(buf, par):
        rows(lambda r: scatter_copy(0, buf, r, par).wait())

    def step(par, x_cur, x_nxt, y_cur, y_oth):
        @pl.when(b == 0)
        def _():
            gather_start(0, x_cur, par)

        @pl.when(b + 1 < n_used)
        def _():
            gather_start(b + 1, x_nxt, 1 - par)

        @pl.when((b == 0) | (be_ref[b] != be_ref[jnp.maximum(b - 1, 0)]))
        def _():
            wgb[...] = wg_ref[0].astype(BF16)
            wub[...] = wu_ref[0].astype(BF16)
            wdb[...] = wd_ref[0].astype(BF16)

        gather_wait(x_cur, par)

        @pl.when(b >= 2)
        def _():
            scatter_wait(y_cur, par)

        x = jnp.concatenate(
            [x_cur[pl.ds(c, blk, stride=ROW_TILES), :] for c in range(ROW_TILES)], axis=1
        ).astype(BF16)
        gate = jnp.dot(x, wgb[...], preferred_element_type=F32)
        up = jnp.dot(x, wub[...], preferred_element_type=F32)
        hid = (jax.nn.silu(gate) * up).astype(BF16)
        y = jnp.dot(hid, wdb[...], preferred_element_type=F32)
        for c in range(ROW_TILES):
            y_cur[pl.ds(c, blk, stride=ROW_TILES), :] = y[:, c * LANES:(c + 1) * LANES]
        scatter_start(b, y_cur, par)

        @pl.when(b == n_used - 1)
        def _():
            scatter_wait(y_cur, par)

            @pl.when(b >= 1)
            def _():
                scatter_wait(y_oth, 1 - par)

    @pl.when((b < n_used) & (b % 2 == 0))
    def _():
        step(0, xb0, xb1, yb0, yb1)

    @pl.when((b < n_used) & (b % 2 == 1))
    def _():
        step(1, xb1, xb0, yb1, yb0)


def _experts(h3, block_expert, n_used, slot_tok, slot_row, wg, wu, wd):
    n_blocks = block_expert.shape[0]
    n_tok = h3.shape[0]

    def wmap(b, be, nu, st, sr):
        return (be[jnp.minimum(b, nu[0] - 1)], 0, 0)

    buf = pltpu.VMEM((EXPERT_BLOCK * ROW_TILES, LANES), F32)
    return pl.pallas_call(
        _experts_kernel,
        grid_spec=pltpu.PrefetchScalarGridSpec(
            num_scalar_prefetch=4,
            grid=(n_blocks,),
            in_specs=[
                pl.BlockSpec(memory_space=pl.ANY),
                pl.BlockSpec((1, D_MODEL, D_EXPERT), wmap),
                pl.BlockSpec((1, D_MODEL, D_EXPERT), wmap),
                pl.BlockSpec((1, D_EXPERT, D_MODEL), wmap),
            ],
            out_specs=pl.BlockSpec(memory_space=pl.ANY),
            scratch_shapes=[
                buf, buf, buf, buf,
                pltpu.VMEM((D_MODEL, D_EXPERT), BF16),
                pltpu.VMEM((D_MODEL, D_EXPERT), BF16),
                pltpu.VMEM((D_EXPERT, D_MODEL), BF16),
                pltpu.SemaphoreType.DMA((2,)),
                pltpu.SemaphoreType.DMA((2,)),
            ],
        ),
        out_shape=jax.ShapeDtypeStruct((TOP_K * n_tok + 2 * EXPERT_BLOCK, ROW_TILES, LANES), F32),
        compiler_params=pltpu.CompilerParams(
            dimension_semantics=("arbitrary",), vmem_limit_bytes=VMEM_LIMIT),
        name="experts",
    )(block_expert, n_used, slot_tok, slot_row, h3, wg, wu, wd)


def _final_kernel(x1_ref, route_ref, nw_ref, y0_ref, y1_ref, op_ref, os_ref, *, blocks_p):
    i = pl.program_id(0)

    tm = x1_ref.shape[0]

    def rows(ref):
        return jnp.concatenate(
            [ref[pl.ds(c, tm, stride=ROW_TILES), :] for c in range(ROW_TILES)], axis=1)

    w1 = route_ref[:, 4:5]
    w2 = route_ref[:, 5:6]
    x2 = x1_ref[...] + (rows(y0_ref) * w1 + rows(y1_ref) * w2)
    out = _rms(x2, nw_ref[...])

    @pl.when(i < blocks_p)
    def _():
        op_ref[...] = out

    @pl.when(i >= blocks_p)
    def _():
        os_ref[...] = out


def _final(x1, route, norm_w, y, tp, tm):
    t = x1.shape[0]
    blocks_p = tp // tm
    blocks_t = t // tm
    return pl.pallas_call(
        functools.partial(_final_kernel, blocks_p=blocks_p),
        grid=(blocks_t,),
        in_specs=[
            pl.BlockSpec((tm, D_MODEL), lambda i: (i, 0)),
            pl.BlockSpec((tm, LANES), lambda i: (i, 0)),
            pl.BlockSpec((1, D_MODEL), lambda i: (0, 0)),
            pl.BlockSpec((tm * ROW_TILES, LANES), lambda i: (i, 0)),
            pl.BlockSpec((tm * ROW_TILES, LANES), lambda i: (i + blocks_t, 0)),
        ],
        out_specs=[
            pl.BlockSpec((tm, D_MODEL), lambda i: (jnp.minimum(i, blocks_p - 1), 0)),
            pl.BlockSpec((tm, D_MODEL), lambda i: (jnp.maximum(i - blocks_p, 0), 0)),
        ],
        out_shape=[
            jax.ShapeDtypeStruct((tp, D_MODEL), F32),
            jax.ShapeDtypeStruct((t - tp, D_MODEL), F32),
        ],
        compiler_params=pltpu.CompilerParams(dimension_semantics=("arbitrary",)),
        name="final",
    )(x1, route, norm_w, y, y)


def _rope_tables(seq_len):
    half = HEAD_DIM // 2
    inv = ROPE_BASE ** (-jnp.arange(half, dtype=F32) / half)
    ang = jnp.arange(seq_len, dtype=F32)[:, None] * inv[None, :]
    cos = jnp.cos(ang)
    sin = jnp.sin(ang)
    return jnp.concatenate([cos, cos], axis=1), jnp.concatenate([-sin, sin], axis=1)


def _tile(limit, *sizes):
    return min(limit, math.gcd(*sizes))


def kernel(x_prompt, x_sample, norm_mix, w_in, gmlp_norm_v, gmlp_w_spatial, gmlp_b_spatial, ret_decay_fwd, ret_decay_bwd, ret_norm, w_out, norm_ffn, w_router_group, b_router_group, w_router_expert, b_router_expert, w_expert_gate, w_expert_up, w_expert_down, norm_final):
    assert norm_mix.shape[0] == 1, "single-layer block"
    bp, sp, d = x_prompt.shape
    bs, ss, _ = x_sample.shape
    assert d == D_MODEL and sp % CHUNK == 0 and ss % CHUNK == 0
    tp, ts = bp * sp, bs * ss
    t = tp + ts
    xp = x_prompt.reshape(tp, d)
    xs = x_sample.reshape(ts, d)

    tm = _tile(1024, sp, ss)
    cos_t, sin_t = _rope_tables(max(sp, ss))
    w_in_b = w_in[0].astype(BF16)
    gv_w = gmlp_norm_v.reshape(1, SECTION)
    proj = _proj(xp, norm_mix, w_in_b, cos_t, sin_t, gv_w, None, t, 0, tm, sp // tm)
    proj = _proj(xs, norm_mix, w_in_b, cos_t, sin_t, gv_w, proj, t, tp // tm, tm, ss // tm)

    lgf = jax.nn.log_sigmoid(ret_decay_fwd[0].astype(F32))
    lgb = jax.nn.log_sigmoid(ret_decay_bwd[0].astype(F32))
    sf, sb = _states(proj, lgf, lgb, tp // CHUNK, sp // CHUNK, ss // CHUNK)
    mixed = _mix(proj, sf, sb, lgf, lgb, gmlp_w_spatial[0].astype(BF16),
                 gmlp_b_spatial[0][:, :, None], ret_norm.reshape(1, SECTION))

    pad = LANES - N_EXPERTS - N_GROUPS
    wr = jnp.concatenate([w_router_expert[0], w_router_group[0], jnp.zeros((d, pad), F32)], axis=1)
    br = jnp.concatenate([b_router_expert[0], b_router_group[0], jnp.zeros((pad,), F32)])[None, :]
    tm = _tile(256, tp, ts)
    x1, h3, route, route_t, cnt = _post(xp, xs, mixed, w_out[0].astype(BF16), norm_ffn,
                                        wr.astype(BF16), br, tm)

    n_blocks = -(-(TOP_K * t) // EXPERT_BLOCK) + N_EXPERTS
    dest, block_expert, n_used, pad_start, pad_count = _plan(
        cnt[0, :N_EXPERTS].astype(I32), route_t, n_blocks)
    slot_tok, slot_row = _invert(dest.reshape(TOP_K * t), pad_start, pad_count, t,
                                 n_blocks * EXPERT_BLOCK)
    y = _experts(h3.reshape(t, ROW_TILES, LANES), block_expert, n_used, slot_tok, slot_row,
                 w_expert_gate[0], w_expert_up[0], w_expert_down[0])

    out_p, out_s = _final(x1, route, norm_final[None, :], y.reshape(-1, LANES), tp, tm)
    return out_p.reshape(bp, sp, d), out_s.reshape(bs, ss, d)
```

```python
import functools
import math

import jax
import jax.numpy as jnp
from jax import lax
from jax.experimental import pallas as pl
from jax.experimental.pallas import tpu as pltpu

F32 = jnp.float32
BF16 = jnp.bfloat16
I32 = jnp.int32

D_MODEL = 2048
CHUNK = 128
HEADS = 8
HEAD_DIM = 128
SECTION = HEADS * HEAD_DIM
N_SECTIONS = 6
PROJ_WIDTH = N_SECTIONS * SECTION
ROPE_BASE = 10000.0
N_GROUPS = 4
EXPERTS_PER_GROUP = 8
N_EXPERTS = N_GROUPS * EXPERTS_PER_GROUP
TOP_K = 2
D_EXPERT = D_MODEL // 4
EPS = 1e-6
LANES = 128
SUBLANES = 8
MXU_COLS = 256
ROW_TILES = D_MODEL // LANES
EXPERT_BLOCK = 256
DMA_UNROLL = 8
VMEM_LIMIT = 56 * 1024 * 1024


def _rms(x, w):
    ms = jnp.mean(x * x, axis=-1, keepdims=True)
    return x * lax.rsqrt(ms + EPS) * w


def _proj_kernel(x_ref, nw_ref, w_ref, cos_ref, sin_ref, gvw_ref, prev_ref, o_ref, h_ref):
    del prev_ref
    j = pl.program_id(1)

    @pl.when(j == 0)
    def _():
        h_ref[...] = _rms(x_ref[...], nw_ref[...]).astype(BF16)

    def rotary(seg):
        return seg * cos_ref[...] + pltpu.roll(seg, HEAD_DIM // 2, axis=1) * sin_ref[...]

    def section(epilogue):
        for c in range(SECTION // MXU_COLS):
            acc = jnp.dot(h_ref[...], w_ref[:, c * MXU_COLS:(c + 1) * MXU_COLS],
                          preferred_element_type=F32)
            for g in range(MXU_COLS // HEAD_DIM):
                sl = slice(c * MXU_COLS + g * HEAD_DIM, c * MXU_COLS + (g + 1) * HEAD_DIM)
                o_ref[:, sl] = epilogue(acc[:, g * HEAD_DIM:(g + 1) * HEAD_DIM], sl).astype(BF16)

    @pl.when(j == 0)
    def _():
        section(lambda seg, sl: jax.nn.gelu(seg))

    @pl.when(j == 1)
    def _():
        section(lambda seg, sl: _rms(jax.nn.gelu(seg), gvw_ref[:, sl]))

    @pl.when(j == 2)
    def _():
        section(lambda seg, sl: rotary(seg))

    @pl.when(j == 3)
    def _():
        section(lambda seg, sl: rotary(seg) * (HEAD_DIM ** -0.5))

    @pl.when(j == 4)
    def _():
        section(lambda seg, sl: seg)

    @pl.when(j == 5)
    def _():
        section(lambda seg, sl: jax.nn.silu(seg))


def _proj(x, norm_w, w_in, cos_t, sin_t, gv_w, prev, t_total, row_block0, tm, per_seq):
    if prev is None:
        prev = jnp.zeros((SUBLANES, LANES), BF16)
        aliases = {}
    else:
        aliases = {6: 0}
    return pl.pallas_call(
        _proj_kernel,
        grid=(x.shape[0] // tm, N_SECTIONS),
        in_specs=[
            pl.BlockSpec((tm, D_MODEL), lambda i, j: (i, 0)),
            pl.BlockSpec((1, D_MODEL), lambda i, j: (0, 0)),
            pl.BlockSpec((D_MODEL, SECTION), lambda i, j: (0, j)),
            pl.BlockSpec((tm, HEAD_DIM), lambda i, j: (i % per_seq, 0)),
            pl.BlockSpec((tm, HEAD_DIM), lambda i, j: (i % per_seq, 0)),
            pl.BlockSpec((1, SECTION), lambda i, j: (0, 0)),
            pl.BlockSpec(memory_space=pl.ANY),
        ],
        out_specs=pl.BlockSpec((tm, SECTION), lambda i, j: (i + row_block0, j)),
        out_shape=jax.ShapeDtypeStruct((t_total, PROJ_WIDTH), BF16),
        scratch_shapes=[pltpu.VMEM((tm, D_MODEL), BF16)],
        input_output_aliases=aliases,
        compiler_params=pltpu.CompilerParams(
            dimension_semantics=("arbitrary", "arbitrary"), vmem_limit_bytes=VMEM_LIMIT),
        name="proj",
    )(x, norm_w, w_in, cos_t, sin_t, gv_w, prev)


def _seq_edge(c, chunks_p, n_p, n_s, last):
    pos = jnp.where(c < chunks_p, c % n_p, (c - chunks_p) % n_s)
    edge = jnp.where(c < chunks_p, n_p - 1, n_s - 1) if last else 0
    return pos == edge


def _states_kernel(lgf_ref, lgb_ref, kf_ref, vf_ref, kb_ref, vb_ref, sf_ref, sb_ref, stf, stb,
                   *, n_chunks, chunks_p, n_p, n_s):
    s = pl.program_id(0)

    @pl.when(_seq_edge(s, chunks_p, n_p, n_s, last=False))
    def _():
        stf[...] = jnp.zeros_like(stf)

    @pl.when(_seq_edge(n_chunks - 1 - s, chunks_p, n_p, n_s, last=True))
    def _():
        stb[...] = jnp.zeros_like(stb)

    row = lax.broadcasted_iota(jnp.int32, (CHUNK, HEAD_DIM), 0).astype(F32)
    tn = (((0,), (0,)), ((), ()))
    for h in range(HEADS):
        sl = slice(h * HEAD_DIM, (h + 1) * HEAD_DIM)
        lgf = lgf_ref[h]
        lgb = lgb_ref[h]
        kd = (kf_ref[:, sl].astype(F32) * jnp.exp((CHUNK - 1.0 - row) * lgf)).astype(BF16)
        kv = lax.dot_general(kd, vf_ref[:, sl], tn, preferred_element_type=F32)
        sf_ref[0, h] = stf[h].astype(BF16)
        stf[h] = stf[h] * jnp.exp(CHUNK * lgf) + kv
        kd = (kb_ref[:, sl].astype(F32) * jnp.exp(row * lgb)).astype(BF16)
        kv = lax.dot_general(kd, vb_ref[:, sl], tn, preferred_element_type=F32)
        sb_ref[0, h] = stb[h].astype(BF16)
        stb[h] = stb[h] * jnp.exp(CHUNK * lgb) + kv


def _states(proj, lgf, lgb, chunks_p, n_p, n_s):
    n_chunks = proj.shape[0] // CHUNK
    last = n_chunks - 1
    smem = pl.BlockSpec(memory_space=pltpu.SMEM)
    st_shape = jax.ShapeDtypeStruct((n_chunks, HEADS, HEAD_DIM, HEAD_DIM), BF16)
    return pl.pallas_call(
        functools.partial(_states_kernel, n_chunks=n_chunks, chunks_p=chunks_p, n_p=n_p, n_s=n_s),
        grid=(n_chunks,),
        in_specs=[
            smem, smem,
            pl.BlockSpec((CHUNK, SECTION), lambda s: (s, 3)),
            pl.BlockSpec((CHUNK, SECTION), lambda s: (s, 4)),
            pl.BlockSpec((CHUNK, SECTION), lambda s: (last - s, 3)),
            pl.BlockSpec((CHUNK, SECTION), lambda s: (last - s, 4)),
        ],
        out_specs=[
            pl.BlockSpec((1, HEADS, HEAD_DIM, HEAD_DIM), lambda s: (s, 0, 0, 0)),
            pl.BlockSpec((1, HEADS, HEAD_DIM, HEAD_DIM), lambda s: (last - s, 0, 0, 0)),
        ],
        out_shape=[st_shape, st_shape],
        scratch_shapes=[pltpu.VMEM((HEADS, HEAD_DIM, HEAD_DIM), F32),
                        pltpu.VMEM((HEADS, HEAD_DIM, HEAD_DIM), F32)],
        compiler_params=pltpu.CompilerParams(dimension_semantics=("arbitrary",)),
        name="states",
    )(lgf, lgb, proj, proj, proj, proj)


def _mix_kernel(lgf_ref, lgb_ref, u_ref, gv_ref, q_ref, k_ref, v_ref, g_ref, sf_ref, sb_ref,
                ws_ref, bs_ref, rn_ref, o_ref, dec_ref, qdf_ref, qdb_ref):
    @pl.when(pl.program_id(0) == 0)
    def _():
        row = lax.broadcasted_iota(jnp.int32, (CHUNK, CHUNK), 0).astype(F32)
        col = lax.broadcasted_iota(jnp.int32, (CHUNK, CHUNK), 1).astype(F32)
        diff = row - col
        for h in range(HEADS):
            lgf = lgf_ref[h]
            lgb = lgb_ref[h]
            dec_ref[h] = jnp.where(diff >= 0, jnp.exp(jnp.maximum(diff, 0.0) * lgf),
                                   jnp.exp(jnp.maximum(-diff, 0.0) * lgb))
            qdf_ref[h] = jnp.exp((row + 1.0) * lgf)
            qdb_ref[h] = jnp.exp((CHUNK - row) * lgb)

    nt = (((1,), (1,)), ((), ()))
    for h in range(HEADS):
        sl = slice(h * HEAD_DIM, (h + 1) * HEAD_DIM)
        mixed = jnp.dot(ws_ref[h], gv_ref[:, sl], preferred_element_type=F32) + bs_ref[h]
        o_ref[:, sl] = (u_ref[:, sl].astype(F32) * mixed).astype(BF16)
        q = q_ref[:, sl]
        v = v_ref[:, sl]
        scores = lax.dot_general(q, k_ref[:, sl], nt, preferred_element_type=F32)
        ret = jnp.dot((scores * dec_ref[h]).astype(BF16), v, preferred_element_type=F32)
        ret += jnp.dot(q, sf_ref[0, h], preferred_element_type=F32) * qdf_ref[h]
        ret += jnp.dot(q, sb_ref[0, h], preferred_element_type=F32) * qdb_ref[h]
        out = _rms(ret, rn_ref[:, sl]) * g_ref[:, sl].astype(F32)
        o_ref[:, SECTION + h * HEAD_DIM:SECTION + (h + 1) * HEAD_DIM] = out.astype(BF16)


def _mix(proj, sf, sb, lgf, lgb, ws, bs, rn):
    t = proj.shape[0]
    smem = pl.BlockSpec(memory_space=pltpu.SMEM)

    def sec(j):
        return pl.BlockSpec((CHUNK, SECTION), lambda c, j=j: (c, j))

    st_spec = pl.BlockSpec((1, HEADS, HEAD_DIM, HEAD_DIM), lambda c: (c, 0, 0, 0))
    tab = pltpu.VMEM((HEADS, CHUNK, CHUNK), F32)
    return pl.pallas_call(
        _mix_kernel,
        grid=(t // CHUNK,),
        in_specs=[
            smem, smem, sec(0), sec(1), sec(2), sec(3), sec(4), sec(5), st_spec, st_spec,
            pl.BlockSpec((HEADS, CHUNK, CHUNK), lambda c: (0, 0, 0)),
            pl.BlockSpec((HEADS, CHUNK, 1), lambda c: (0, 0, 0)),
            pl.BlockSpec((1, SECTION), lambda c: (0, 0)),
        ],
        out_specs=pl.BlockSpec((CHUNK, 2 * SECTION), lambda c: (c, 0)),
        out_shape=jax.ShapeDtypeStruct((t, 2 * SECTION), BF16),
        scratch_shapes=[tab, tab, tab],
        compiler_params=pltpu.CompilerParams(dimension_semantics=("arbitrary",)),
        name="mix",
    )(lgf, lgb, proj, proj, proj, proj, proj, proj, sf, sb, ws, bs, rn)


def _post_kernel(xp_ref, xs_ref, m_ref, wo_ref, nw_ref, wr_ref, br_ref,
                 x1_ref, h3_ref, route_ref, rt_ref, cnt_ref, *, tm, blocks_p):
    i = pl.program_id(0)

    @pl.when(i == 0)
    def _():
        cnt_ref[...] = jnp.zeros_like(cnt_ref)

    x = jnp.where(i < blocks_p, xp_ref[...], xs_ref[...])
    x1 = x + jnp.dot(m_ref[...], wo_ref[...], preferred_element_type=F32)
    x1_ref[...] = x1
    h = _rms(x1, nw_ref[...])
    for c in range(ROW_TILES):
        h3_ref[pl.ds(c, tm, stride=ROW_TILES), :] = h[:, c * LANES:(c + 1) * LANES]

    logits = jnp.dot(h.astype(BF16), wr_ref[...], preferred_element_type=F32) + br_ref[...]
    lane = lax.broadcasted_iota(jnp.int32, (tm, LANES), 1).astype(F32)
    neg = jnp.float32(-jnp.inf)

    def first_max(vals):
        m = jnp.max(vals, axis=-1, keepdims=True)
        idx = jnp.min(jnp.where(vals == m, lane, float(LANES)), axis=-1, keepdims=True)
        return m, idx

    lg = jnp.where((lane >= N_EXPERTS) & (lane < N_EXPERTS + N_GROUPS), logits, neg)
    gmax, gidx = first_max(lg)
    p_sel = 1.0 / jnp.sum(jnp.exp(lg - gmax), axis=-1, keepdims=True)
    lo = (gidx - N_EXPERTS) * EXPERTS_PER_GROUP
    le = jnp.where((lane >= lo) & (lane < lo + EXPERTS_PER_GROUP), logits, neg)
    top1, i1 = first_max(le)
    top2, i2 = first_max(jnp.where(lane == i1, neg, le))
    e2 = jnp.exp(top2 - top1)
    w1 = p_sel / (1.0 + e2)
    w2 = p_sel * e2 / (1.0 + e2)

    hit1 = lane == i1
    hit2 = lane == i2
    onehot = jnp.where(hit1 | hit2, 1.0, 0.0).astype(BF16)
    r = lax.broadcasted_iota(jnp.int32, (tm, tm), 0)
    c = lax.broadcasted_iota(jnp.int32, (tm, tm), 1)
    lower = jnp.where(c < r, 1.0, 0.0).astype(BF16)
    before = jnp.dot(lower, onehot, preferred_element_type=F32) + cnt_ref[...]
    rank1 = jnp.sum(jnp.where(hit1, before, 0.0), axis=-1, keepdims=True)
    rank2 = jnp.sum(jnp.where(hit2, before, 0.0), axis=-1, keepdims=True)
    cnt_ref[...] += jnp.sum(onehot.astype(F32), axis=0, keepdims=True)

    route = jnp.where(lane == 0, i1, 0.0)
    route = jnp.where(lane == 1, i2, route)
    route = jnp.where(lane == 2, rank1, route)
    route = jnp.where(lane == 3, rank2, route)
    route = jnp.where(lane == 4, w1, route)
    route = jnp.where(lane == 5, w2, route)
    route_ref[...] = route
    rt_ref[...] = jnp.transpose(route)[0:SUBLANES, :]


def _post(xp, xs, mixed, w_out, norm_w, wr, br, tm):
    t = mixed.shape[0]
    blocks_p = xp.shape[0] // tm
    return pl.pallas_call(
        functools.partial(_post_kernel, tm=tm, blocks_p=blocks_p),
        grid=(t // tm,),
        in_specs=[
            pl.BlockSpec((tm, D_MODEL), lambda i: (jnp.minimum(i, blocks_p - 1), 0)),
            pl.BlockSpec((tm, D_MODEL), lambda i: (jnp.maximum(i - blocks_p, 0), 0)),
            pl.BlockSpec((tm, D_MODEL), lambda i: (i, 0)),
            pl.BlockSpec((D_MODEL, D_MODEL), lambda i: (0, 0)),
            pl.BlockSpec((1, D_MODEL), lambda i: (0, 0)),
            pl.BlockSpec((D_MODEL, LANES), lambda i: (0, 0)),
            pl.BlockSpec((1, LANES), lambda i: (0, 0)),
        ],
        out_specs=[
            pl.BlockSpec((tm, D_MODEL), lambda i: (i, 0)),
            pl.BlockSpec((tm * ROW_TILES, LANES), lambda i: (i, 0)),
            pl.BlockSpec((tm, LANES), lambda i: (i, 0)),
            pl.BlockSpec((SUBLANES, tm), lambda i: (0, i)),
            pl.BlockSpec((1, LANES), lambda i: (0, 0)),
        ],
        out_shape=[
            jax.ShapeDtypeStruct((t, D_MODEL), F32),
            jax.ShapeDtypeStruct((t * ROW_TILES, LANES), F32),
            jax.ShapeDtypeStruct((t, LANES), F32),
            jax.ShapeDtypeStruct((SUBLANES, t), F32),
            jax.ShapeDtypeStruct((1, LANES), F32),
        ],
        compiler_params=pltpu.CompilerParams(
            dimension_semantics=("arbitrary",), vmem_limit_bytes=VMEM_LIMIT),
        name="post",
    )(xp, xs, mixed, w_out, norm_w, wr, br)


def _plan_kernel(cnt_ref, rt_ref, dest_ref, be_ref, nu_ref, zst_ref, zcn_ref, ps_ref, *, n_blocks):
    def per_expert(e, first):
        n = (cnt_ref[e] + (EXPERT_BLOCK - 1)) // EXPERT_BLOCK
        ps_ref[e] = first * EXPERT_BLOCK
        zst_ref[e] = first * EXPERT_BLOCK + cnt_ref[e]
        zcn_ref[e] = n * EXPERT_BLOCK - cnt_ref[e]

        def fill(j, carry):
            be_ref[first + j] = e
            return carry
        lax.fori_loop(0, n, fill, 0)
        return first + n
    n_used = lax.fori_loop(0, N_EXPERTS, per_expert, 0)
    nu_ref[0] = n_used

    def fill_tail(j, carry):
        be_ref[j] = N_EXPERTS - 1
        return carry
    lax.fori_loop(n_used, n_blocks, fill_tail, 0)

    for k in range(TOP_K):
        e = rt_ref[k:k + 1, :]
        base = jnp.zeros_like(e)
        for x in range(N_EXPERTS):
            base = jnp.where(e == float(x), ps_ref[x].astype(F32), base)
        dest_ref[k:k + 1, :] = (base + rt_ref[TOP_K + k:TOP_K + k + 1, :]).astype(I32)


def _plan(counts, route_t, n_blocks):
    t = route_t.shape[1]
    smem = pl.BlockSpec(memory_space=pltpu.SMEM)
    return pl.pallas_call(
        functools.partial(_plan_kernel, n_blocks=n_blocks),
        grid=(1,),
        in_specs=[smem, pl.BlockSpec((SUBLANES, t), lambda i: (0, 0))],
        out_specs=[pl.BlockSpec((TOP_K, t), lambda i: (0, 0)), smem, smem, smem, smem],
        out_shape=[
            jax.ShapeDtypeStruct((TOP_K, t), I32),
            jax.ShapeDtypeStruct((n_blocks,), I32),
            jax.ShapeDtypeStruct((1,), I32),
            jax.ShapeDtypeStruct((N_EXPERTS,), I32),
            jax.ShapeDtypeStruct((N_EXPERTS,), I32),
        ],
        scratch_shapes=[pltpu.SMEM((N_EXPERTS,), I32)],
        compiler_params=pltpu.CompilerParams(dimension_semantics=("arbitrary",)),
        name="plan",
    )(counts, route_t)


def _invert_kernel(dest_ref, zst_ref, zcn_ref, tok_ref, row_ref, *, n_tok):
    n_assign = TOP_K * n_tok

    def pad_expert(e, carry):
        start = zst_ref[e]

        def one(j, c):
            p = start + j
            tok_ref[p] = 0
            row_ref[p] = n_assign + (p & (2 * EXPERT_BLOCK - 1))
            return c
        lax.fori_loop(0, zcn_ref[e], one, 0)
        return carry
    lax.fori_loop(0, N_EXPERTS, pad_expert, 0)

    for k in range(TOP_K):
        def fill(i, carry, k=k):
            for u in range(DMA_UNROLL):
                tok = i * DMA_UNROLL + u
                p = dest_ref[k * n_tok + tok]
                tok_ref[p] = tok
                row_ref[p] = k * n_tok + tok
            return carry
        lax.fori_loop(0, n_tok // DMA_UNROLL, fill, 0)


def _invert(dest_flat, pad_start, pad_count, n_tok, n_slots):
    assert EXPERT_BLOCK & (EXPERT_BLOCK - 1) == 0
    smem = pl.BlockSpec(memory_space=pltpu.SMEM)
    return pl.pallas_call(
        functools.partial(_invert_kernel, n_tok=n_tok),
        grid_spec=pltpu.PrefetchScalarGridSpec(
            num_scalar_prefetch=3,
            grid=(1,),
            in_specs=[],
            out_specs=[smem, smem],
        ),
        out_shape=[jax.ShapeDtypeStruct((n_slots,), I32), jax.ShapeDtypeStruct((n_slots,), I32)],
        compiler_params=pltpu.CompilerParams(dimension_semantics=("arbitrary",)),
        name="invert",
    )(dest_flat, pad_start, pad_count)


def _experts_kernel(be_ref, nu_ref, tok_ref, row_ref, h3_ref, wg_ref, wu_ref, wd_ref, y_ref,
                    xb0, xb1, yb0, yb1, wgb, wub, wdb, gsem, ssem):
    b = pl.program_id(0)
    n_used = nu_ref[0]
    blk = EXPERT_BLOCK

    def slab(buf, r):
        return buf.at[pl.ds(pl.multiple_of(r * ROW_TILES, ROW_TILES), ROW_TILES)]

    def gather_copy(tok, buf, r, par):
        return pltpu.make_async_copy(h3_ref.at[tok], slab(buf, r), gsem.at[par])

    def scatter_copy(dst, buf, r, par):
        return pltpu.make_async_copy(slab(buf, r), y_ref.at[dst], ssem.at[par])

    def rows(fn):
        def body(i, carry):
            for u in range(DMA_UNROLL):
                fn(i * DMA_UNROLL + u)
            return carry
        lax.fori_loop(0, blk // DMA_UNROLL, body, 0)

    def gather_start(block, buf, par):
        rows(lambda r: gather_copy(tok_ref[block * blk + r], buf, r, par).start())

    def scatter_start(block, buf, par):
        rows(lambda r: scatter_copy(row_ref[block * blk + r], buf, r, par).start())

    def gather_wait(buf, par):
        rows(lambda r: gather_copy(0, buf, r, par).wait())

    def scatter_wait(buf, par):
        rows(lambda r: scatter_copy(0, buf, r, par).wait())

    def step(par, x_cur, x_nxt, y_cur, y_oth):
        @pl.when(b == 0)
        def _():
            gather_start(0, x_cur, par)

        @pl.when(b + 1 < n_used)
        def _():
            gather_start(b + 1, x_nxt, 1 - par)

        @pl.when((b == 0) | (be_ref[b] != be_ref[jnp.maximum(b - 1, 0)]))
        def _():
            wgb[...] = wg_ref[0].astype(BF16)
            wub[...] = wu_ref[0].astype(BF16)
            wdb[...] = wd_ref[0].astype(BF16)

        gather_wait(x_cur, par)

        @pl.when(b >= 2)
        def _():
            scatter_wait(y_cur, par)

        x = jnp.concatenate(
            [x_cur[pl.ds(c, blk, stride=ROW_TILES), :] for c in range(ROW_TILES)], axis=1
        ).astype(BF16)
        gate = jnp.dot(x, wgb[...], preferred_element_type=F32)
        up = jnp.dot(x, wub[...], preferred_element_type=F32)
        hid = (jax.nn.silu(gate) * up).astype(BF16)
        y = jnp.dot(hid, wdb[...], preferred_element_type=F32)
        for c in range(ROW_TILES):
            y_cur[pl.ds(c, blk, stride=ROW_TILES), :] = y[:, c * LANES:(c + 1) * LANES]
        scatter_start(b, y_cur, par)

        @pl.when(b == n_used - 1)
        def _():
            scatter_wait(y_cur, par)

            @pl.when(b >= 1)
            def _():
                scatter_wait(y_oth, 1 - par)

    @pl.when((b < n_used) & (b % 2 == 0))
    def _():
        step(0, xb0, xb1, yb0, yb1)

    @pl.when((b < n_used) & (b % 2 == 1))
    def _():
        step(1, xb1, xb0, yb1, yb0)


def _experts(h3, block_expert, n_used, slot_tok, slot_row, wg, wu, wd):
    n_blocks = block_expert.shape[0]
    n_tok = h3.shape[0]

    def wmap(b, be, nu, st, sr):
        return (be[jnp.minimum(b, nu[0] - 1)], 0, 0)

    buf = pltpu.VMEM((EXPERT_BLOCK * ROW_TILES, LANES), F32)
    return pl.pallas_call(
        _experts_kernel,
        grid_spec=pltpu.PrefetchScalarGridSpec(
            num_scalar_prefetch=4,
            grid=(n_blocks,),
            in_specs=[
                pl.BlockSpec(memory_space=pl.ANY),
                pl.BlockSpec((1, D_MODEL, D_EXPERT), wmap),
                pl.BlockSpec((1, D_MODEL, D_EXPERT), wmap),
                pl.BlockSpec((1, D_EXPERT, D_MODEL), wmap),
            ],
            out_specs=pl.BlockSpec(memory_space=pl.ANY),
            scratch_shapes=[
                buf, buf, buf, buf,
                pltpu.VMEM((D_MODEL, D_EXPERT), BF16),
                pltpu.VMEM((D_MODEL, D_EXPERT), BF16),
                pltpu.VMEM((D_EXPERT, D_MODEL), BF16),
                pltpu.SemaphoreType.DMA((2,)),
                pltpu.SemaphoreType.DMA((2,)),
            ],
        ),
        out_shape=jax.ShapeDtypeStruct((TOP_K * n_tok + 2 * EXPERT_BLOCK, ROW_TILES, LANES), F32),
        compiler_params=pltpu.CompilerParams(
            dimension_semantics=("arbitrary",), vmem_limit_bytes=VMEM_LIMIT),
        name="experts",
    )(block_expert, n_used, slot_tok, slot_row, h3, wg, wu, wd)


def _final_kernel(x1_ref, route_ref, nw_ref, y0_ref, y1_ref, op_ref, os_ref, *, blocks_p):
    i = pl.program_id(0)

    tm = x1_ref.shape[0]

    def rows(ref):
        return jnp.concatenate(
            [ref[pl.ds(c, tm, stride=ROW_TILES), :] for c in range(ROW_TILES)], axis=1)

    w1 = route_ref[:, 4:5]
    w2 = route_ref[:, 5:6]
    x2 = x1_ref[...] + (rows(y0_ref) * w1 + rows(y1_ref) * w2)
    out = _rms(x2, nw_ref[...])

    @pl.when(i < blocks_p)
    def _():
        op_ref[...] = out

    @pl.when(i >= blocks_p)
    def _():
        os_ref[...] = out


def _final(x1, route, norm_w, y, tp, tm):
    t = x1.shape[0]
    blocks_p = tp // tm
    blocks_t = t // tm
    return pl.pallas_call(
        functools.partial(_final_kernel, blocks_p=blocks_p),
        grid=(blocks_t,),
        in_specs=[
            pl.BlockSpec((tm, D_MODEL), lambda i: (i, 0)),
            pl.BlockSpec((tm, LANES), lambda i: (i, 0)),
            pl.BlockSpec((1, D_MODEL), lambda i: (0, 0)),
            pl.BlockSpec((tm * ROW_TILES, LANES), lambda i: (i, 0)),
            pl.BlockSpec((tm * ROW_TILES, LANES), lambda i: (i + blocks_t, 0)),
        ],
        out_specs=[
            pl.BlockSpec((tm, D_MODEL), lambda i: (jnp.minimum(i, blocks_p - 1), 0)),
            pl.BlockSpec((tm, D_MODEL), lambda i: (jnp.maximum(i - blocks_p, 0), 0)),
        ],
        out_shape=[
            jax.ShapeDtypeStruct((tp, D_MODEL), F32),
            jax.ShapeDtypeStruct((t - tp, D_MODEL), F32),
        ],
        compiler_params=pltpu.CompilerParams(dimension_semantics=("arbitrary",)),
        name="final",
    )(x1, route, norm_w, y, y)


def _rope_tables(seq_len):
    half = HEAD_DIM // 2
    inv = ROPE_BASE ** (-jnp.arange(half, dtype=F32) / half)
    ang = jnp.arange(seq_len, dtype=F32)[:, None] * inv[None, :]
    cos = jnp.cos(ang)
    sin = jnp.sin(ang)
    return jnp.concatenate([cos, cos], axis=1), jnp.concatenate([-sin, sin], axis=1)


def _tile(limit, *sizes):
    return min(limit, math.gcd(*sizes))


def kernel(x_prompt, x_sample, norm_mix, w_in, gmlp_norm_v, gmlp_w_spatial, gmlp_b_spatial, ret_decay_fwd, ret_decay_bwd, ret_norm, w_out, norm_ffn, w_router_group, b_router_group, w_router_expert, b_router_expert, w_expert_gate, w_expert_up, w_expert_down, norm_final):
    assert norm_mix.shape[0] == 1, "single-layer block"
    bp, sp, d = x_prompt.shape
    bs, ss, _ = x_sample.shape
    assert d == D_MODEL and sp % CHUNK == 0 and ss % CHUNK == 0
    tp, ts = bp * sp, bs * ss
    t = tp + ts
    xp = x_prompt.reshape(tp, d)
    xs = x_sample.reshape(ts, d)

    tm = _tile(1024, sp, ss)
    cos_t, sin_t = _rope_tables(max(sp, ss))
    w_in_b = w_in[0].astype(BF16)
    gv_w = gmlp_norm_v.reshape(1, SECTION)
    proj = _proj(xp, norm_mix, w_in_b, cos_t, sin_t, gv_w, None, t, 0, tm, sp // tm)
    proj = _proj(xs, norm_mix, w_in_b, cos_t, sin_t, gv_w, proj, t, tp // tm, tm, ss // tm)

    lgf = jax.nn.log_sigmoid(ret_decay_fwd[0].astype(F32))
    lgb = jax.nn.log_sigmoid(ret_decay_bwd[0].astype(F32))
    sf, sb = _states(proj, lgf, lgb, tp // CHUNK, sp // CHUNK, ss // CHUNK)
    mixed = _mix(proj, sf, sb, lgf, lgb, gmlp_w_spatial[0].astype(BF16),
                 gmlp_b_spatial[0][:, :, None], ret_norm.reshape(1, SECTION))

    pad = LANES - N_EXPERTS - N_GROUPS
    wr = jnp.concatenate([w_router_expert[0], w_router_group[0], jnp.zeros((d, pad), F32)], axis=1)
    br = jnp.concatenate([b_router_expert[0], b_router_group[0], jnp.zeros((pad,), F32)])[None, :]
    tm = _tile(256, tp, ts)
    x1, h3, route, route_t, cnt = _post(xp, xs, mixed, w_out[0].astype(BF16), norm_ffn,
                                        wr.astype(BF16), br, tm)

    n_blocks = -(-(TOP_K * t) // EXPERT_BLOCK) + N_EXPERTS
    dest, block_expert, n_used, pad_start, pad_count = _plan(
        cnt[0, :N_EXPERTS].astype(I32), route_t, n_blocks)
    slot_tok, slot_row = _invert(dest.reshape(TOP_K * t), pad_start, pad_count, t,
                                 n_blocks * EXPERT_BLOCK)
    y = _experts(h3.reshape(t, ROW_TILES, LANES), block_expert, n_used, slot_tok, slot_row,
                 w_expert_gate[0], w_expert_up[0], w_expert_down[0])

    out_p, out_s = _final(x1, route, norm_final[None, :], y.reshape(-1, LANES), tp, tm)
    return out_p.reshape(bp, sp, d), out_s.reshape(bs, ss, d)
```

```python
import functools
import math

import jax
import jax.numpy as jnp
from jax import lax
from jax.experimental import pallas as pl
from jax.experimental.pallas import tpu as pltpu

F32 = jnp.float32
BF16 = jnp.bfloat16
I32 = jnp.int32

D_MODEL = 2048
CHUNK = 128
HEADS = 8
HEAD_DIM = 128
SECTION = HEADS * HEAD_DIM
N_SECTIONS = 6
PROJ_WIDTH = N_SECTIONS * SECTION
ROPE_BASE = 10000.0
N_GROUPS = 4
EXPERTS_PER_GROUP = 8
N_EXPERTS = N_GROUPS * EXPERTS_PER_GROUP
TOP_K = 2
D_EXPERT = D_MODEL // 4
EPS = 1e-6
LANES = 128
SUBLANES = 8
MXU_COLS = 256
ROW_TILES = D_MODEL // LANES
EXPERT_BLOCK = 256
DMA_UNROLL = 8
VMEM_LIMIT = 56 * 1024 * 1024


def _rms(x, w):
    ms = jnp.mean(x * x, axis=-1, keepdims=True)
    return x * lax.rsqrt(ms + EPS) * w


def _proj_kernel(x_ref, nw_ref, w_ref, cos_ref, sin_ref, gvw_ref, prev_ref, o_ref, h_ref):
    del prev_ref
    j = pl.program_id(1)

    @pl.when(j == 0)
    def _():
        h_ref[...] = _rms(x_ref[...], nw_ref[...]).astype(BF16)

    def rotary(seg):
        return seg * cos_ref[...] + pltpu.roll(seg, HEAD_DIM // 2, axis=1) * sin_ref[...]

    def section(epilogue):
        for c in range(SECTION // MXU_COLS):
            acc = jnp.dot(h_ref[...], w_ref[:, c * MXU_COLS:(c + 1) * MXU_COLS],
                          preferred_element_type=F32)
            for g in range(MXU_COLS // HEAD_DIM):
                sl = slice(c * MXU_COLS + g * HEAD_DIM, c * MXU_COLS + (g + 1) * HEAD_DIM)
                o_ref[:, sl] = epilogue(acc[:, g * HEAD_DIM:(g + 1) * HEAD_DIM], sl).astype(BF16)

    @pl.when(j == 0)
    def _():
        section(lambda seg, sl: jax.nn.gelu(seg))

    @pl.when(j == 1)
    def _():
        section(lambda seg, sl: _rms(jax.nn.gelu(seg), gvw_ref[:, sl]))

    @pl.when(j == 2)
    def _():
        section(lambda seg, sl: rotary(seg))

    @pl.when(j == 3)
    def _():
        section(lambda seg, sl: rotary(seg) * (HEAD_DIM ** -0.5))

    @pl.when(j == 4)
    def _():
        section(lambda seg, sl: seg)

    @pl.when(j == 5)
    def _():
        section(lambda seg, sl: jax.nn.silu(seg))


def _proj(x, norm_w, w_in, cos_t, sin_t, gv_w, prev, t_total, row_block0, tm, per_seq):
    if prev is None:
        prev = jnp.zeros((SUBLANES, LANES), BF16)
        aliases = {}
    else:
        aliases = {6: 0}
    return pl.pallas_call(
        _proj_kernel,
        grid=(x.shape[0] // tm, N_SECTIONS),
        in_specs=[
            pl.BlockSpec((tm, D_MODEL), lambda i, j: (i, 0)),
            pl.BlockSpec((1, D_MODEL), lambda i, j: (0, 0)),
            pl.BlockSpec((D_MODEL, SECTION), lambda i, j: (0, j)),
            pl.BlockSpec((tm, HEAD_DIM), lambda i, j: (i % per_seq, 0)),
            pl.BlockSpec((tm, HEAD_DIM), lambda i, j: (i % per_seq, 0)),
            pl.BlockSpec((1, SECTION), lambda i, j: (0, 0)),
            pl.BlockSpec(memory_space=pl.ANY),
        ],
        out_specs=pl.BlockSpec((tm, SECTION), lambda i, j: (i + row_block0, j)),
        out_shape=jax.ShapeDtypeStruct((t_total, PROJ_WIDTH), BF16),
        scratch_shapes=[pltpu.VMEM((tm, D_MODEL), BF16)],
        input_output_aliases=aliases,
        compiler_params=pltpu.CompilerParams(
            dimension_semantics=("arbitrary", "arbitrary"), vmem_limit_bytes=VMEM_LIMIT),
        name="proj",
    )(x, norm_w, w_in, cos_t, sin_t, gv_w, prev)


def _seq_edge(c, chunks_p, n_p, n_s, last):
    pos = jnp.where(c < chunks_p, c % n_p, (c - chunks_p) % n_s)
    edge = jnp.where(c < chunks_p, n_p - 1, n_s - 1) if last else 0
    return pos == edge


def _states_kernel(lgf_ref, lgb_ref, kf_ref, vf_ref, kb_ref, vb_ref, sf_ref, sb_ref, stf, stb,
                   *, n_chunks, chunks_p, n_p, n_s):
    s = pl.program_id(0)

    @pl.when(_seq_edge(s, chunks_p, n_p, n_s, last=False))
    def _():
        stf[...] = jnp.zeros_like(stf)

    @pl.when(_seq_edge(n_chunks - 1 - s, chunks_p, n_p, n_s, last=True))
    def _():
        stb[...] = jnp.zeros_like(stb)

    row = lax.broadcasted_iota(jnp.int32, (CHUNK, HEAD_DIM), 0).astype(F32)
    tn = (((0,), (0,)), ((), ()))
    for h in range(HEADS):
        sl = slice(h * HEAD_DIM, (h + 1) * HEAD_DIM)
        lgf = lgf_ref[h]
        lgb = lgb_ref[h]
        kd = (kf_ref[:, sl].astype(F32) * jnp.exp((CHUNK - 1.0 - row) * lgf)).astype(BF16)
        kv = lax.dot_general(kd, vf_ref[:, sl], tn, preferred_element_type=F32)
        sf_ref[0, h] = stf[h].astype(BF16)
        stf[h] = stf[h] * jnp.exp(CHUNK * lgf) + kv
        kd = (kb_ref[:, sl].astype(F32) * jnp.exp(row * lgb)).astype(BF16)
        kv = lax.dot_general(kd, vb_ref[:, sl], tn, preferred_element_type=F32)
        sb_ref[0, h] = stb[h].astype(BF16)
        stb[h] = stb[h] * jnp.exp(CHUNK * lgb) + kv


def _states(proj, lgf, lgb, chunks_p, n_p, n_s):
    n_chunks = proj.shape[0] // CHUNK
    last = n_chunks - 1
    smem = pl.BlockSpec(memory_space=pltpu.SMEM)
    st_shape = jax.ShapeDtypeStruct((n_chunks, HEADS, HEAD_DIM, HEAD_DIM), BF16)
    return pl.pallas_call(
        functools.partial(_states_kernel, n_chunks=n_chunks, chunks_p=chunks_p, n_p=n_p, n_s=n_s),
        grid=(n_chunks,),
        in_specs=[
            smem, smem,
            pl.BlockSpec((CHUNK, SECTION), lambda s: (s, 3)),
            pl.BlockSpec((CHUNK, SECTION), lambda s: (s, 4)),
            pl.BlockSpec((CHUNK, SECTION), lambda s: (last - s, 3)),
            pl.BlockSpec((CHUNK, SECTION), lambda s: (last - s, 4)),
        ],
        out_specs=[
            pl.BlockSpec((1, HEADS, HEAD_DIM, HEAD_DIM), lambda s: (s, 0, 0, 0)),
            pl.BlockSpec((1, HEADS, HEAD_DIM, HEAD_DIM), lambda s: (last - s, 0, 0, 0)),
        ],
        out_shape=[st_shape, st_shape],
        scratch_shapes=[pltpu.VMEM((HEADS, HEAD_DIM, HEAD_DIM), F32),
                        pltpu.VMEM((HEADS, HEAD_DIM, HEAD_DIM), F32)],
        compiler_params=pltpu.CompilerParams(dimension_semantics=("arbitrary",)),
        name="states",
    )(lgf, lgb, proj, proj, proj, proj)


def _mix_kernel(lgf_ref, lgb_ref, u_ref, gv_ref, q_ref, k_ref, v_ref, g_ref, sf_ref, sb_ref,
                ws_ref, bs_ref, rn_ref, o_ref, dec_ref, qdf_ref, qdb_ref):
    @pl.when(pl.program_id(0) == 0)
    def _():
        row = lax.broadcasted_iota(jnp.int32, (CHUNK, CHUNK), 0).astype(F32)
        col = lax.broadcasted_iota(jnp.int32, (CHUNK, CHUNK), 1).astype(F32)
        diff = row - col
        for h in range(HEADS):
            lgf = lgf_ref[h]
            lgb = lgb_ref[h]
            dec_ref[h] = jnp.where(diff >= 0, jnp.exp(jnp.maximum(diff, 0.0) * lgf),
                                   jnp.exp(jnp.maximum(-diff, 0.0) * lgb))
            qdf_ref[h] = jnp.exp((row + 1.0) * lgf)
            qdb_ref[h] = jnp.exp((CHUNK - row) * lgb)

    nt = (((1,), (1,)), ((), ()))
    for h in range(HEADS):
        sl = slice(h * HEAD_DIM, (h + 1) * HEAD_DIM)
        mixed = jnp.dot(ws_ref[h], gv_ref[:, sl], preferred_element_type=F32) + bs_ref[h]
        o_ref[:, sl] = (u_ref[:, sl].astype(F32) * mixed).astype(BF16)
        q = q_ref[:, sl]
        v = v_ref[:, sl]
        scores = lax.dot_general(q, k_ref[:, sl], nt, preferred_element_type=F32)
        ret = jnp.dot((scores * dec_ref[h]).astype(BF16), v, preferred_element_type=F32)
        ret += jnp.dot(q, sf_ref[0, h], preferred_element_type=F32) * qdf_ref[h]
        ret += jnp.dot(q, sb_ref[0, h], preferred_element_type=F32) * qdb_ref[h]
        out = _rms(ret, rn_ref[:, sl]) * g_ref[:, sl].astype(F32)
        o_ref[:, SECTION + h * HEAD_DIM:SECTION + (h + 1) * HEAD_DIM] = out.astype(BF16)


def _mix(proj, sf, sb, lgf, lgb, ws, bs, rn):
    t = proj.shape[0]
    smem = pl.BlockSpec(memory_space=pltpu.SMEM)

    def sec(j):
        return pl.BlockSpec((CHUNK, SECTION), lambda c, j=j: (c, j))

    st_spec = pl.BlockSpec((1, HEADS, HEAD_DIM, HEAD_DIM), lambda c: (c, 0, 0, 0))
    tab = pltpu.VMEM((HEADS, CHUNK, CHUNK), F32)
    return pl.pallas_call(
        _mix_kernel,
        grid=(t // CHUNK,),
        in_specs=[
            smem, smem, sec(0), sec(1), sec(2), sec(3), sec(4), sec(5), st_spec, st_spec,
            pl.BlockSpec((HEADS, CHUNK, CHUNK), lambda c: (0, 0, 0)),
            pl.BlockSpec((HEADS, CHUNK, 1), lambda c: (0, 0, 0)),
            pl.BlockSpec((1, SECTION), lambda c: (0, 0)),
        ],
        out_specs=pl.BlockSpec((CHUNK, 2 * SECTION), lambda c: (c, 0)),
        out_shape=jax.ShapeDtypeStruct((t, 2 * SECTION), BF16),
        scratch_shapes=[tab, tab, tab],
        compiler_params=pltpu.CompilerParams(dimension_semantics=("arbitrary",)),
        name="mix",
    )(lgf, lgb, proj, proj, proj, proj, proj, proj, sf, sb, ws, bs, rn)


def _post_kernel(xp_ref, xs_ref, m_ref, wo_ref, nw_ref, wr_ref, br_ref,
                 x1_ref, h3_ref, route_ref, rt_ref, cnt_ref, *, tm, blocks_p):
    i = pl.program_id(0)

    @pl.when(i == 0)
    def _():
        cnt_ref[...] = jnp.zeros_like(cnt_ref)

    x = jnp.where(i < blocks_p, xp_ref[...], xs_ref[...])
    x1 = x + jnp.dot(m_ref[...], wo_ref[...], preferred_element_type=F32)
    x1_ref[...] = x1
    h = _rms(x1, nw_ref[...])
    for c in range(ROW_TILES):
        h3_ref[pl.ds(c, tm, stride=ROW_TILES), :] = h[:, c * LANES:(c + 1) * LANES]

    logits = jnp.dot(h.astype(BF16), wr_ref[...], preferred_element_type=F32) + br_ref[...]
    lane = lax.broadcasted_iota(jnp.int32, (tm, LANES), 1).astype(F32)
    neg = jnp.float32(-jnp.inf)

    def first_max(vals):
        m = jnp.max(vals, axis=-1, keepdims=True)
        idx = jnp.min(jnp.where(vals == m, lane, float(LANES)), axis=-1, keepdims=True)
        return m, idx

    lg = jnp.where((lane >= N_EXPERTS) & (lane < N_EXPERTS + N_GROUPS), logits, neg)
    gmax, gidx = first_max(lg)
    p_sel = 1.0 / jnp.sum(jnp.exp(lg - gmax), axis=-1, keepdims=True)
    lo = (gidx - N_EXPERTS) * EXPERTS_PER_GROUP
    le = jnp.where((lane >= lo) & (lane < lo + EXPERTS_PER_GROUP), logits, neg)
    top1, i1 = first_max(le)
    top2, i2 = first_max(jnp.where(lane == i1, neg, le))
    e2 = jnp.exp(top2 - top1)
    w1 = p_sel / (1.0 + e2)
    w2 = p_sel * e2 / (1.0 + e2)

    hit1 = lane == i1
    hit2 = lane == i2
    onehot = jnp.where(hit1 | hit2, 1.0, 0.0).astype(BF16)
    r = lax.broadcasted_iota(jnp.int32, (tm, tm), 0)
    c = lax.broadcasted_iota(jnp.int32, (tm, tm), 1)
    lower = jnp.where(c < r, 1.0, 0.0).astype(BF16)
    before = jnp.dot(lower, onehot, preferred_element_type=F32) + cnt_ref[...]
    rank1 = jnp.sum(jnp.where(hit1, before, 0.0), axis=-1, keepdims=True)
    rank2 = jnp.sum(jnp.where(hit2, before, 0.0), axis=-1, keepdims=True)
    cnt_ref[...] += jnp.sum(onehot.astype(F32), axis=0, keepdims=True)

    route = jnp.where(lane == 0, i1, 0.0)
    route = jnp.where(lane == 1, i2, route)
    route = jnp.where(lane == 2, rank1, route)
    route = jnp.where(lane == 3, rank2, route)
    route = jnp.where(lane == 4, w1, route)
    route = jnp.where(lane == 5, w2, route)
    route_ref[...] = route
    rt_ref[...] = jnp.transpose(route)[0:SUBLANES, :]


def _post(xp, xs, mixed, w_out, norm_w, wr, br, tm):
    t = mixed.shape[0]
    blocks_p = xp.shape[0] // tm
    return pl.pallas_call(
        functools.partial(_post_kernel, tm=tm, blocks_p=blocks_p),
        grid=(t // tm,),
        in_specs=[
            pl.BlockSpec((tm, D_MODEL), lambda i: (jnp.minimum(i, blocks_p - 1), 0)),
            pl.BlockSpec((tm, D_MODEL), lambda i: (jnp.maximum(i - blocks_p, 0), 0)),
            pl.BlockSpec((tm, D_MODEL), lambda i: (i, 0)),
            pl.BlockSpec((D_MODEL, D_MODEL), lambda i: (0, 0)),
            pl.BlockSpec((1, D_MODEL), lambda i: (0, 0)),
            pl.BlockSpec((D_MODEL, LANES), lambda i: (0, 0)),
            pl.BlockSpec((1, LANES), lambda i: (0, 0)),
        ],
        out_specs=[
            pl.BlockSpec((tm, D_MODEL), lambda i: (i, 0)),
            pl.BlockSpec((tm * ROW_TILES, LANES), lambda i: (i, 0)),
            pl.BlockSpec((tm, LANES), lambda i: (i, 0)),
            pl.BlockSpec((SUBLANES, tm), lambda i: (0, i)),
            pl.BlockSpec((1, LANES), lambda i: (0, 0)),
        ],
        out_shape=[
            jax.ShapeDtypeStruct((t, D_MODEL), F32),
            jax.ShapeDtypeStruct((t * ROW_TILES, LANES), F32),
            jax.ShapeDtypeStruct((t, LANES), F32),
            jax.ShapeDtypeStruct((SUBLANES, t), F32),
            jax.ShapeDtypeStruct((1, LANES), F32),
        ],
        compiler_params=pltpu.CompilerParams(
            dimension_semantics=("arbitrary",), vmem_limit_bytes=VMEM_LIMIT),
        name="post",
    )(xp, xs, mixed, w_out, norm_w, wr, br)


def _plan_kernel(cnt_ref, rt_ref, dest_ref, be_ref, nu_ref, zst_ref, zcn_ref, ps_ref, *, n_blocks):
    def per_expert(e, first):
        n = (cnt_ref[e] + (EXPERT_BLOCK - 1)) // EXPERT_BLOCK
        ps_ref[e] = first * EXPERT_BLOCK
        zst_ref[e] = first * EXPERT_BLOCK + cnt_ref[e]
        zcn_ref[e] = n * EXPERT_BLOCK - cnt_ref[e]

        def fill(j, carry):
            be_ref[first + j] = e
            return carry
        lax.fori_loop(0, n, fill, 0)
        return first + n
    n_used = lax.fori_loop(0, N_EXPERTS, per_expert, 0)
    nu_ref[0] = n_used

    def fill_tail(j, carry):
        be_ref[j] = N_EXPERTS - 1
        return carry
    lax.fori_loop(n_used, n_blocks, fill_tail, 0)

    for k in range(TOP_K):
        e = rt_ref[k:k + 1, :]
        base = jnp.zeros_like(e)
        for x in range(N_EXPERTS):
            base = jnp.where(e == float(x), ps_ref[x].astype(F32), base)
        dest_ref[k:k + 1, :] = (base + rt_ref[TOP_K + k:TOP_K + k + 1, :]).astype(I32)


def _plan(counts, route_t, n_blocks):
    t = route_t.shape[1]
    smem = pl.BlockSpec(memory_space=pltpu.SMEM)
    return pl.pallas_call(
        functools.partial(_plan_kernel, n_blocks=n_blocks),
        grid=(1,),
        in_specs=[smem, pl.BlockSpec((SUBLANES, t), lambda i: (0, 0))],
        out_specs=[pl.BlockSpec((TOP_K, t), lambda i: (0, 0)), smem, smem, smem, smem],
        out_shape=[
            jax.ShapeDtypeStruct((TOP_K, t), I32),
            jax.ShapeDtypeStruct((n_blocks,), I32),
            jax.ShapeDtypeStruct((1,), I32),
            jax.ShapeDtypeStruct((N_EXPERTS,), I32),
            jax.ShapeDtypeStruct((N_EXPERTS,), I32),
        ],
        scratch_shapes=[pltpu.SMEM((N_EXPERTS,), I32)],
        compiler_params=pltpu.CompilerParams(dimension_semantics=("arbitrary",)),
        name="plan",
    )(counts, route_t)


def _invert_kernel(dest_ref, zst_ref, zcn_ref, tok_ref, row_ref, *, n_tok):
    n_assign = TOP_K * n_tok

    def lead(j, carry):
        row_ref[j] = n_assign + 2 * EXPERT_BLOCK + j
        return carry
    lax.fori_loop(0, EXPERT_BLOCK, lead, 0)

    def pad_expert(e, carry):
        start = zst_ref[e]

        def one(j, c):
            p = start + j
            tok_ref[p] = 0
            row_ref[EXPERT_BLOCK + p] = n_assign + (p & (2 * EXPERT_BLOCK - 1))
            return c
        lax.fori_loop(0, zcn_ref[e], one, 0)
        return carry
    lax.fori_loop(0, N_EXPERTS, pad_expert, 0)

    for k in range(TOP_K):
        def fill(i, carry, k=k):
            for u in range(DMA_UNROLL):
                tok = i * DMA_UNROLL + u
                p = dest_ref[k * n_tok + tok]
                tok_ref[p] = tok
                row_ref[EXPERT_BLOCK + p] = k * n_tok + tok
            return carry
        lax.fori_loop(0, n_tok // DMA_UNROLL, fill, 0)


def _invert(dest_flat, pad_start, pad_count, n_tok, n_slots):
    assert EXPERT_BLOCK & (EXPERT_BLOCK - 1) == 0
    smem = pl.BlockSpec(memory_space=pltpu.SMEM)
    return pl.pallas_call(
        functools.partial(_invert_kernel, n_tok=n_tok),
        grid_spec=pltpu.PrefetchScalarGridSpec(
            num_scalar_prefetch=3,
            grid=(1,),
            in_specs=[],
            out_specs=[smem, smem],
        ),
        out_shape=[jax.ShapeDtypeStruct((n_slots,), I32),
                   jax.ShapeDtypeStruct((n_slots + EXPERT_BLOCK,), I32)],
        compiler_params=pltpu.CompilerParams(dimension_semantics=("arbitrary",)),
        name="invert",
    )(dest_flat, pad_start, pad_count)


def _experts_kernel(be_ref, nu_ref, tok_ref, row_ref, h3_ref, wg_ref, wu_ref, wd_ref, y_ref,
                    xb0, xb1, yb0, yb1, wgb, wub, wdb, gsem, ssem):
    b = pl.program_id(0)
    n_used = nu_ref[0]
    blk = EXPERT_BLOCK

    def slab(buf, r):
        start = r * ROW_TILES
        if not isinstance(r, int):
            start = pl.multiple_of(start, ROW_TILES)
        return buf.at[pl.ds(start, ROW_TILES)]

    def gather_copy(tok, buf, r, par):
        return pltpu.make_async_copy(h3_ref.at[tok], slab(buf, r), gsem.at[par])

    def scatter_copy(dst, buf, r, par):
        return pltpu.make_async_copy(slab(buf, r), y_ref.at[dst], ssem.at[par])

    def rows(fn):
        def body(i, carry):
            for u in range(DMA_UNROLL):
                fn(i * DMA_UNROLL + u)
            return carry
        lax.fori_loop(0, blk // DMA_UNROLL, body, 0)

    def gather_wait(buf, par):
        rows(lambda r: gather_copy(0, buf, r, par).wait())

    def scatter_wait(buf, par):
        rows(lambda r: scatter_copy(0, buf, r, par).wait())

    def step(par, x_cur, x_nxt, y_cur, y_oth):
        @pl.when(b == 0)
        def _():
            rows(lambda r: gather_copy(tok_ref[r], x_cur, r, par).start())
            y_oth[...] = jnp.zeros_like(y_oth)

        @pl.when((b == 0) | (be_ref[b] != be_ref[jnp.maximum(b - 1, 0)]))
        def _():
            wgb[...] = wg_ref[0].astype(BF16)
            wub[...] = wu_ref[0].astype(BF16)
            wdb[...] = wd_ref[0].astype(BF16)

        gather_wait(x_cur, par)

        @pl.when(b >= 1)
        def _():
            scatter_wait(y_cur, par)

        x = jnp.concatenate(
            [x_cur[pl.ds(c, blk, stride=ROW_TILES), :] for c in range(ROW_TILES)], axis=1
        ).astype(BF16)
        nxt = jnp.minimum(b + 1, n_used - 1) * blk
        for r in range(blk):
            gather_copy(tok_ref[nxt + r], x_nxt, r, 1 - par).start(priority=0)
            scatter_copy(row_ref[b * blk + r], y_oth, r, 1 - par).start(priority=1)
        gate = jnp.dot(x, wgb[...], preferred_element_type=F32)
        up = jnp.dot(x, wub[...], preferred_element_type=F32)
        hid = (jax.nn.silu(gate) * up).astype(BF16)
        y = jnp.dot(hid, wdb[...], preferred_element_type=F32)
        for c in range(ROW_TILES):
            y_cur[pl.ds(c, blk, stride=ROW_TILES), :] = y[:, c * LANES:(c + 1) * LANES]

    def drain(par, x_cur, y_cur, y_oth):
        gather_wait(x_cur, par)
        rows(lambda r: scatter_copy(row_ref[b * blk + r], y_oth, r, 1 - par).start())
        scatter_wait(y_cur, par)
        scatter_wait(y_oth, 1 - par)

    for par, bufs in ((0, (xb0, xb1, yb0, yb1)), (1, (xb1, xb0, yb1, yb0))):
        @pl.when((b < n_used) & (b % 2 == par))
        def _(par=par, bufs=bufs):
            step(par, *bufs)

        @pl.when((b == n_used) & (b % 2 == par))
        def _(par=par, bufs=bufs):
            drain(par, bufs[0], bufs[2], bufs[3])


def _experts(h3, block_expert, n_used, slot_tok, slot_row, wg, wu, wd):
    n_blocks = block_expert.shape[0]
    n_tok = h3.shape[0]

    def wmap(b, be, nu, st, sr):
        return (be[jnp.minimum(b, nu[0] - 1)], 0, 0)

    buf = pltpu.VMEM((EXPERT_BLOCK * ROW_TILES, LANES), F32)
    return pl.pallas_call(
        _experts_kernel,
        grid_spec=pltpu.PrefetchScalarGridSpec(
            num_scalar_prefetch=4,
            grid=(n_blocks,),
            in_specs=[
                pl.BlockSpec(memory_space=pl.ANY),
                pl.BlockSpec((1, D_MODEL, D_EXPERT), wmap),
                pl.BlockSpec((1, D_MODEL, D_EXPERT), wmap),
                pl.BlockSpec((1, D_EXPERT, D_MODEL), wmap),
            ],
            out_specs=pl.BlockSpec(memory_space=pl.ANY),
            scratch_shapes=[
                buf, buf, buf, buf,
                pltpu.VMEM((D_MODEL, D_EXPERT), BF16),
                pltpu.VMEM((D_MODEL, D_EXPERT), BF16),
                pltpu.VMEM((D_EXPERT, D_MODEL), BF16),
                pltpu.SemaphoreType.DMA((2,)),
                pltpu.SemaphoreType.DMA((2,)),
            ],
        ),
        out_shape=jax.ShapeDtypeStruct((TOP_K * n_tok + 3 * EXPERT_BLOCK, ROW_TILES, LANES), F32),
        compiler_params=pltpu.CompilerParams(
            dimension_semantics=("arbitrary",), vmem_limit_bytes=VMEM_LIMIT),
        name="experts",
    )(block_expert, n_used, slot_tok, slot_row, h3, wg, wu, wd)


def _final_kernel(x1_ref, route_ref, nw_ref, y0_ref, y1_ref, op_ref, os_ref, *, blocks_p):
    i = pl.program_id(0)

    tm = x1_ref.shape[0]

    def rows(ref):
        return jnp.concatenate(
            [ref[pl.ds(c, tm, stride=ROW_TILES), :] for c in range(ROW_TILES)], axis=1)

    w1 = route_ref[:, 4:5]
    w2 = route_ref[:, 5:6]
    x2 = x1_ref[...] + (rows(y0_ref) * w1 + rows(y1_ref) * w2)
    out = _rms(x2, nw_ref[...])

    @pl.when(i < blocks_p)
    def _():
        op_ref[...] = out

    @pl.when(i >= blocks_p)
    def _():
        os_ref[...] = out


def _final(x1, route, norm_w, y, tp, tm):
    t = x1.shape[0]
    blocks_p = tp // tm
    blocks_t = t // tm
    return pl.pallas_call(
        functools.partial(_final_kernel, blocks_p=blocks_p),
        grid=(blocks_t,),
        in_specs=[
            pl.BlockSpec((tm, D_MODEL), lambda i: (i, 0)),
            pl.BlockSpec((tm, LANES), lambda i: (i, 0)),
            pl.BlockSpec((1, D_MODEL), lambda i: (0, 0)),
            pl.BlockSpec((tm * ROW_TILES, LANES), lambda i: (i, 0)),
            pl.BlockSpec((tm * ROW_TILES, LANES), lambda i: (i + blocks_t, 0)),
        ],
        out_specs=[
            pl.BlockSpec((tm, D_MODEL), lambda i: (jnp.minimum(i, blocks_p - 1), 0)),
            pl.BlockSpec((tm, D_MODEL), lambda i: (jnp.maximum(i - blocks_p, 0), 0)),
        ],
        out_shape=[
            jax.ShapeDtypeStruct((tp, D_MODEL), F32),
            jax.ShapeDtypeStruct((t - tp, D_MODEL), F32),
        ],
        compiler_params=pltpu.CompilerParams(dimension_semantics=("arbitrary",)),
        name="final",
    )(x1, route, norm_w, y, y)


def _rope_tables(seq_len):
    half = HEAD_DIM // 2
    inv = ROPE_BASE ** (-jnp.arange(half, dtype=F32) / half)
    ang = jnp.arange(seq_len, dtype=F32)[:, None] * inv[None, :]
    cos = jnp.cos(ang)
    sin = jnp.sin(ang)
    return jnp.concatenate([cos, cos], axis=1), jnp.concatenate([-sin, sin], axis=1)


def _tile(limit, *sizes):
    return min(limit, math.gcd(*sizes))


def kernel(x_prompt, x_sample, norm_mix, w_in, gmlp_norm_v, gmlp_w_spatial, gmlp_b_spatial, ret_decay_fwd, ret_decay_bwd, ret_norm, w_out, norm_ffn, w_router_group, b_router_group, w_router_expert, b_router_expert, w_expert_gate, w_expert_up, w_expert_down, norm_final):
    assert norm_mix.shape[0] == 1, "single-layer block"
    bp, sp, d = x_prompt.shape
    bs, ss, _ = x_sample.shape
    assert d == D_MODEL and sp % CHUNK == 0 and ss % CHUNK == 0
    tp, ts = bp * sp, bs * ss
    t = tp + ts
    xp = x_prompt.reshape(tp, d)
    xs = x_sample.reshape(ts, d)

    tm = _tile(1024, sp, ss)
    cos_t, sin_t = _rope_tables(max(sp, ss))
    w_in_b = w_in[0].astype(BF16)
    gv_w = gmlp_norm_v.reshape(1, SECTION)
    proj = _proj(xp, norm_mix, w_in_b, cos_t, sin_t, gv_w, None, t, 0, tm, sp // tm)
    proj = _proj(xs, norm_mix, w_in_b, cos_t, sin_t, gv_w, proj, t, tp // tm, tm, ss // tm)

    lgf = jax.nn.log_sigmoid(ret_decay_fwd[0].astype(F32))
    lgb = jax.nn.log_sigmoid(ret_decay_bwd[0].astype(F32))
    sf, sb = _states(proj, lgf, lgb, tp // CHUNK, sp // CHUNK, ss // CHUNK)
    mixed = _mix(proj, sf, sb, lgf, lgb, gmlp_w_spatial[0].astype(BF16),
                 gmlp_b_spatial[0][:, :, None], ret_norm.reshape(1, SECTION))

    pad = LANES - N_EXPERTS - N_GROUPS
    wr = jnp.concatenate([w_router_expert[0], w_router_group[0], jnp.zeros((d, pad), F32)], axis=1)
    br = jnp.concatenate([b_router_expert[0], b_router_group[0], jnp.zeros((pad,), F32)])[None, :]
    tm = _tile(256, tp, ts)
    x1, h3, route, route_t, cnt = _post(xp, xs, mixed, w_out[0].astype(BF16), norm_ffn,
                                        wr.astype(BF16), br, tm)

    n_blocks = -(-(TOP_K * t) // EXPERT_BLOCK) + N_EXPERTS
    dest, block_expert, n_used, pad_start, pad_count = _plan(
        cnt[0, :N_EXPERTS].astype(I32), route_t, n_blocks)
    slot_tok, slot_row = _invert(dest.reshape(TOP_K * t), pad_start, pad_count, t,
                                 n_blocks * EXPERT_BLOCK)
    y = _experts(h3.reshape(t, ROW_TILES, LANES), block_expert, n_used, slot_tok, slot_row,
                 w_expert_gate[0], w_expert_up[0], w_expert_down[0])

    out_p, out_s = _final(x1, route, norm_final[None, :], y.reshape(-1, LANES), tp, tm)
    return out_p.reshape(bp, sp, d), out_s.reshape(bs, ss, d)
```

```python
import functools
import math

import jax
import jax.numpy as jnp
from jax import lax
from jax.experimental import pallas as pl
from jax.experimental.pallas import tpu as pltpu

F32 = jnp.float32
BF16 = jnp.bfloat16
I32 = jnp.int32
U32 = jnp.uint32

D_MODEL = 2048
CHUNK = 128
HEADS = 8
HEAD_DIM = 128
SECTION = HEADS * HEAD_DIM
N_SECTIONS = 6
PROJ_WIDTH = N_SECTIONS * SECTION
ROPE_BASE = 10000.0
N_GROUPS = 4
EXPERTS_PER_GROUP = 8
N_EXPERTS = N_GROUPS * EXPERTS_PER_GROUP
TOP_K = 2
D_EXPERT = D_MODEL // 4
EPS = 1e-6
LANES = 128
SUBLANES = 8
MXU_COLS = 256
ROW_TILES = D_MODEL // LANES // 2
EXPERT_BLOCK = 256
DMA_UNROLL = 8
VMEM_LIMIT = 56 * 1024 * 1024


def _rms(x, w):
    ms = jnp.mean(x * x, axis=-1, keepdims=True)
    return x * lax.rsqrt(ms + EPS) * w


def _pack_row_tiles(x):
    def bits(c):
        return lax.bitcast_convert_type(x[:, c * LANES:(c + 1) * LANES].astype(BF16).astype(F32), U32)
    return [(bits(2 * c) >> 16) | bits(2 * c + 1) for c in range(ROW_TILES)]


def _unpack_row_tiles(words):
    tiles = []
    for w in words:
        tiles.append(lax.bitcast_convert_type(w << 16, F32))
        tiles.append(lax.bitcast_convert_type(w & jnp.uint32(0xFFFF0000), F32))
    return tiles


def _proj_kernel(x_ref, nw_ref, w_ref, cos_ref, sin_ref, gvw_ref, prev_ref, o_ref, h_ref):
    del prev_ref
    j = pl.program_id(1)

    @pl.when(j == 0)
    def _():
        h_ref[...] = _rms(x_ref[...], nw_ref[...]).astype(BF16)

    def rotary(seg):
        return seg * cos_ref[...] + pltpu.roll(seg, HEAD_DIM // 2, axis=1) * sin_ref[...]

    def section(epilogue):
        for c in range(SECTION // MXU_COLS):
            acc = jnp.dot(h_ref[...], w_ref[:, c * MXU_COLS:(c + 1) * MXU_COLS],
                          preferred_element_type=F32)
            for g in range(MXU_COLS // HEAD_DIM):
                sl = slice(c * MXU_COLS + g * HEAD_DIM, c * MXU_COLS + (g + 1) * HEAD_DIM)
                o_ref[:, sl] = epilogue(acc[:, g * HEAD_DIM:(g + 1) * HEAD_DIM], sl).astype(BF16)

    @pl.when(j == 0)
    def _():
        section(lambda seg, sl: jax.nn.gelu(seg))

    @pl.when(j == 1)
    def _():
        section(lambda seg, sl: _rms(jax.nn.gelu(seg), gvw_ref[:, sl]))

    @pl.when(j == 2)
    def _():
        section(lambda seg, sl: rotary(seg))

    @pl.when(j == 3)
    def _():
        section(lambda seg, sl: rotary(seg) * (HEAD_DIM ** -0.5))

    @pl.when(j == 4)
    def _():
        section(lambda seg, sl: seg)

    @pl.when(j == 5)
    def _():
        section(lambda seg, sl: jax.nn.silu(seg))


def _proj(x, norm_w, w_in, cos_t, sin_t, gv_w, prev, t_total, row_block0, tm, per_seq):
    if prev is None:
        prev = jnp.zeros((SUBLANES, LANES), BF16)
        aliases = {}
    else:
        aliases = {6: 0}
    return pl.pallas_call(
        _proj_kernel,
        grid=(x.shape[0] // tm, N_SECTIONS),
        in_specs=[
            pl.BlockSpec((tm, D_MODEL), lambda i, j: (i, 0)),
            pl.BlockSpec((1, D_MODEL), lambda i, j: (0, 0)),
            pl.BlockSpec((D_MODEL, SECTION), lambda i, j: (0, j)),
            pl.BlockSpec((tm, HEAD_DIM), lambda i, j: (i % per_seq, 0)),
            pl.BlockSpec((tm, HEAD_DIM), lambda i, j: (i % per_seq, 0)),
            pl.BlockSpec((1, SECTION), lambda i, j: (0, 0)),
            pl.BlockSpec(memory_space=pl.ANY),
        ],
        out_specs=pl.BlockSpec((tm, SECTION), lambda i, j: (i + row_block0, j)),
        out_shape=jax.ShapeDtypeStruct((t_total, PROJ_WIDTH), BF16),
        scratch_shapes=[pltpu.VMEM((tm, D_MODEL), BF16)],
        input_output_aliases=aliases,
        compiler_params=pltpu.CompilerParams(
            dimension_semantics=("arbitrary", "arbitrary"), vmem_limit_bytes=VMEM_LIMIT),
        name="proj",
    )(x, norm_w, w_in, cos_t, sin_t, gv_w, prev)


def _seq_edge(c, chunks_p, n_p, n_s, last):
    pos = jnp.where(c < chunks_p, c % n_p, (c - chunks_p) % n_s)
    edge = jnp.where(c < chunks_p, n_p - 1, n_s - 1) if last else 0
    return pos == edge


def _states_kernel(lgf_ref, lgb_ref, kf_ref, vf_ref, kb_ref, vb_ref, sf_ref, sb_ref, stf, stb,
                   *, n_chunks, chunks_p, n_p, n_s):
    s = pl.program_id(0)

    @pl.when(_seq_edge(s, chunks_p, n_p, n_s, last=False))
    def _():
        stf[...] = jnp.zeros_like(stf)

    @pl.when(_seq_edge(n_chunks - 1 - s, chunks_p, n_p, n_s, last=True))
    def _():
        stb[...] = jnp.zeros_like(stb)

    row = lax.broadcasted_iota(jnp.int32, (CHUNK, HEAD_DIM), 0).astype(F32)
    tn = (((0,), (0,)), ((), ()))
    for h in range(HEADS):
        sl = slice(h * HEAD_DIM, (h + 1) * HEAD_DIM)
        lgf = lgf_ref[h]
        lgb = lgb_ref[h]
        kd = (kf_ref[:, sl].astype(F32) * jnp.exp((CHUNK - 1.0 - row) * lgf)).astype(BF16)
        kv = lax.dot_general(kd, vf_ref[:, sl], tn, preferred_element_type=F32)
        sf_ref[0, h] = stf[h].astype(BF16)
        stf[h] = stf[h] * jnp.exp(CHUNK * lgf) + kv
        kd = (kb_ref[:, sl].astype(F32) * jnp.exp(row * lgb)).astype(BF16)
        kv = lax.dot_general(kd, vb_ref[:, sl], tn, preferred_element_type=F32)
        sb_ref[0, h] = stb[h].astype(BF16)
        stb[h] = stb[h] * jnp.exp(CHUNK * lgb) + kv


def _states(proj, lgf, lgb, chunks_p, n_p, n_s):
    n_chunks = proj.shape[0] // CHUNK
    last = n_chunks - 1
    smem = pl.BlockSpec(memory_space=pltpu.SMEM)
    st_shape = jax.ShapeDtypeStruct((n_chunks, HEADS, HEAD_DIM, HEAD_DIM), BF16)
    return pl.pallas_call(
        functools.partial(_states_kernel, n_chunks=n_chunks, chunks_p=chunks_p, n_p=n_p, n_s=n_s),
        grid=(n_chunks,),
        in_specs=[
            smem, smem,
            pl.BlockSpec((CHUNK, SECTION), lambda s: (s, 3)),
            pl.BlockSpec((CHUNK, SECTION), lambda s: (s, 4)),
            pl.BlockSpec((CHUNK, SECTION), lambda s: (last - s, 3)),
            pl.BlockSpec((CHUNK, SECTION), lambda s: (last - s, 4)),
        ],
        out_specs=[
            pl.BlockSpec((1, HEADS, HEAD_DIM, HEAD_DIM), lambda s: (s, 0, 0, 0)),
            pl.BlockSpec((1, HEADS, HEAD_DIM, HEAD_DIM), lambda s: (last - s, 0, 0, 0)),
        ],
        out_shape=[st_shape, st_shape],
        scratch_shapes=[pltpu.VMEM((HEADS, HEAD_DIM, HEAD_DIM), F32),
                        pltpu.VMEM((HEADS, HEAD_DIM, HEAD_DIM), F32)],
        compiler_params=pltpu.CompilerParams(dimension_semantics=("arbitrary",)),
        name="states",
    )(lgf, lgb, proj, proj, proj, proj)


def _mix_kernel(lgf_ref, lgb_ref, u_ref, gv_ref, q_ref, k_ref, v_ref, g_ref, sf_ref, sb_ref,
                ws_ref, bs_ref, rn_ref, o_ref, dec_ref, qdf_ref, qdb_ref):
    @pl.when(pl.program_id(0) == 0)
    def _():
        row = lax.broadcasted_iota(jnp.int32, (CHUNK, CHUNK), 0).astype(F32)
        col = lax.broadcasted_iota(jnp.int32, (CHUNK, CHUNK), 1).astype(F32)
        diff = row - col
        for h in range(HEADS):
            lgf = lgf_ref[h]
            lgb = lgb_ref[h]
            dec_ref[h] = jnp.where(diff >= 0, jnp.exp(jnp.maximum(diff, 0.0) * lgf),
                                   jnp.exp(jnp.maximum(-diff, 0.0) * lgb))
            qdf_ref[h] = jnp.exp((row + 1.0) * lgf)
            qdb_ref[h] = jnp.exp((CHUNK - row) * lgb)

    nt = (((1,), (1,)), ((), ()))
    for h in range(HEADS):
        sl = slice(h * HEAD_DIM, (h + 1) * HEAD_DIM)
        mixed = jnp.dot(ws_ref[h], gv_ref[:, sl], preferred_element_type=F32) + bs_ref[h]
        o_ref[:, sl] = (u_ref[:, sl].astype(F32) * mixed).astype(BF16)
        q = q_ref[:, sl]
        v = v_ref[:, sl]
        scores = lax.dot_general(q, k_ref[:, sl], nt, preferred_element_type=F32)
        ret = jnp.dot((scores * dec_ref[h]).astype(BF16), v, preferred_element_type=F32)
        ret += jnp.dot(q, sf_ref[0, h], preferred_element_type=F32) * qdf_ref[h]
        ret += jnp.dot(q, sb_ref[0, h], preferred_element_type=F32) * qdb_ref[h]
        out = _rms(ret, rn_ref[:, sl]) * g_ref[:, sl].astype(F32)
        o_ref[:, SECTION + h * HEAD_DIM:SECTION + (h + 1) * HEAD_DIM] = out.astype(BF16)


def _mix(proj, sf, sb, lgf, lgb, ws, bs, rn):
    t = proj.shape[0]
    smem = pl.BlockSpec(memory_space=pltpu.SMEM)

    def sec(j):
        return pl.BlockSpec((CHUNK, SECTION), lambda c, j=j: (c, j))

    st_spec = pl.BlockSpec((1, HEADS, HEAD_DIM, HEAD_DIM), lambda c: (c, 0, 0, 0))
    tab = pltpu.VMEM((HEADS, CHUNK, CHUNK), F32)
    return pl.pallas_call(
        _mix_kernel,
        grid=(t // CHUNK,),
        in_specs=[
            smem, smem, sec(0), sec(1), sec(2), sec(3), sec(4), sec(5), st_spec, st_spec,
            pl.BlockSpec((HEADS, CHUNK, CHUNK), lambda c: (0, 0, 0)),
            pl.BlockSpec((HEADS, CHUNK, 1), lambda c: (0, 0, 0)),
            pl.BlockSpec((1, SECTION), lambda c: (0, 0)),
        ],
        out_specs=pl.BlockSpec((CHUNK, 2 * SECTION), lambda c: (c, 0)),
        out_shape=jax.ShapeDtypeStruct((t, 2 * SECTION), BF16),
        scratch_shapes=[tab, tab, tab],
        compiler_params=pltpu.CompilerParams(dimension_semantics=("arbitrary",)),
        name="mix",
    )(lgf, lgb, proj, proj, proj, proj, proj, proj, sf, sb, ws, bs, rn)


def _post_kernel(xp_ref, xs_ref, m_ref, wo_ref, nw_ref, wr_ref, br_ref,
                 x1_ref, h3_ref, route_ref, rt_ref, cnt_ref, *, tm, blocks_p):
    i = pl.program_id(0)

    @pl.when(i == 0)
    def _():
        cnt_ref[...] = jnp.zeros_like(cnt_ref)

    x = jnp.where(i < blocks_p, xp_ref[...], xs_ref[...])
    x1 = x + jnp.dot(m_ref[...], wo_ref[...], preferred_element_type=F32)
    x1_ref[...] = x1
    h = _rms(x1, nw_ref[...])
    for c, words in enumerate(_pack_row_tiles(h)):
        h3_ref[pl.ds(c, tm, stride=ROW_TILES), :] = words

    logits = jnp.dot(h.astype(BF16), wr_ref[...], preferred_element_type=F32) + br_ref[...]
    lane = lax.broadcasted_iota(jnp.int32, (tm, LANES), 1).astype(F32)
    neg = jnp.float32(-jnp.inf)

    def first_max(vals):
        m = jnp.max(vals, axis=-1, keepdims=True)
        idx = jnp.min(jnp.where(vals == m, lane, float(LANES)), axis=-1, keepdims=True)
        return m, idx

    lg = jnp.where((lane >= N_EXPERTS) & (lane < N_EXPERTS + N_GROUPS), logits, neg)
    gmax, gidx = first_max(lg)
    p_sel = 1.0 / jnp.sum(jnp.exp(lg - gmax), axis=-1, keepdims=True)
    lo = (gidx - N_EXPERTS) * EXPERTS_PER_GROUP
    le = jnp.where((lane >= lo) & (lane < lo + EXPERTS_PER_GROUP), logits, neg)
    top1, i1 = first_max(le)
    top2, i2 = first_max(jnp.where(lane == i1, neg, le))
    e2 = jnp.exp(top2 - top1)
    w1 = p_sel / (1.0 + e2)
    w2 = p_sel * e2 / (1.0 + e2)

    hit1 = lane == i1
    hit2 = lane == i2
    onehot = jnp.where(hit1 | hit2, 1.0, 0.0).astype(BF16)
    r = lax.broadcasted_iota(jnp.int32, (tm, tm), 0)
    c = lax.broadcasted_iota(jnp.int32, (tm, tm), 1)
    lower = jnp.where(c < r, 1.0, 0.0).astype(BF16)
    before = jnp.dot(lower, onehot, preferred_element_type=F32) + cnt_ref[...]
    rank1 = jnp.sum(jnp.where(hit1, before, 0.0), axis=-1, keepdims=True)
    rank2 = jnp.sum(jnp.where(hit2, before, 0.0), axis=-1, keepdims=True)
    cnt_ref[...] += jnp.sum(onehot.astype(F32), axis=0, keepdims=True)

    route = jnp.where(lane == 0, i1, 0.0)
    route = jnp.where(lane == 1, i2, route)
    route = jnp.where(lane == 2, rank1, route)
    route = jnp.where(lane == 3, rank2, route)
    route = jnp.where(lane == 4, w1, route)
    route = jnp.where(lane == 5, w2, route)
    route_ref[...] = route
    rt_ref[...] = jnp.transpose(route)[0:SUBLANES, :]


def _post(xp, xs, mixed, w_out, norm_w, wr, br, tm):
    t = mixed.shape[0]
    blocks_p = xp.shape[0] // tm
    return pl.pallas_call(
        functools.partial(_post_kernel, tm=tm, blocks_p=blocks_p),
        grid=(t // tm,),
        in_specs=[
            pl.BlockSpec((tm, D_MODEL), lambda i: (jnp.minimum(i, blocks_p - 1), 0)),
            pl.BlockSpec((tm, D_MODEL), lambda i: (jnp.maximum(i - blocks_p, 0), 0)),
            pl.BlockSpec((tm, D_MODEL), lambda i: (i, 0)),
            pl.BlockSpec((D_MODEL, D_MODEL), lambda i: (0, 0)),
            pl.BlockSpec((1, D_MODEL), lambda i: (0, 0)),
            pl.BlockSpec((D_MODEL, LANES), lambda i: (0, 0)),
            pl.BlockSpec((1, LANES), lambda i: (0, 0)),
        ],
        out_specs=[
            pl.BlockSpec((tm, D_MODEL), lambda i: (i, 0)),
            pl.BlockSpec((tm * ROW_TILES, LANES), lambda i: (i, 0)),
            pl.BlockSpec((tm, LANES), lambda i: (i, 0)),
            pl.BlockSpec((SUBLANES, tm), lambda i: (0, i)),
            pl.BlockSpec((1, LANES), lambda i: (0, 0)),
        ],
        out_shape=[
            jax.ShapeDtypeStruct((t, D_MODEL), F32),
            jax.ShapeDtypeStruct((t * ROW_TILES, LANES), U32),
            jax.ShapeDtypeStruct((t, LANES), F32),
            jax.ShapeDtypeStruct((SUBLANES, t), F32),
            jax.ShapeDtypeStruct((1, LANES), F32),
        ],
        compiler_params=pltpu.CompilerParams(
            dimension_semantics=("arbitrary",), vmem_limit_bytes=VMEM_LIMIT),
        name="post",
    )(xp, xs, mixed, w_out, norm_w, wr, br)


def _plan_kernel(cnt_ref, rt_ref, dest_ref, be_ref, nu_ref, zst_ref, zcn_ref, ps_ref, *, n_blocks):
    def per_expert(e, first):
        n = (cnt_ref[e] + (EXPERT_BLOCK - 1)) // EXPERT_BLOCK
        ps_ref[e] = first * EXPERT_BLOCK
        zst_ref[e] = first * EXPERT_BLOCK + cnt_ref[e]
        zcn_ref[e] = n * EXPERT_BLOCK - cnt_ref[e]

        def fill(j, carry):
            be_ref[first + j] = e
            return carry
        lax.fori_loop(0, n, fill, 0)
        return first + n
    n_used = lax.fori_loop(0, N_EXPERTS, per_expert, 0)
    nu_ref[0] = n_used

    def fill_tail(j, carry):
        be_ref[j] = N_EXPERTS - 1
        return carry
    lax.fori_loop(n_used, n_blocks, fill_tail, 0)

    for k in range(TOP_K):
        e = rt_ref[k:k + 1, :]
        base = jnp.zeros_like(e)
        for x in range(N_EXPERTS):
            base = jnp.where(e == float(x), ps_ref[x].astype(F32), base)
        dest_ref[k:k + 1, :] = (base + rt_ref[TOP_K + k:TOP_K + k + 1, :]).astype(I32)


def _plan(counts, route_t, n_blocks):
    t = route_t.shape[1]
    smem = pl.BlockSpec(memory_space=pltpu.SMEM)
    return pl.pallas_call(
        functools.partial(_plan_kernel, n_blocks=n_blocks),
        grid=(1,),
        in_specs=[smem, pl.BlockSpec((SUBLANES, t), lambda i: (0, 0))],
        out_specs=[pl.BlockSpec((TOP_K, t), lambda i: (0, 0)), smem, smem, smem, smem],
        out_shape=[
            jax.ShapeDtypeStruct((TOP_K, t), I32),
            jax.ShapeDtypeStruct((n_blocks,), I32),
            jax.ShapeDtypeStruct((1,), I32),
            jax.ShapeDtypeStruct((N_EXPERTS,), I32),
            jax.ShapeDtypeStruct((N_EXPERTS,), I32),
        ],
        scratch_shapes=[pltpu.SMEM((N_EXPERTS,), I32)],
        compiler_params=pltpu.CompilerParams(dimension_semantics=("arbitrary",)),
        name="plan",
    )(counts, route_t)


def _invert_kernel(dest_ref, zst_ref, zcn_ref, tok_ref, row_ref, *, n_tok):
    n_assign = TOP_K * n_tok

    def lead(j, carry):
        row_ref[j] = n_assign + 2 * EXPERT_BLOCK + j
        return carry
    lax.fori_loop(0, EXPERT_BLOCK, lead, 0)

    def pad_expert(e, carry):
        start = zst_ref[e]

        def one(j, c):
            p = start + j
            tok_ref[p] = 0
            row_ref[EXPERT_BLOCK + p] = n_assign + (p & (2 * EXPERT_BLOCK - 1))
            return c
        lax.fori_loop(0, zcn_ref[e], one, 0)
        return carry
    lax.fori_loop(0, N_EXPERTS, pad_expert, 0)

    for k in range(TOP_K):
        def fill(i, carry, k=k):
            for u in range(DMA_UNROLL):
                tok = i * DMA_UNROLL + u
                p = dest_ref[k * n_tok + tok]
                tok_ref[p] = tok
                row_ref[EXPERT_BLOCK + p] = k * n_tok + tok
            return carry
        lax.fori_loop(0, n_tok // DMA_UNROLL, fill, 0)


def _invert(dest_flat, pad_start, pad_count, n_tok, n_slots):
    assert EXPERT_BLOCK & (EXPERT_BLOCK - 1) == 0
    smem = pl.BlockSpec(memory_space=pltpu.SMEM)
    return pl.pallas_call(
        functools.partial(_invert_kernel, n_tok=n_tok),
        grid_spec=pltpu.PrefetchScalarGridSpec(
            num_scalar_prefetch=3,
            grid=(1,),
            in_specs=[],
            out_specs=[smem, smem],
        ),
        out_shape=[jax.ShapeDtypeStruct((n_slots,), I32),
                   jax.ShapeDtypeStruct((n_slots + EXPERT_BLOCK,), I32)],
        compiler_params=pltpu.CompilerParams(dimension_semantics=("arbitrary",)),
        name="invert",
    )(dest_flat, pad_start, pad_count)


def _experts_kernel(be_ref, nu_ref, tok_ref, row_ref, h3_ref, wg_ref, wu_ref, wd_ref, y_ref,
                    xb0, xb1, yb0, yb1, wgb, wub, wdb, gsem, ssem):
    b = pl.program_id(0)
    n_used = nu_ref[0]
    blk = EXPERT_BLOCK

    def slab(buf, r):
        start = r * ROW_TILES
        if not isinstance(r, int):
            start = pl.multiple_of(start, ROW_TILES)
        return buf.at[pl.ds(start, ROW_TILES)]

    def gather_copy(tok, buf, r, par):
        return pltpu.make_async_copy(h3_ref.at[tok], slab(buf, r), gsem.at[par])

    def scatter_copy(dst, buf, r, par):
        return pltpu.make_async_copy(slab(buf, r), y_ref.at[dst], ssem.at[par])

    def rows(fn):
        def body(i, carry):
            for u in range(DMA_UNROLL):
                fn(i * DMA_UNROLL + u)
            return carry
        lax.fori_loop(0, blk // DMA_UNROLL, body, 0)

    def gather_wait(buf, par):
        rows(lambda r: gather_copy(0, buf, r, par).wait())

    def scatter_wait(buf, par):
        rows(lambda r: scatter_copy(0, buf, r, par).wait())

    def step(par, x_cur, x_nxt, y_cur, y_oth):
        @pl.when(b == 0)
        def _():
            rows(lambda r: gather_copy(tok_ref[r], x_cur, r, par).start())
            y_oth[...] = jnp.zeros_like(y_oth)

        @pl.when((b == 0) | (be_ref[b] != be_ref[jnp.maximum(b - 1, 0)]))
        def _():
            wgb[...] = wg_ref[0].astype(BF16)
            wub[...] = wu_ref[0].astype(BF16)
            wdb[...] = wd_ref[0].astype(BF16)

        gather_wait(x_cur, par)

        @pl.when(b >= 1)
        def _():
            scatter_wait(y_cur, par)

        x = jnp.concatenate(_unpack_row_tiles(
            [x_cur[pl.ds(c, blk, stride=ROW_TILES), :] for c in range(ROW_TILES)]), axis=1
        ).astype(BF16)
        nxt = jnp.minimum(b + 1, n_used - 1) * blk
        for r in range(blk):
            gather_copy(tok_ref[nxt + r], x_nxt, r, 1 - par).start(priority=0)
            scatter_copy(row_ref[b * blk + r], y_oth, r, 1 - par).start(priority=1)
        gate = jnp.dot(x, wgb[...], preferred_element_type=F32)
        up = jnp.dot(x, wub[...], preferred_element_type=F32)
        hid = (jax.nn.silu(gate) * up).astype(BF16)
        y = jnp.dot(hid, wdb[...], preferred_element_type=F32)
        for c, words in enumerate(_pack_row_tiles(y)):
            y_cur[pl.ds(c, blk, stride=ROW_TILES), :] = words

    def drain(par, x_cur, y_cur, y_oth):
        gather_wait(x_cur, par)
        rows(lambda r: scatter_copy(row_ref[b * blk + r], y_oth, r, 1 - par).start())
        scatter_wait(y_cur, par)
        scatter_wait(y_oth, 1 - par)

    for par, bufs in ((0, (xb0, xb1, yb0, yb1)), (1, (xb1, xb0, yb1, yb0))):
        @pl.when((b < n_used) & (b % 2 == par))
        def _(par=par, bufs=bufs):
            step(par, *bufs)

        @pl.when((b == n_used) & (b % 2 == par))
        def _(par=par, bufs=bufs):
            drain(par, bufs[0], bufs[2], bufs[3])


def _experts(h3, block_expert, n_used, slot_tok, slot_row, wg, wu, wd):
    n_blocks = block_expert.shape[0]
    n_tok = h3.shape[0]

    def wmap(b, be, nu, st, sr):
        return (be[jnp.minimum(b, nu[0] - 1)], 0, 0)

    buf = pltpu.VMEM((EXPERT_BLOCK * ROW_TILES, LANES), U32)
    return pl.pallas_call(
        _experts_kernel,
        grid_spec=pltpu.PrefetchScalarGridSpec(
            num_scalar_prefetch=4,
            grid=(n_blocks,),
            in_specs=[
                pl.BlockSpec(memory_space=pl.ANY),
                pl.BlockSpec((1, D_MODEL, D_EXPERT), wmap),
                pl.BlockSpec((1, D_MODEL, D_EXPERT), wmap),
                pl.BlockSpec((1, D_EXPERT, D_MODEL), wmap),
            ],
            out_specs=pl.BlockSpec(memory_space=pl.ANY),
            scratch_shapes=[
                buf, buf, buf, buf,
                pltpu.VMEM((D_MODEL, D_EXPERT), BF16),
                pltpu.VMEM((D_MODEL, D_EXPERT), BF16),
                pltpu.VMEM((D_EXPERT, D_MODEL), BF16),
                pltpu.SemaphoreType.DMA((2,)),
                pltpu.SemaphoreType.DMA((2,)),
            ],
        ),
        out_shape=jax.ShapeDtypeStruct((TOP_K * n_tok + 3 * EXPERT_BLOCK, ROW_TILES, LANES), U32),
        compiler_params=pltpu.CompilerParams(
            dimension_semantics=("arbitrary",), vmem_limit_bytes=VMEM_LIMIT),
        name="experts",
    )(block_expert, n_used, slot_tok, slot_row, h3, wg, wu, wd)


def _final_kernel(x1_ref, route_ref, nw_ref, y0_ref, y1_ref, op_ref, os_ref, *, blocks_p):
    i = pl.program_id(0)

    tm = x1_ref.shape[0]

    def rows(ref):
        return jnp.concatenate(_unpack_row_tiles(
            [ref[pl.ds(c, tm, stride=ROW_TILES), :] for c in range(ROW_TILES)]), axis=1)

    w1 = route_ref[:, 4:5]
    w2 = route_ref[:, 5:6]
    x2 = x1_ref[...] + (rows(y0_ref) * w1 + rows(y1_ref) * w2)
    out = _rms(x2, nw_ref[...])

    @pl.when(i < blocks_p)
    def _():
        op_ref[...] = out

    @pl.when(i >= blocks_p)
    def _():
        os_ref[...] = out


def _final(x1, route, norm_w, y, tp, tm):
    t = x1.shape[0]
    blocks_p = tp // tm
    blocks_t = t // tm
    return pl.pallas_call(
        functools.partial(_final_kernel, blocks_p=blocks_p),
        grid=(blocks_t,),
        in_specs=[
            pl.BlockSpec((tm, D_MODEL), lambda i: (i, 0)),
            pl.BlockSpec((tm, LANES), lambda i: (i, 0)),
            pl.BlockSpec((1, D_MODEL), lambda i: (0, 0)),
            pl.BlockSpec((tm * ROW_TILES, LANES), lambda i: (i, 0)),
            pl.BlockSpec((tm * ROW_TILES, LANES), lambda i: (i + blocks_t, 0)),
        ],
        out_specs=[
            pl.BlockSpec((tm, D_MODEL), lambda i: (jnp.minimum(i, blocks_p - 1), 0)),
            pl.BlockSpec((tm, D_MODEL), lambda i: (jnp.maximum(i - blocks_p, 0), 0)),
        ],
        out_shape=[
            jax.ShapeDtypeStruct((tp, D_MODEL), F32),
            jax.ShapeDtypeStruct((t - tp, D_MODEL), F32),
        ],
        compiler_params=pltpu.CompilerParams(dimension_semantics=("arbitrary",)),
        name="final",
    )(x1, route, norm_w, y, y)


def _rope_tables(seq_len):
    half = HEAD_DIM // 2
    inv = ROPE_BASE ** (-jnp.arange(half, dtype=F32) / half)
    ang = jnp.arange(seq_len, dtype=F32)[:, None] * inv[None, :]
    cos = jnp.cos(ang)
    sin = jnp.sin(ang)
    return jnp.concatenate([cos, cos], axis=1), jnp.concatenate([-sin, sin], axis=1)


def _tile(limit, *sizes):
    return min(limit, math.gcd(*sizes))


def kernel(x_prompt, x_sample, norm_mix, w_in, gmlp_norm_v, gmlp_w_spatial, gmlp_b_spatial, ret_decay_fwd, ret_decay_bwd, ret_norm, w_out, norm_ffn, w_router_group, b_router_group, w_router_expert, b_router_expert, w_expert_gate, w_expert_up, w_expert_down, norm_final):
    assert norm_mix.shape[0] == 1, "single-layer block"
    bp, sp, d = x_prompt.shape
    bs, ss, _ = x_sample.shape
    assert d == D_MODEL and sp % CHUNK == 0 and ss % CHUNK == 0
    tp, ts = bp * sp, bs * ss
    t = tp + ts
    xp = x_prompt.reshape(tp, d)
    xs = x_sample.reshape(ts, d)

    tm = _tile(1024, sp, ss)
    cos_t, sin_t = _rope_tables(max(sp, ss))
    w_in_b = w_in[0].astype(BF16)
    gv_w = gmlp_norm_v.reshape(1, SECTION)
    proj = _proj(xp, norm_mix, w_in_b, cos_t, sin_t, gv_w, None, t, 0, tm, sp // tm)
    proj = _proj(xs, norm_mix, w_in_b, cos_t, sin_t, gv_w, proj, t, tp // tm, tm, ss // tm)

    lgf = jax.nn.log_sigmoid(ret_decay_fwd[0].astype(F32))
    lgb = jax.nn.log_sigmoid(ret_decay_bwd[0].astype(F32))
    sf, sb = _states(proj, lgf, lgb, tp // CHUNK, sp // CHUNK, ss // CHUNK)
    mixed = _mix(proj, sf, sb, lgf, lgb, gmlp_w_spatial[0].astype(BF16),
                 gmlp_b_spatial[0][:, :, None], ret_norm.reshape(1, SECTION))

    pad = LANES - N_EXPERTS - N_GROUPS
    wr = jnp.concatenate([w_router_expert[0], w_router_group[0], jnp.zeros((d, pad), F32)], axis=1)
    br = jnp.concatenate([b_router_expert[0], b_router_group[0], jnp.zeros((pad,), F32)])[None, :]
    tm = _tile(256, tp, ts)
    x1, h3, route, route_t, cnt = _post(xp, xs, mixed, w_out[0].astype(BF16), norm_ffn,
                                        wr.astype(BF16), br, tm)

    n_blocks = -(-(TOP_K * t) // EXPERT_BLOCK) + N_EXPERTS
    dest, block_expert, n_used, pad_start, pad_count = _plan(
        cnt[0, :N_EXPERTS].astype(I32), route_t, n_blocks)
    slot_tok, slot_row = _invert(dest.reshape(TOP_K * t), pad_start, pad_count, t,
                                 n_blocks * EXPERT_BLOCK)
    y = _experts(h3.reshape(t, ROW_TILES, LANES), block_expert, n_used, slot_tok, slot_row,
                 w_expert_gate[0], w_expert_up[0], w_expert_down[0])

    out_p, out_s = _final(x1, route, norm_final[None, :], y.reshape(-1, LANES), tp, tm)
    return out_p.reshape(bp, sp, d), out_s.reshape(bs, ss, d)
```

```python
import functools
import math

import jax
import jax.numpy as jnp
from jax import lax
from jax.experimental import pallas as pl
from jax.experimental.pallas import tpu as pltpu

F32 = jnp.float32
BF16 = jnp.bfloat16
I32 = jnp.int32
U32 = jnp.uint32

D_MODEL = 2048
CHUNK = 128
HEADS = 8
HEAD_DIM = 128
SECTION = HEADS * HEAD_DIM
N_SECTIONS = 6
PROJ_WIDTH = N_SECTIONS * SECTION
ROPE_BASE = 10000.0
N_GROUPS = 4
EXPERTS_PER_GROUP = 8
N_EXPERTS = N_GROUPS * EXPERTS_PER_GROUP
TOP_K = 2
D_EXPERT = D_MODEL // 4
EPS = 1e-6
LANES = 128
SUBLANES = 8
MXU_COLS = 256
ROW_TILES = D_MODEL // LANES // 2
EXPERT_BLOCK = 256
DMA_UNROLL = 8
VMEM_LIMIT = 56 * 1024 * 1024


def _rms(x, w):
    ms = jnp.mean(x * x, axis=-1, keepdims=True)
    return x * lax.rsqrt(ms + EPS) * w


def _pack_row_tiles(x):
    def bits(c):
        return lax.bitcast_convert_type(x[:, c * LANES:(c + 1) * LANES].astype(BF16).astype(F32), U32)
    return [(bits(2 * c) >> 16) | bits(2 * c + 1) for c in range(ROW_TILES)]


def _unpack_row_tiles(words):
    tiles = []
    for w in words:
        tiles.append(lax.bitcast_convert_type(w << 16, F32))
        tiles.append(lax.bitcast_convert_type(w & jnp.uint32(0xFFFF0000), F32))
    return tiles


def _proj_kernel(x_ref, nw_ref, w_ref, cos_ref, sin_ref, gvw_ref, prev_ref, o_ref, h_ref):
    del prev_ref
    j = pl.program_id(1)

    @pl.when(j == 0)
    def _():
        h_ref[...] = _rms(x_ref[...], nw_ref[...]).astype(BF16)

    def rotary(seg):
        return seg * cos_ref[...] + pltpu.roll(seg, HEAD_DIM // 2, axis=1) * sin_ref[...]

    def section(epilogue):
        for c in range(SECTION // MXU_COLS):
            acc = jnp.dot(h_ref[...], w_ref[:, c * MXU_COLS:(c + 1) * MXU_COLS],
                          preferred_element_type=F32)
            for g in range(MXU_COLS // HEAD_DIM):
                sl = slice(c * MXU_COLS + g * HEAD_DIM, c * MXU_COLS + (g + 1) * HEAD_DIM)
                o_ref[:, sl] = epilogue(acc[:, g * HEAD_DIM:(g + 1) * HEAD_DIM], sl).astype(BF16)

    @pl.when(j == 0)
    def _():
        section(lambda seg, sl: jax.nn.gelu(seg))

    @pl.when(j == 1)
    def _():
        section(lambda seg, sl: _rms(jax.nn.gelu(seg), gvw_ref[:, sl]))

    @pl.when(j == 2)
    def _():
        section(lambda seg, sl: rotary(seg))

    @pl.when(j == 3)
    def _():
        section(lambda seg, sl: rotary(seg) * (HEAD_DIM ** -0.5))

    @pl.when(j == 4)
    def _():
        section(lambda seg, sl: seg)

    @pl.when(j == 5)
    def _():
        section(lambda seg, sl: jax.nn.silu(seg))


def _proj(x, norm_w, w_in, cos_t, sin_t, gv_w, prev, t_total, row_block0, tm, per_seq):
    if prev is None:
        prev = jnp.zeros((SUBLANES, LANES), BF16)
        aliases = {}
    else:
        aliases = {6: 0}
    return pl.pallas_call(
        _proj_kernel,
        grid=(x.shape[0] // tm, N_SECTIONS),
        in_specs=[
            pl.BlockSpec((tm, D_MODEL), lambda i, j: (i, 0)),
            pl.BlockSpec((1, D_MODEL), lambda i, j: (0, 0)),
            pl.BlockSpec((D_MODEL, SECTION), lambda i, j: (0, j)),
            pl.BlockSpec((tm, HEAD_DIM), lambda i, j: (i % per_seq, 0)),
            pl.BlockSpec((tm, HEAD_DIM), lambda i, j: (i % per_seq, 0)),
            pl.BlockSpec((1, SECTION), lambda i, j: (0, 0)),
            pl.BlockSpec(memory_space=pl.ANY),
        ],
        out_specs=pl.BlockSpec((None, tm, SECTION), lambda i, j: (j, i + row_block0, 0)),
        out_shape=jax.ShapeDtypeStruct((N_SECTIONS, t_total, SECTION), BF16),
        scratch_shapes=[pltpu.VMEM((tm, D_MODEL), BF16)],
        input_output_aliases=aliases,
        compiler_params=pltpu.CompilerParams(
            dimension_semantics=("arbitrary", "arbitrary"), vmem_limit_bytes=VMEM_LIMIT),
        name="proj",
    )(x, norm_w, w_in, cos_t, sin_t, gv_w, prev)


def _seq_edge(c, chunks_p, n_p, n_s, last):
    pos = jnp.where(c < chunks_p, c % n_p, (c - chunks_p) % n_s)
    edge = jnp.where(c < chunks_p, n_p - 1, n_s - 1) if last else 0
    return pos == edge


def _states_kernel(lgf_ref, lgb_ref, kf_ref, vf_ref, kb_ref, vb_ref, sf_ref, sb_ref, stf, stb,
                   *, group, n_chunks, chunks_p, n_p, n_s):
    s = pl.program_id(0)
    row = lax.broadcasted_iota(jnp.int32, (CHUNK, HEAD_DIM), 0).astype(F32)
    tn = (((0,), (0,)), ((), ()))

    def one_chunk(g, carry):
        cf = s * group + g
        gb = group - 1 - g

        @pl.when(_seq_edge(cf, chunks_p, n_p, n_s, last=False))
        def _():
            stf[...] = jnp.zeros_like(stf)

        @pl.when(_seq_edge(n_chunks - 1 - cf, chunks_p, n_p, n_s, last=True))
        def _():
            stb[...] = jnp.zeros_like(stb)

        rf = pl.ds(pl.multiple_of(g * CHUNK, CHUNK), CHUNK)
        rb = pl.ds(pl.multiple_of(gb * CHUNK, CHUNK), CHUNK)
        for h in range(HEADS):
            sl = slice(h * HEAD_DIM, (h + 1) * HEAD_DIM)
            lgf = lgf_ref[h]
            lgb = lgb_ref[h]
            kd = (kf_ref[rf, sl].astype(F32) * jnp.exp((CHUNK - 1.0 - row) * lgf)).astype(BF16)
            kv = lax.dot_general(kd, vf_ref[rf, sl], tn, preferred_element_type=F32)
            sf_ref[g, h] = stf[h].astype(BF16)
            stf[h] = stf[h] * jnp.exp(CHUNK * lgf) + kv
            kd = (kb_ref[rb, sl].astype(F32) * jnp.exp(row * lgb)).astype(BF16)
            kv = lax.dot_general(kd, vb_ref[rb, sl], tn, preferred_element_type=F32)
            sb_ref[gb, h] = stb[h].astype(BF16)
            stb[h] = stb[h] * jnp.exp(CHUNK * lgb) + kv
        return carry
    lax.fori_loop(0, group, one_chunk, 0)


def _states(proj, lgf, lgb, group, chunks_p, n_p, n_s):
    n_chunks = proj.shape[1] // CHUNK
    last = n_chunks // group - 1
    rows = group * CHUNK
    smem = pl.BlockSpec(memory_space=pltpu.SMEM)
    st_shape = jax.ShapeDtypeStruct((n_chunks, HEADS, HEAD_DIM, HEAD_DIM), BF16)
    return pl.pallas_call(
        functools.partial(_states_kernel, group=group, n_chunks=n_chunks, chunks_p=chunks_p,
                          n_p=n_p, n_s=n_s),
        grid=(n_chunks // group,),
        in_specs=[
            smem, smem,
            pl.BlockSpec((None, rows, SECTION), lambda s: (3, s, 0)),
            pl.BlockSpec((None, rows, SECTION), lambda s: (4, s, 0)),
            pl.BlockSpec((None, rows, SECTION), lambda s: (3, last - s, 0)),
            pl.BlockSpec((None, rows, SECTION), lambda s: (4, last - s, 0)),
        ],
        out_specs=[
            pl.BlockSpec((group, HEADS, HEAD_DIM, HEAD_DIM), lambda s: (s, 0, 0, 0)),
            pl.BlockSpec((group, HEADS, HEAD_DIM, HEAD_DIM), lambda s: (last - s, 0, 0, 0)),
        ],
        out_shape=[st_shape, st_shape],
        scratch_shapes=[pltpu.VMEM((HEADS, HEAD_DIM, HEAD_DIM), F32),
                        pltpu.VMEM((HEADS, HEAD_DIM, HEAD_DIM), F32)],
        compiler_params=pltpu.CompilerParams(dimension_semantics=("arbitrary",)),
        name="states",
    )(lgf, lgb, proj, proj, proj, proj)


def _mix_kernel(lgf_ref, lgb_ref, u_ref, gv_ref, q_ref, k_ref, v_ref, g_ref, sf_ref, sb_ref,
                ws_ref, bs_ref, rn_ref, o_ref, dec_ref, qdf_ref, qdb_ref):
    @pl.when(pl.program_id(0) == 0)
    def _():
        row = lax.broadcasted_iota(jnp.int32, (CHUNK, CHUNK), 0).astype(F32)
        col = lax.broadcasted_iota(jnp.int32, (CHUNK, CHUNK), 1).astype(F32)
        diff = row - col
        for h in range(HEADS):
            lgf = lgf_ref[h]
            lgb = lgb_ref[h]
            dec_ref[h] = jnp.where(diff >= 0, jnp.exp(jnp.maximum(diff, 0.0) * lgf),
                                   jnp.exp(jnp.maximum(-diff, 0.0) * lgb))
            qdf_ref[h] = jnp.exp((row + 1.0) * lgf)
            qdb_ref[h] = jnp.exp((CHUNK - row) * lgb)

    nt = (((1,), (1,)), ((), ()))

    def one_chunk(g, carry):
        rows = pl.ds(pl.multiple_of(g * CHUNK, CHUNK), CHUNK)
        for h in range(HEADS):
            sl = slice(h * HEAD_DIM, (h + 1) * HEAD_DIM)
            mixed = jnp.dot(ws_ref[h], gv_ref[rows, sl], preferred_element_type=F32) + bs_ref[h]
            o_ref[rows, sl] = (u_ref[rows, sl].astype(F32) * mixed).astype(BF16)
            q = q_ref[rows, sl]
            v = v_ref[rows, sl]
            scores = lax.dot_general(q, k_ref[rows, sl], nt, preferred_element_type=F32)
            ret = jnp.dot((scores * dec_ref[h]).astype(BF16), v, preferred_element_type=F32)
            ret += jnp.dot(q, sf_ref[g, h], preferred_element_type=F32) * qdf_ref[h]
            ret += jnp.dot(q, sb_ref[g, h], preferred_element_type=F32) * qdb_ref[h]
            out = _rms(ret, rn_ref[:, sl]) * g_ref[rows, sl].astype(F32)
            o_ref[rows, SECTION + h * HEAD_DIM:SECTION + (h + 1) * HEAD_DIM] = out.astype(BF16)
        return carry
    lax.fori_loop(0, u_ref.shape[0] // CHUNK, one_chunk, 0)


def _mix(proj, sf, sb, lgf, lgb, ws, bs, rn, group):
    t = proj.shape[1]
    rows = group * CHUNK
    smem = pl.BlockSpec(memory_space=pltpu.SMEM)

    def sec(j):
        return pl.BlockSpec((None, rows, SECTION), lambda c, j=j: (j, c, 0))

    st_spec = pl.BlockSpec((group, HEADS, HEAD_DIM, HEAD_DIM), lambda c: (c, 0, 0, 0))
    tab = pltpu.VMEM((HEADS, CHUNK, CHUNK), F32)
    return pl.pallas_call(
        _mix_kernel,
        grid=(t // rows,),
        in_specs=[
            smem, smem, sec(0), sec(1), sec(2), sec(3), sec(4), sec(5), st_spec, st_spec,
            pl.BlockSpec((HEADS, CHUNK, CHUNK), lambda c: (0, 0, 0)),
            pl.BlockSpec((HEADS, CHUNK, 1), lambda c: (0, 0, 0)),
            pl.BlockSpec((1, SECTION), lambda c: (0, 0)),
        ],
        out_specs=pl.BlockSpec((rows, 2 * SECTION), lambda c: (c, 0)),
        out_shape=jax.ShapeDtypeStruct((t, 2 * SECTION), BF16),
        scratch_shapes=[tab, tab, tab],
        compiler_params=pltpu.CompilerParams(dimension_semantics=("arbitrary",)),
        name="mix",
    )(lgf, lgb, proj, proj, proj, proj, proj, proj, sf, sb, ws, bs, rn)


def _post_kernel(xp_ref, xs_ref, m_ref, wo_ref, nw_ref, wr_ref, br_ref,
                 x1_ref, h3_ref, route_ref, rt_ref, cnt_ref, *, tm, blocks_p):
    i = pl.program_id(0)

    @pl.when(i == 0)
    def _():
        cnt_ref[...] = jnp.zeros_like(cnt_ref)

    x = jnp.where(i < blocks_p, xp_ref[...], xs_ref[...])
    x1 = x + jnp.dot(m_ref[...], wo_ref[...], preferred_element_type=F32)
    x1_ref[...] = x1
    h = _rms(x1, nw_ref[...])
    for c, words in enumerate(_pack_row_tiles(h)):
        h3_ref[pl.ds(c, tm, stride=ROW_TILES), :] = words

    logits = jnp.dot(h.astype(BF16), wr_ref[...], preferred_element_type=F32) + br_ref[...]
    lane = lax.broadcasted_iota(jnp.int32, (tm, LANES), 1).astype(F32)
    neg = jnp.float32(-jnp.inf)

    def first_max(vals):
        m = jnp.max(vals, axis=-1, keepdims=True)
        idx = jnp.min(jnp.where(vals == m, lane, float(LANES)), axis=-1, keepdims=True)
        return m, idx

    lg = jnp.where((lane >= N_EXPERTS) & (lane < N_EXPERTS + N_GROUPS), logits, neg)
    gmax, gidx = first_max(lg)
    p_sel = 1.0 / jnp.sum(jnp.exp(lg - gmax), axis=-1, keepdims=True)
    lo = (gidx - N_EXPERTS) * EXPERTS_PER_GROUP
    le = jnp.where((lane >= lo) & (lane < lo + EXPERTS_PER_GROUP), logits, neg)
    top1, i1 = first_max(le)
    top2, i2 = first_max(jnp.where(lane == i1, neg, le))
    e2 = jnp.exp(top2 - top1)
    w1 = p_sel / (1.0 + e2)
    w2 = p_sel * e2 / (1.0 + e2)

    hit1 = lane == i1
    hit2 = lane == i2
    onehot = jnp.where(hit1 | hit2, 1.0, 0.0).astype(BF16)
    r = lax.broadcasted_iota(jnp.int32, (tm, tm), 0)
    c = lax.broadcasted_iota(jnp.int32, (tm, tm), 1)
    lower = jnp.where(c < r, 1.0, 0.0).astype(BF16)
    before = jnp.dot(lower, onehot, preferred_element_type=F32) + cnt_ref[...]
    rank1 = jnp.sum(jnp.where(hit1, before, 0.0), axis=-1, keepdims=True)
    rank2 = jnp.sum(jnp.where(hit2, before, 0.0), axis=-1, keepdims=True)
    cnt_ref[...] += jnp.sum(onehot.astype(F32), axis=0, keepdims=True)

    route = jnp.where(lane == 0, i1, 0.0)
    route = jnp.where(lane == 1, i2, route)
    route = jnp.where(lane == 2, rank1, route)
    route = jnp.where(lane == 3, rank2, route)
    route = jnp.where(lane == 4, w1, route)
    route = jnp.where(lane == 5, w2, route)
    route_ref[...] = route
    rt_ref[...] = jnp.transpose(route)[0:SUBLANES, :]


def _post(xp, xs, mixed, w_out, norm_w, wr, br, tm):
    t = mixed.shape[0]
    blocks_p = xp.shape[0] // tm
    return pl.pallas_call(
        functools.partial(_post_kernel, tm=tm, blocks_p=blocks_p),
        grid=(t // tm,),
        in_specs=[
            pl.BlockSpec((tm, D_MODEL), lambda i: (jnp.minimum(i, blocks_p - 1), 0)),
            pl.BlockSpec((tm, D_MODEL), lambda i: (jnp.maximum(i - blocks_p, 0), 0)),
            pl.BlockSpec((tm, D_MODEL), lambda i: (i, 0)),
            pl.BlockSpec((D_MODEL, D_MODEL), lambda i: (0, 0)),
            pl.BlockSpec((1, D_MODEL), lambda i: (0, 0)),
            pl.BlockSpec((D_MODEL, LANES), lambda i: (0, 0)),
            pl.BlockSpec((1, LANES), lambda i: (0, 0)),
        ],
        out_specs=[
            pl.BlockSpec((tm, D_MODEL), lambda i: (i, 0)),
            pl.BlockSpec((tm * ROW_TILES, LANES), lambda i: (i, 0)),
            pl.BlockSpec((tm, LANES), lambda i: (i, 0)),
            pl.BlockSpec((SUBLANES, tm), lambda i: (0, i)),
            pl.BlockSpec((1, LANES), lambda i: (0, 0)),
        ],
        out_shape=[
            jax.ShapeDtypeStruct((t, D_MODEL), F32),
            jax.ShapeDtypeStruct((t * ROW_TILES, LANES), U32),
            jax.ShapeDtypeStruct((t, LANES), F32),
            jax.ShapeDtypeStruct((SUBLANES, t), F32),
            jax.ShapeDtypeStruct((1, LANES), F32),
        ],
        compiler_params=pltpu.CompilerParams(
            dimension_semantics=("arbitrary",), vmem_limit_bytes=VMEM_LIMIT),
        name="post",
    )(xp, xs, mixed, w_out, norm_w, wr, br)


def _plan_kernel(cnt_ref, rt_ref, dest_ref, be_ref, nu_ref, zst_ref, zcn_ref, ps_ref, *, n_blocks):
    def per_expert(e, first):
        n = (cnt_ref[e] + (EXPERT_BLOCK - 1)) // EXPERT_BLOCK
        ps_ref[e] = first * EXPERT_BLOCK
        zst_ref[e] = first * EXPERT_BLOCK + cnt_ref[e]
        zcn_ref[e] = n * EXPERT_BLOCK - cnt_ref[e]

        def fill(j, carry):
            be_ref[first + j] = e
            return carry
        lax.fori_loop(0, n, fill, 0)
        return first + n
    n_used = lax.fori_loop(0, N_EXPERTS, per_expert, 0)
    nu_ref[0] = n_used

    def fill_tail(j, carry):
        be_ref[j] = N_EXPERTS - 1
        return carry
    lax.fori_loop(n_used, n_blocks, fill_tail, 0)

    for k in range(TOP_K):
        e = rt_ref[k:k + 1, :]
        base = jnp.zeros_like(e)
        for x in range(N_EXPERTS):
            base = jnp.where(e == float(x), ps_ref[x].astype(F32), base)
        dest_ref[k:k + 1, :] = (base + rt_ref[TOP_K + k:TOP_K + k + 1, :]).astype(I32)


def _plan(counts, route_t, n_blocks):
    t = route_t.shape[1]
    smem = pl.BlockSpec(memory_space=pltpu.SMEM)
    return pl.pallas_call(
        functools.partial(_plan_kernel, n_blocks=n_blocks),
        grid=(1,),
        in_specs=[smem, pl.BlockSpec((SUBLANES, t), lambda i: (0, 0))],
        out_specs=[pl.BlockSpec((TOP_K, t), lambda i: (0, 0)), smem, smem, smem, smem],
        out_shape=[
            jax.ShapeDtypeStruct((TOP_K, t), I32),
            jax.ShapeDtypeStruct((n_blocks,), I32),
            jax.ShapeDtypeStruct((1,), I32),
            jax.ShapeDtypeStruct((N_EXPERTS,), I32),
            jax.ShapeDtypeStruct((N_EXPERTS,), I32),
        ],
        scratch_shapes=[pltpu.SMEM((N_EXPERTS,), I32)],
        compiler_params=pltpu.CompilerParams(dimension_semantics=("arbitrary",)),
        name="plan",
    )(counts, route_t)


def _invert_kernel(dest_ref, zst_ref, zcn_ref, tok_ref, row_ref, *, n_tok):
    n_assign = TOP_K * n_tok

    def lead(j, carry):
        row_ref[j] = n_assign + 2 * EXPERT_BLOCK + j
        return carry
    lax.fori_loop(0, EXPERT_BLOCK, lead, 0)

    def pad_expert(e, carry):
        start = zst_ref[e]

        def one(j, c):
            p = start + j
            tok_ref[p] = 0
            row_ref[EXPERT_BLOCK + p] = n_assign + (p & (2 * EXPERT_BLOCK - 1))
            return c
        lax.fori_loop(0, zcn_ref[e], one, 0)
        return carry
    lax.fori_loop(0, N_EXPERTS, pad_expert, 0)

    for k in range(TOP_K):
        def fill(i, carry, k=k):
            for u in range(DMA_UNROLL):
                tok = i * DMA_UNROLL + u
                p = dest_ref[k * n_tok + tok]
                tok_ref[p] = tok
                row_ref[EXPERT_BLOCK + p] = k * n_tok + tok
            return carry
        lax.fori_loop(0, n_tok // DMA_UNROLL, fill, 0)


def _invert(dest_flat, pad_start, pad_count, n_tok, n_slots):
    assert EXPERT_BLOCK & (EXPERT_BLOCK - 1) == 0
    smem = pl.BlockSpec(memory_space=pltpu.SMEM)
    return pl.pallas_call(
        functools.partial(_invert_kernel, n_tok=n_tok),
        grid_spec=pltpu.PrefetchScalarGridSpec(
            num_scalar_prefetch=3,
            grid=(1,),
            in_specs=[],
            out_specs=[smem, smem],
        ),
        out_shape=[jax.ShapeDtypeStruct((n_slots,), I32),
                   jax.ShapeDtypeStruct((n_slots + EXPERT_BLOCK,), I32)],
        compiler_params=pltpu.CompilerParams(dimension_semantics=("arbitrary",)),
        name="invert",
    )(dest_flat, pad_start, pad_count)


def _experts_kernel(be_ref, nu_ref, tok_ref, row_ref, h3_ref, wg_ref, wu_ref, wd_ref, y_ref,
                    xb0, xb1, yb0, yb1, wgb, wub, wdb, gsem, ssem):
    b = pl.program_id(0)
    n_used = nu_ref[0]
    blk = EXPERT_BLOCK

    def slab(buf, r):
        start = r * ROW_TILES
        if not isinstance(r, int):
            start = pl.multiple_of(start, ROW_TILES)
        return buf.at[pl.ds(start, ROW_TILES)]

    def gather_copy(tok, buf, r, par):
        return pltpu.make_async_copy(h3_ref.at[tok], slab(buf, r), gsem.at[par])

    def scatter_copy(dst, buf, r, par):
        return pltpu.make_async_copy(slab(buf, r), y_ref.at[dst], ssem.at[par])

    def rows(fn):
        def body(i, carry):
            for u in range(DMA_UNROLL):
                fn(i * DMA_UNROLL + u)
            return carry
        lax.fori_loop(0, blk // DMA_UNROLL, body, 0)

    def gather_wait(buf, par):
        rows(lambda r: gather_copy(0, buf, r, par).wait())

    def scatter_wait(buf, par):
        rows(lambda r: scatter_copy(0, buf, r, par).wait())

    def step(par, x_cur, x_nxt, y_cur, y_oth):
        @pl.when(b == 0)
        def _():
            rows(lambda r: gather_copy(tok_ref[r], x_cur, r, par).start())
            y_oth[...] = jnp.zeros_like(y_oth)

        @pl.when((b == 0) | (be_ref[b] != be_ref[jnp.maximum(b - 1, 0)]))
        def _():
            wgb[...] = wg_ref[0].astype(BF16)
            wub[...] = wu_ref[0].astype(BF16)
            wdb[...] = wd_ref[0].astype(BF16)

        gather_wait(x_cur, par)

        @pl.when(b >= 1)
        def _():
            scatter_wait(y_cur, par)

        x = jnp.concatenate(_unpack_row_tiles(
            [x_cur[pl.ds(c, blk, stride=ROW_TILES), :] for c in range(ROW_TILES)]), axis=1
        ).astype(BF16)
        nxt = jnp.minimum(b + 1, n_used - 1) * blk
        for r in range(blk):
            gather_copy(tok_ref[nxt + r], x_nxt, r, 1 - par).start(priority=0)
            scatter_copy(row_ref[b * blk + r], y_oth, r, 1 - par).start(priority=1)
        gate = jnp.dot(x, wgb[...], preferred_element_type=F32)
        up = jnp.dot(x, wub[...], preferred_element_type=F32)
        hid = (jax.nn.silu(gate) * up).astype(BF16)
        y = jnp.dot(hid, wdb[...], preferred_element_type=F32)
        for c, words in enumerate(_pack_row_tiles(y)):
            y_cur[pl.ds(c, blk, stride=ROW_TILES), :] = words

    def drain(par, x_cur, y_cur, y_oth):
        gather_wait(x_cur, par)
        rows(lambda r: scatter_copy(row_ref[b * blk + r], y_oth, r, 1 - par).start())
        scatter_wait(y_cur, par)
        scatter_wait(y_oth, 1 - par)

    for par, bufs in ((0, (xb0, xb1, yb0, yb1)), (1, (xb1, xb0, yb1, yb0))):
        @pl.when((b < n_used) & (b % 2 == par))
        def _(par=par, bufs=bufs):
            step(par, *bufs)

        @pl.when((b == n_used) & (b % 2 == par))
        def _(par=par, bufs=bufs):
            drain(par, bufs[0], bufs[2], bufs[3])


def _experts(h3, block_expert, n_used, slot_tok, slot_row, wg, wu, wd):
    n_blocks = block_expert.shape[0]
    n_tok = h3.shape[0]

    def wmap(b, be, nu, st, sr):
        return (be[jnp.minimum(b, nu[0] - 1)], 0, 0)

    buf = pltpu.VMEM((EXPERT_BLOCK * ROW_TILES, LANES), U32)
    return pl.pallas_call(
        _experts_kernel,
        grid_spec=pltpu.PrefetchScalarGridSpec(
            num_scalar_prefetch=4,
            grid=(n_blocks,),
            in_specs=[
                pl.BlockSpec(memory_space=pl.ANY),
                pl.BlockSpec((1, D_MODEL, D_EXPERT), wmap),
                pl.BlockSpec((1, D_MODEL, D_EXPERT), wmap),
                pl.BlockSpec((1, D_EXPERT, D_MODEL), wmap),
            ],
            out_specs=pl.BlockSpec(memory_space=pl.ANY),
            scratch_shapes=[
                buf, buf, buf, buf,
                pltpu.VMEM((D_MODEL, D_EXPERT), BF16),
                pltpu.VMEM((D_MODEL, D_EXPERT), BF16),
                pltpu.VMEM((D_EXPERT, D_MODEL), BF16),
                pltpu.SemaphoreType.DMA((2,)),
                pltpu.SemaphoreType.DMA((2,)),
            ],
        ),
        out_shape=jax.ShapeDtypeStruct((TOP_K * n_tok + 3 * EXPERT_BLOCK, ROW_TILES, LANES), U32),
        compiler_params=pltpu.CompilerParams(
            dimension_semantics=("arbitrary",), vmem_limit_bytes=VMEM_LIMIT),
        name="experts",
    )(block_expert, n_used, slot_tok, slot_row, h3, wg, wu, wd)


def _final_kernel(x1_ref, route_ref, nw_ref, y0_ref, y1_ref, op_ref, os_ref, *, blocks_p):
    i = pl.program_id(0)

    tm = x1_ref.shape[0]

    def rows(ref):
        return jnp.concatenate(_unpack_row_tiles(
            [ref[pl.ds(c, tm, stride=ROW_TILES), :] for c in range(ROW_TILES)]), axis=1)

    w1 = route_ref[:, 4:5]
    w2 = route_ref[:, 5:6]
    x2 = x1_ref[...] + (rows(y0_ref) * w1 + rows(y1_ref) * w2)
    out = _rms(x2, nw_ref[...])

    @pl.when(i < blocks_p)
    def _():
        op_ref[...] = out

    @pl.when(i >= blocks_p)
    def _():
        os_ref[...] = out


def _final(x1, route, norm_w, y, tp, tm):
    t = x1.shape[0]
    blocks_p = tp // tm
    blocks_t = t // tm
    return pl.pallas_call(
        functools.partial(_final_kernel, blocks_p=blocks_p),
        grid=(blocks_t,),
        in_specs=[
            pl.BlockSpec((tm, D_MODEL), lambda i: (i, 0)),
            pl.BlockSpec((tm, LANES), lambda i: (i, 0)),
            pl.BlockSpec((1, D_MODEL), lambda i: (0, 0)),
            pl.BlockSpec((tm * ROW_TILES, LANES), lambda i: (i, 0)),
            pl.BlockSpec((tm * ROW_TILES, LANES), lambda i: (i + blocks_t, 0)),
        ],
        out_specs=[
            pl.BlockSpec((tm, D_MODEL), lambda i: (jnp.minimum(i, blocks_p - 1), 0)),
            pl.BlockSpec((tm, D_MODEL), lambda i: (jnp.maximum(i - blocks_p, 0), 0)),
        ],
        out_shape=[
            jax.ShapeDtypeStruct((tp, D_MODEL), F32),
            jax.ShapeDtypeStruct((t - tp, D_MODEL), F32),
        ],
        compiler_params=pltpu.CompilerParams(dimension_semantics=("arbitrary",)),
        name="final",
    )(x1, route, norm_w, y, y)


def _rope_tables(seq_len):
    half = HEAD_DIM // 2
    inv = ROPE_BASE ** (-jnp.arange(half, dtype=F32) / half)
    ang = jnp.arange(seq_len, dtype=F32)[:, None] * inv[None, :]
    cos = jnp.cos(ang)
    sin = jnp.sin(ang)
    return jnp.concatenate([cos, cos], axis=1), jnp.concatenate([-sin, sin], axis=1)


def _tile(limit, *sizes):
    return min(limit, math.gcd(*sizes))


def kernel(x_prompt, x_sample, norm_mix, w_in, gmlp_norm_v, gmlp_w_spatial, gmlp_b_spatial, ret_decay_fwd, ret_decay_bwd, ret_norm, w_out, norm_ffn, w_router_group, b_router_group, w_router_expert, b_router_expert, w_expert_gate, w_expert_up, w_expert_down, norm_final):
    assert norm_mix.shape[0] == 1, "single-layer block"
    bp, sp, d = x_prompt.shape
    bs, ss, _ = x_sample.shape
    assert d == D_MODEL and sp % CHUNK == 0 and ss % CHUNK == 0
    tp, ts = bp * sp, bs * ss
    t = tp + ts
    xp = x_prompt.reshape(tp, d)
    xs = x_sample.reshape(ts, d)

    tm = _tile(1024, sp, ss)
    cos_t, sin_t = _rope_tables(max(sp, ss))
    w_in_b = w_in[0].astype(BF16)
    gv_w = gmlp_norm_v.reshape(1, SECTION)
    proj = _proj(xp, norm_mix, w_in_b, cos_t, sin_t, gv_w, None, t, 0, tm, sp // tm)
    proj = _proj(xs, norm_mix, w_in_b, cos_t, sin_t, gv_w, proj, t, tp // tm, tm, ss // tm)

    lgf = jax.nn.log_sigmoid(ret_decay_fwd[0].astype(F32))
    lgb = jax.nn.log_sigmoid(ret_decay_bwd[0].astype(F32))
    group = _tile(4, sp // CHUNK, ss // CHUNK)
    sf, sb = _states(proj, lgf, lgb, group, tp // CHUNK, sp // CHUNK, ss // CHUNK)
    mixed = _mix(proj, sf, sb, lgf, lgb, gmlp_w_spatial[0].astype(BF16),
                 gmlp_b_spatial[0][:, :, None], ret_norm.reshape(1, SECTION), group)

    pad = LANES - N_EXPERTS - N_GROUPS
    wr = jnp.concatenate([w_router_expert[0], w_router_group[0], jnp.zeros((d, pad), F32)], axis=1)
    br = jnp.concatenate([b_router_expert[0], b_router_group[0], jnp.zeros((pad,), F32)])[None, :]
    tm = _tile(256, tp, ts)
    x1, h3, route, route_t, cnt = _post(xp, xs, mixed, w_out[0].astype(BF16), norm_ffn,
                                        wr.astype(BF16), br, tm)

    n_blocks = -(-(TOP_K * t) // EXPERT_BLOCK) + N_EXPERTS
    dest, block_expert, n_used, pad_start, pad_count = _plan(
        cnt[0, :N_EXPERTS].astype(I32), route_t, n_blocks)
    slot_tok, slot_row = _invert(dest.reshape(TOP_K * t), pad_start, pad_count, t,
                                 n_blocks * EXPERT_BLOCK)
    y = _experts(h3.reshape(t, ROW_TILES, LANES), block_expert, n_used, slot_tok, slot_row,
                 w_expert_gate[0], w_expert_up[0], w_expert_down[0])

    out_p, out_s = _final(x1, route, norm_final[None, :], y.reshape(-1, LANES), tp, tm)
    return out_p.reshape(bp, sp, d), out_s.reshape(bs, ss, d)
```

```python
import functools
import math

import jax
import jax.numpy as jnp
from jax import lax
from jax.experimental import pallas as pl
from jax.experimental.pallas import tpu as pltpu

F32 = jnp.float32
BF16 = jnp.bfloat16
I32 = jnp.int32
U32 = jnp.uint32

D_MODEL = 2048
CHUNK = 128
HEADS = 8
HEAD_DIM = 128
SECTION = HEADS * HEAD_DIM
N_SECTIONS = 6
PROJ_WIDTH = N_SECTIONS * SECTION
ROPE_BASE = 10000.0
N_GROUPS = 4
EXPERTS_PER_GROUP = 8
N_EXPERTS = N_GROUPS * EXPERTS_PER_GROUP
TOP_K = 2
D_EXPERT = D_MODEL // 4
EPS = 1e-6
LANES = 128
SUBLANES = 8
MXU_COLS = 256
ROW_TILES = D_MODEL // LANES // 2
EXPERT_BLOCK = 256
DMA_UNROLL = 8
VMEM_LIMIT = 56 * 1024 * 1024


def _rms(x, w):
    ms = jnp.mean(x * x, axis=-1, keepdims=True)
    return x * lax.rsqrt(ms + EPS) * w


def _pack_row_tiles(x):
    def bits(c):
        return lax.bitcast_convert_type(x[:, c * LANES:(c + 1) * LANES].astype(BF16).astype(F32), U32)
    return [(bits(2 * c) >> 16) | bits(2 * c + 1) for c in range(ROW_TILES)]


def _unpack_row_tiles(words):
    tiles = []
    for w in words:
        tiles.append(lax.bitcast_convert_type(w << 16, F32))
        tiles.append(lax.bitcast_convert_type(w & jnp.uint32(0xFFFF0000), F32))
    return tiles


def _proj_kernel(x_ref, nw_ref, w_ref, cos_ref, sin_ref, gvw_ref, prev_ref, o_ref, h_ref):
    del prev_ref
    j = pl.program_id(1)

    @pl.when(j == 0)
    def _():
        h_ref[...] = _rms(x_ref[...], nw_ref[...]).astype(BF16)

    def rotary(seg):
        return seg * cos_ref[...] + pltpu.roll(seg, HEAD_DIM // 2, axis=1) * sin_ref[...]

    def section(epilogue):
        for c in range(SECTION // MXU_COLS):
            acc = jnp.dot(h_ref[...], w_ref[:, c * MXU_COLS:(c + 1) * MXU_COLS],
                          preferred_element_type=F32)
            for g in range(MXU_COLS // HEAD_DIM):
                sl = slice(c * MXU_COLS + g * HEAD_DIM, c * MXU_COLS + (g + 1) * HEAD_DIM)
                o_ref[:, sl] = epilogue(acc[:, g * HEAD_DIM:(g + 1) * HEAD_DIM], sl).astype(BF16)

    @pl.when(j == 0)
    def _():
        section(lambda seg, sl: jax.nn.gelu(seg))

    @pl.when(j == 1)
    def _():
        section(lambda seg, sl: _rms(jax.nn.gelu(seg), gvw_ref[:, sl]))

    @pl.when(j == 2)
    def _():
        section(lambda seg, sl: rotary(seg))

    @pl.when(j == 3)
    def _():
        section(lambda seg, sl: rotary(seg) * (HEAD_DIM ** -0.5))

    @pl.when(j == 4)
    def _():
        section(lambda seg, sl: seg)

    @pl.when(j == 5)
    def _():
        section(lambda seg, sl: jax.nn.silu(seg))


def _proj(x, norm_w, w_in, cos_t, sin_t, gv_w, prev, t_total, row_block0, tm, per_seq):
    if prev is None:
        prev = jnp.zeros((SUBLANES, LANES), BF16)
        aliases = {}
    else:
        aliases = {6: 0}
    return pl.pallas_call(
        _proj_kernel,
        grid=(x.shape[0] // tm, N_SECTIONS),
        in_specs=[
            pl.BlockSpec((tm, D_MODEL), lambda i, j: (i, 0)),
            pl.BlockSpec((1, D_MODEL), lambda i, j: (0, 0)),
            pl.BlockSpec((D_MODEL, SECTION), lambda i, j: (0, j)),
            pl.BlockSpec((tm, HEAD_DIM), lambda i, j: (i % per_seq, 0)),
            pl.BlockSpec((tm, HEAD_DIM), lambda i, j: (i % per_seq, 0)),
            pl.BlockSpec((1, SECTION), lambda i, j: (0, 0)),
            pl.BlockSpec(memory_space=pl.ANY),
        ],
        out_specs=pl.BlockSpec((None, tm, SECTION), lambda i, j: (j, i + row_block0, 0)),
        out_shape=jax.ShapeDtypeStruct((N_SECTIONS, t_total, SECTION), BF16),
        scratch_shapes=[pltpu.VMEM((tm, D_MODEL), BF16)],
        input_output_aliases=aliases,
        compiler_params=pltpu.CompilerParams(
            dimension_semantics=("arbitrary", "arbitrary"), vmem_limit_bytes=VMEM_LIMIT),
        name="proj",
    )(x, norm_w, w_in, cos_t, sin_t, gv_w, prev)


def _seq_edge(c, chunks_p, n_p, n_s, last):
    pos = jnp.where(c < chunks_p, c % n_p, (c - chunks_p) % n_s)
    edge = jnp.where(c < chunks_p, n_p - 1, n_s - 1) if last else 0
    return pos == edge


def _states_kernel(lgf_ref, lgb_ref, kf_ref, vf_ref, kb_ref, vb_ref, sf_ref, sb_ref, stf, stb,
                   *, group, n_chunks, chunks_p, n_p, n_s):
    s = pl.program_id(0)
    row = lax.broadcasted_iota(jnp.int32, (CHUNK, HEAD_DIM), 0).astype(F32)
    tn = (((0,), (0,)), ((), ()))

    def one_chunk(g, carry):
        cf = s * group + g
        gb = group - 1 - g

        @pl.when(_seq_edge(cf, chunks_p, n_p, n_s, last=False))
        def _():
            stf[...] = jnp.zeros_like(stf)

        @pl.when(_seq_edge(n_chunks - 1 - cf, chunks_p, n_p, n_s, last=True))
        def _():
            stb[...] = jnp.zeros_like(stb)

        rf = pl.ds(pl.multiple_of(g * CHUNK, CHUNK), CHUNK)
        rb = pl.ds(pl.multiple_of(gb * CHUNK, CHUNK), CHUNK)
        for h in range(HEADS):
            sl = slice(h * HEAD_DIM, (h + 1) * HEAD_DIM)
            lgf = lgf_ref[h]
            lgb = lgb_ref[h]
            kd = (kf_ref[rf, sl].astype(F32) * jnp.exp((CHUNK - 1.0 - row) * lgf)).astype(BF16)
            kv = lax.dot_general(kd, vf_ref[rf, sl], tn, preferred_element_type=F32)
            sf_ref[g, h] = stf[h].astype(BF16)
            stf[h] = stf[h] * jnp.exp(CHUNK * lgf) + kv
            kd = (kb_ref[rb, sl].astype(F32) * jnp.exp(row * lgb)).astype(BF16)
            kv = lax.dot_general(kd, vb_ref[rb, sl], tn, preferred_element_type=F32)
            sb_ref[gb, h] = stb[h].astype(BF16)
            stb[h] = stb[h] * jnp.exp(CHUNK * lgb) + kv
        return carry
    lax.fori_loop(0, group, one_chunk, 0)


def _states(proj, lgf, lgb, group, chunks_p, n_p, n_s):
    n_chunks = proj.shape[1] // CHUNK
    last = n_chunks // group - 1
    rows = group * CHUNK
    smem = pl.BlockSpec(memory_space=pltpu.SMEM)
    st_shape = jax.ShapeDtypeStruct((n_chunks, HEADS, HEAD_DIM, HEAD_DIM), BF16)
    return pl.pallas_call(
        functools.partial(_states_kernel, group=group, n_chunks=n_chunks, chunks_p=chunks_p,
                          n_p=n_p, n_s=n_s),
        grid=(n_chunks // group,),
        in_specs=[
            smem, smem,
            pl.BlockSpec((None, rows, SECTION), lambda s: (3, s, 0)),
            pl.BlockSpec((None, rows, SECTION), lambda s: (4, s, 0)),
            pl.BlockSpec((None, rows, SECTION), lambda s: (3, last - s, 0)),
            pl.BlockSpec((None, rows, SECTION), lambda s: (4, last - s, 0)),
        ],
        out_specs=[
            pl.BlockSpec((group, HEADS, HEAD_DIM, HEAD_DIM), lambda s: (s, 0, 0, 0)),
            pl.BlockSpec((group, HEADS, HEAD_DIM, HEAD_DIM), lambda s: (last - s, 0, 0, 0)),
        ],
        out_shape=[st_shape, st_shape],
        scratch_shapes=[pltpu.VMEM((HEADS, HEAD_DIM, HEAD_DIM), F32),
                        pltpu.VMEM((HEADS, HEAD_DIM, HEAD_DIM), F32)],
        compiler_params=pltpu.CompilerParams(dimension_semantics=("arbitrary",)),
        name="states",
    )(lgf, lgb, proj, proj, proj, proj)


def _mix_kernel(lgf_ref, lgb_ref, u_ref, gv_ref, q_ref, k_ref, v_ref, g_ref, sf_ref, sb_ref,
                ws_ref, bs_ref, rn_ref, o_ref, dec_ref, qdf_ref, qdb_ref):
    @pl.when(pl.program_id(0) == 0)
    def _():
        row = lax.broadcasted_iota(jnp.int32, (CHUNK, CHUNK), 0).astype(F32)
        col = lax.broadcasted_iota(jnp.int32, (CHUNK, CHUNK), 1).astype(F32)
        diff = row - col
        for h in range(HEADS):
            lgf = lgf_ref[h]
            lgb = lgb_ref[h]
            dec_ref[h] = jnp.where(diff >= 0, jnp.exp(jnp.maximum(diff, 0.0) * lgf),
                                   jnp.exp(jnp.maximum(-diff, 0.0) * lgb))
            qdf_ref[h] = jnp.exp((row + 1.0) * lgf)
            qdb_ref[h] = jnp.exp((CHUNK - row) * lgb)

    nt = (((1,), (1,)), ((), ()))

    def one_chunk(g, carry):
        rows = pl.ds(pl.multiple_of(g * CHUNK, CHUNK), CHUNK)
        for h in range(HEADS):
            sl = slice(h * HEAD_DIM, (h + 1) * HEAD_DIM)
            mixed = jnp.dot(ws_ref[h], gv_ref[rows, sl], preferred_element_type=F32) + bs_ref[h]
            o_ref[rows, sl] = (u_ref[rows, sl].astype(F32) * mixed).astype(BF16)
            q = q_ref[rows, sl]
            v = v_ref[rows, sl]
            scores = lax.dot_general(q, k_ref[rows, sl], nt, preferred_element_type=F32)
            ret = jnp.dot((scores * dec_ref[h]).astype(BF16), v, preferred_element_type=F32)
            ret += jnp.dot(q, sf_ref[g, h], preferred_element_type=F32) * qdf_ref[h]
            ret += jnp.dot(q, sb_ref[g, h], preferred_element_type=F32) * qdb_ref[h]
            out = _rms(ret, rn_ref[:, sl]) * g_ref[rows, sl].astype(F32)
            o_ref[rows, SECTION + h * HEAD_DIM:SECTION + (h + 1) * HEAD_DIM] = out.astype(BF16)
        return carry
    lax.fori_loop(0, u_ref.shape[0] // CHUNK, one_chunk, 0)


def _mix(proj, sf, sb, lgf, lgb, ws, bs, rn, group):
    t = proj.shape[1]
    rows = group * CHUNK
    smem = pl.BlockSpec(memory_space=pltpu.SMEM)

    def sec(j):
        return pl.BlockSpec((None, rows, SECTION), lambda c, j=j: (j, c, 0))

    st_spec = pl.BlockSpec((group, HEADS, HEAD_DIM, HEAD_DIM), lambda c: (c, 0, 0, 0))
    tab = pltpu.VMEM((HEADS, CHUNK, CHUNK), F32)
    return pl.pallas_call(
        _mix_kernel,
        grid=(t // rows,),
        in_specs=[
            smem, smem, sec(0), sec(1), sec(2), sec(3), sec(4), sec(5), st_spec, st_spec,
            pl.BlockSpec((HEADS, CHUNK, CHUNK), lambda c: (0, 0, 0)),
            pl.BlockSpec((HEADS, CHUNK, 1), lambda c: (0, 0, 0)),
            pl.BlockSpec((1, SECTION), lambda c: (0, 0)),
        ],
        out_specs=pl.BlockSpec((rows, 2 * SECTION), lambda c: (c, 0)),
        out_shape=jax.ShapeDtypeStruct((t, 2 * SECTION), BF16),
        scratch_shapes=[tab, tab, tab],
        compiler_params=pltpu.CompilerParams(dimension_semantics=("arbitrary",)),
        name="mix",
    )(lgf, lgb, proj, proj, proj, proj, proj, proj, sf, sb, ws, bs, rn)


def _post_kernel(xp_ref, xs_ref, m_ref, wo_ref, nw_ref, wr_ref, br_ref,
                 x1_ref, h3_ref, route_ref, rt_ref, cnt_ref, x1_prev, *, tm, blocks_p, blocks_t):
    i = pl.program_id(0)

    @pl.when(i == 0)
    def _():
        cnt_ref[...] = jnp.zeros_like(cnt_ref)
        x1_prev[...] = jnp.zeros_like(x1_prev)

    h = _rms(x1_prev[...], nw_ref[...])
    for c, words in enumerate(_pack_row_tiles(h)):
        h3_ref[pl.ds(c, tm, stride=ROW_TILES), :] = words
    logits = jnp.dot(h.astype(BF16), wr_ref[...], preferred_element_type=F32) + br_ref[...]

    x = jnp.where(jnp.minimum(i, blocks_t - 1) < blocks_p, xp_ref[...], xs_ref[...])
    x1 = x + jnp.dot(m_ref[...], wo_ref[...], preferred_element_type=F32)
    x1_ref[...] = x1
    x1_prev[...] = x1

    lane = lax.broadcasted_iota(jnp.int32, (tm, LANES), 1).astype(F32)
    neg = jnp.float32(-jnp.inf)

    def first_max(vals):
        m = jnp.max(vals, axis=-1, keepdims=True)
        idx = jnp.min(jnp.where(vals == m, lane, float(LANES)), axis=-1, keepdims=True)
        return m, idx

    lg = jnp.where((lane >= N_EXPERTS) & (lane < N_EXPERTS + N_GROUPS), logits, neg)
    gmax, gidx = first_max(lg)
    p_sel = 1.0 / jnp.sum(jnp.exp(lg - gmax), axis=-1, keepdims=True)
    lo = (gidx - N_EXPERTS) * EXPERTS_PER_GROUP
    le = jnp.where((lane >= lo) & (lane < lo + EXPERTS_PER_GROUP), logits, neg)
    top1, i1 = first_max(le)
    top2, i2 = first_max(jnp.where(lane == i1, neg, le))
    e2 = jnp.exp(top2 - top1)
    w1 = p_sel / (1.0 + e2)
    w2 = p_sel * e2 / (1.0 + e2)

    hit1 = lane == i1
    hit2 = lane == i2
    onehot = jnp.where((hit1 | hit2) & (i > 0), 1.0, 0.0).astype(BF16)
    r = lax.broadcasted_iota(jnp.int32, (tm, tm), 0)
    c = lax.broadcasted_iota(jnp.int32, (tm, tm), 1)
    lower = jnp.where(c < r, 1.0, 0.0).astype(BF16)
    before = jnp.dot(lower, onehot, preferred_element_type=F32) + cnt_ref[...]
    rank1 = jnp.sum(jnp.where(hit1, before, 0.0), axis=-1, keepdims=True)
    rank2 = jnp.sum(jnp.where(hit2, before, 0.0), axis=-1, keepdims=True)
    cnt_ref[...] += jnp.sum(onehot.astype(F32), axis=0, keepdims=True)

    route = jnp.where(lane == 0, i1, 0.0)
    route = jnp.where(lane == 1, i2, route)
    route = jnp.where(lane == 2, rank1, route)
    route = jnp.where(lane == 3, rank2, route)
    route = jnp.where(lane == 4, w1, route)
    route = jnp.where(lane == 5, w2, route)
    route_ref[...] = route
    rt_ref[...] = jnp.transpose(route)[0:SUBLANES, :]


def _post(xp, xs, mixed, w_out, norm_w, wr, br, tm):
    t = mixed.shape[0]
    blocks_p = xp.shape[0] // tm
    blocks_t = t // tm
    last = blocks_t - 1

    def cur(i):
        return jnp.minimum(i, last)

    def prev(i):
        return jnp.maximum(i - 1, 0)

    return pl.pallas_call(
        functools.partial(_post_kernel, tm=tm, blocks_p=blocks_p, blocks_t=blocks_t),
        grid=(blocks_t + 1,),
        in_specs=[
            pl.BlockSpec((tm, D_MODEL), lambda i: (jnp.minimum(i, blocks_p - 1), 0)),
            pl.BlockSpec((tm, D_MODEL), lambda i: (jnp.clip(i - blocks_p, 0, last - blocks_p), 0)),
            pl.BlockSpec((tm, D_MODEL), lambda i: (cur(i), 0)),
            pl.BlockSpec((D_MODEL, D_MODEL), lambda i: (0, 0), pipeline_mode=pl.Buffered(1)),
            pl.BlockSpec((1, D_MODEL), lambda i: (0, 0)),
            pl.BlockSpec((D_MODEL, LANES), lambda i: (0, 0)),
            pl.BlockSpec((1, LANES), lambda i: (0, 0)),
        ],
        out_specs=[
            pl.BlockSpec((tm, D_MODEL), lambda i: (cur(i), 0)),
            pl.BlockSpec((tm * ROW_TILES, LANES), lambda i: (prev(i), 0)),
            pl.BlockSpec((tm, LANES), lambda i: (prev(i), 0)),
            pl.BlockSpec((SUBLANES, tm), lambda i: (0, prev(i))),
            pl.BlockSpec((1, LANES), lambda i: (0, 0)),
        ],
        scratch_shapes=[pltpu.VMEM((tm, D_MODEL), F32)],
        out_shape=[
            jax.ShapeDtypeStruct((t, D_MODEL), F32),
            jax.ShapeDtypeStruct((t * ROW_TILES, LANES), U32),
            jax.ShapeDtypeStruct((t, LANES), F32),
            jax.ShapeDtypeStruct((SUBLANES, t), F32),
            jax.ShapeDtypeStruct((1, LANES), F32),
        ],
        compiler_params=pltpu.CompilerParams(
            dimension_semantics=("arbitrary",), vmem_limit_bytes=VMEM_LIMIT),
        name="post",
    )(xp, xs, mixed, w_out, norm_w, wr, br)


def _plan_kernel(cnt_ref, rt_ref, dest_ref, be_ref, nu_ref, zst_ref, zcn_ref, ps_ref, *, n_blocks):
    def per_expert(e, first):
        n = (cnt_ref[e] + (EXPERT_BLOCK - 1)) // EXPERT_BLOCK
        ps_ref[e] = first * EXPERT_BLOCK
        zst_ref[e] = first * EXPERT_BLOCK + cnt_ref[e]
        zcn_ref[e] = n * EXPERT_BLOCK - cnt_ref[e]

        def fill(j, carry):
            be_ref[first + j] = e
            return carry
        lax.fori_loop(0, n, fill, 0)
        return first + n
    n_used = lax.fori_loop(0, N_EXPERTS, per_expert, 0)
    nu_ref[0] = n_used

    def fill_tail(j, carry):
        be_ref[j] = N_EXPERTS - 1
        return carry
    lax.fori_loop(n_used, n_blocks, fill_tail, 0)

    for k in range(TOP_K):
        e = rt_ref[k:k + 1, :]
        base = jnp.zeros_like(e)
        for x in range(N_EXPERTS):
            base = jnp.where(e == float(x), ps_ref[x].astype(F32), base)
        dest_ref[k:k + 1, :] = (base + rt_ref[TOP_K + k:TOP_K + k + 1, :]).astype(I32)


def _plan(counts, route_t, n_blocks):
    t = route_t.shape[1]
    smem = pl.BlockSpec(memory_space=pltpu.SMEM)
    return pl.pallas_call(
        functools.partial(_plan_kernel, n_blocks=n_blocks),
        grid=(1,),
        in_specs=[smem, pl.BlockSpec((SUBLANES, t), lambda i: (0, 0))],
        out_specs=[pl.BlockSpec((TOP_K, t), lambda i: (0, 0)), smem, smem, smem, smem],
        out_shape=[
            jax.ShapeDtypeStruct((TOP_K, t), I32),
            jax.ShapeDtypeStruct((n_blocks,), I32),
            jax.ShapeDtypeStruct((1,), I32),
            jax.ShapeDtypeStruct((N_EXPERTS,), I32),
            jax.ShapeDtypeStruct((N_EXPERTS,), I32),
        ],
        scratch_shapes=[pltpu.SMEM((N_EXPERTS,), I32)],
        compiler_params=pltpu.CompilerParams(dimension_semantics=("arbitrary",)),
        name="plan",
    )(counts, route_t)


def _invert_kernel(dest_ref, zst_ref, zcn_ref, tok_ref, row_ref, *, n_tok):
    n_assign = TOP_K * n_tok

    def lead(j, carry):
        row_ref[j] = n_assign + 2 * EXPERT_BLOCK + j
        return carry
    lax.fori_loop(0, EXPERT_BLOCK, lead, 0)

    def pad_expert(e, carry):
        start = zst_ref[e]

        def one(j, c):
            p = start + j
            tok_ref[p] = 0
            row_ref[EXPERT_BLOCK + p] = n_assign + (p & (2 * EXPERT_BLOCK - 1))
            return c
        lax.fori_loop(0, zcn_ref[e], one, 0)
        return carry
    lax.fori_loop(0, N_EXPERTS, pad_expert, 0)

    def fill(i, carry):
        for u in range(DMA_UNROLL):
            tok = i * DMA_UNROLL + u
            for k in range(TOP_K):
                p = dest_ref[k * n_tok + tok]
                tok_ref[p] = tok
                row_ref[EXPERT_BLOCK + p] = k * n_tok + tok
        return carry
    lax.fori_loop(0, n_tok // DMA_UNROLL, fill, 0)


def _invert(dest_flat, pad_start, pad_count, n_tok, n_slots):
    assert EXPERT_BLOCK & (EXPERT_BLOCK - 1) == 0
    smem = pl.BlockSpec(memory_space=pltpu.SMEM)
    return pl.pallas_call(
        functools.partial(_invert_kernel, n_tok=n_tok),
        grid_spec=pltpu.PrefetchScalarGridSpec(
            num_scalar_prefetch=3,
            grid=(1,),
            in_specs=[],
            out_specs=[smem, smem],
        ),
        out_shape=[jax.ShapeDtypeStruct((n_slots,), I32),
                   jax.ShapeDtypeStruct((n_slots + EXPERT_BLOCK,), I32)],
        compiler_params=pltpu.CompilerParams(dimension_semantics=("arbitrary",)),
        name="invert",
    )(dest_flat, pad_start, pad_count)


def _experts_kernel(be_ref, nu_ref, tok_ref, row_ref, h3_ref, wg_ref, wu_ref, wd_ref, y_ref,
                    xb0, xb1, yb0, yb1, wgb, wub, wdb, gsem, ssem):
    b = pl.program_id(0)
    n_used = nu_ref[0]
    blk = EXPERT_BLOCK

    def slab(buf, r):
        start = r * ROW_TILES
        if not isinstance(r, int):
            start = pl.multiple_of(start, ROW_TILES)
        return buf.at[pl.ds(start, ROW_TILES)]

    def gather_copy(tok, buf, r, par):
        return pltpu.make_async_copy(h3_ref.at[tok], slab(buf, r), gsem.at[par])

    def scatter_copy(dst, buf, r, par):
        return pltpu.make_async_copy(slab(buf, r), y_ref.at[dst], ssem.at[par])

    def rows(fn):
        def body(i, carry):
            for u in range(DMA_UNROLL):
                fn(i * DMA_UNROLL + u)
            return carry
        lax.fori_loop(0, blk // DMA_UNROLL, body, 0)

    def gather_wait(buf, par):
        rows(lambda r: gather_copy(0, buf, r, par).wait())

    def scatter_wait(buf, par):
        rows(lambda r: scatter_copy(0, buf, r, par).wait())

    def step(par, x_cur, x_nxt, y_cur, y_oth):
        @pl.when(b == 0)
        def _():
            rows(lambda r: gather_copy(tok_ref[r], x_cur, r, par).start())
            y_oth[...] = jnp.zeros_like(y_oth)

        @pl.when((b == 0) | (be_ref[b] != be_ref[jnp.maximum(b - 1, 0)]))
        def _():
            wgb[...] = wg_ref[0].astype(BF16)
            wub[...] = wu_ref[0].astype(BF16)
            wdb[...] = wd_ref[0].astype(BF16)

        gather_wait(x_cur, par)

        @pl.when(b >= 1)
        def _():
            scatter_wait(y_cur, par)

        x = jnp.concatenate(_unpack_row_tiles(
            [x_cur[pl.ds(c, blk, stride=ROW_TILES), :] for c in range(ROW_TILES)]), axis=1
        ).astype(BF16)
        nxt = jnp.minimum(b + 1, n_used - 1) * blk
        for r in range(blk):
            gather_copy(tok_ref[nxt + r], x_nxt, r, 1 - par).start(priority=r % 2)
            scatter_copy(row_ref[b * blk + r], y_oth, r, 1 - par).start(priority=(r + 1) % 2)
        gate = jnp.dot(x, wgb[...], preferred_element_type=F32)
        up = jnp.dot(x, wub[...], preferred_element_type=F32)
        hid = (jax.nn.silu(gate) * up).astype(BF16)
        y = jnp.dot(hid, wdb[...], preferred_element_type=F32)
        for c, words in enumerate(_pack_row_tiles(y)):
            y_cur[pl.ds(c, blk, stride=ROW_TILES), :] = words

    def drain(par, x_cur, y_cur, y_oth):
        gather_wait(x_cur, par)
        rows(lambda r: scatter_copy(row_ref[b * blk + r], y_oth, r, 1 - par).start())
        scatter_wait(y_cur, par)
        scatter_wait(y_oth, 1 - par)

    for par, bufs in ((0, (xb0, xb1, yb0, yb1)), (1, (xb1, xb0, yb1, yb0))):
        @pl.when((b < n_used) & (b % 2 == par))
        def _(par=par, bufs=bufs):
            step(par, *bufs)

        @pl.when((b == n_used) & (b % 2 == par))
        def _(par=par, bufs=bufs):
            drain(par, bufs[0], bufs[2], bufs[3])


def _experts(h3, block_expert, n_used, slot_tok, slot_row, wg, wu, wd):
    n_blocks = block_expert.shape[0]
    n_tok = h3.shape[0]

    def wmap(b, be, nu, st, sr):
        return (be[jnp.minimum(b, nu[0] - 1)], 0, 0)

    buf = pltpu.VMEM((EXPERT_BLOCK * ROW_TILES, LANES), U32)
    return pl.pallas_call(
        _experts_kernel,
        grid_spec=pltpu.PrefetchScalarGridSpec(
            num_scalar_prefetch=4,
            grid=(n_blocks,),
            in_specs=[
                pl.BlockSpec(memory_space=pl.ANY),
                pl.BlockSpec((1, D_MODEL, D_EXPERT), wmap),
                pl.BlockSpec((1, D_MODEL, D_EXPERT), wmap),
                pl.BlockSpec((1, D_EXPERT, D_MODEL), wmap),
            ],
            out_specs=pl.BlockSpec(memory_space=pl.ANY),
            scratch_shapes=[
                buf, buf, buf, buf,
                pltpu.VMEM((D_MODEL, D_EXPERT), BF16),
                pltpu.VMEM((D_MODEL, D_EXPERT), BF16),
                pltpu.VMEM((D_EXPERT, D_MODEL), BF16),
                pltpu.SemaphoreType.DMA((2,)),
                pltpu.SemaphoreType.DMA((2,)),
            ],
        ),
        out_shape=jax.ShapeDtypeStruct((TOP_K * n_tok + 3 * EXPERT_BLOCK, ROW_TILES, LANES), U32),
        compiler_params=pltpu.CompilerParams(
            dimension_semantics=("arbitrary",), vmem_limit_bytes=VMEM_LIMIT),
        name="experts",
    )(block_expert, n_used, slot_tok, slot_row, h3, wg, wu, wd)


def _final_kernel(x1_ref, route_ref, nw_ref, y0_ref, y1_ref, op_ref, os_ref, *, blocks_p):
    i = pl.program_id(0)

    tm = x1_ref.shape[0]

    def rows(ref):
        return jnp.concatenate(_unpack_row_tiles(
            [ref[pl.ds(c, tm, stride=ROW_TILES), :] for c in range(ROW_TILES)]), axis=1)

    w1 = route_ref[:, 4:5]
    w2 = route_ref[:, 5:6]
    x2 = x1_ref[...] + (rows(y0_ref) * w1 + rows(y1_ref) * w2)
    out = _rms(x2, nw_ref[...])

    @pl.when(i < blocks_p)
    def _():
        op_ref[...] = out

    @pl.when(i >= blocks_p)
    def _():
        os_ref[...] = out


def _final(x1, route, norm_w, y, tp, tm):
    t = x1.shape[0]
    blocks_p = tp // tm
    blocks_t = t // tm
    return pl.pallas_call(
        functools.partial(_final_kernel, blocks_p=blocks_p),
        grid=(blocks_t,),
        in_specs=[
            pl.BlockSpec((tm, D_MODEL), lambda i: (i, 0)),
            pl.BlockSpec((tm, LANES), lambda i: (i, 0)),
            pl.BlockSpec((1, D_MODEL), lambda i: (0, 0)),
            pl.BlockSpec((tm * ROW_TILES, LANES), lambda i: (i, 0)),
            pl.BlockSpec((tm * ROW_TILES, LANES), lambda i: (i + blocks_t, 0)),
        ],
        out_specs=[
            pl.BlockSpec((tm, D_MODEL), lambda i: (jnp.minimum(i, blocks_p - 1), 0)),
            pl.BlockSpec((tm, D_MODEL), lambda i: (jnp.maximum(i - blocks_p, 0), 0)),
        ],
        out_shape=[
            jax.ShapeDtypeStruct((tp, D_MODEL), F32),
            jax.ShapeDtypeStruct((t - tp, D_MODEL), F32),
        ],
        compiler_params=pltpu.CompilerParams(dimension_semantics=("arbitrary",)),
        name="final",
    )(x1, route, norm_w, y, y)


def _rope_tables(seq_len):
    half = HEAD_DIM // 2
    inv = ROPE_BASE ** (-jnp.arange(half, dtype=F32) / half)
    ang = jnp.arange(seq_len, dtype=F32)[:, None] * inv[None, :]
    cos = jnp.cos(ang)
    sin = jnp.sin(ang)
    return jnp.concatenate([cos, cos], axis=1), jnp.concatenate([-sin, sin], axis=1)


def _tile(limit, *sizes):
    return min(limit, math.gcd(*sizes))


def kernel(x_prompt, x_sample, norm_mix, w_in, gmlp_norm_v, gmlp_w_spatial, gmlp_b_spatial, ret_decay_fwd, ret_decay_bwd, ret_norm, w_out, norm_ffn, w_router_group, b_router_group, w_router_expert, b_router_expert, w_expert_gate, w_expert_up, w_expert_down, norm_final):
    assert norm_mix.shape[0] == 1, "single-layer block"
    bp, sp, d = x_prompt.shape
    bs, ss, _ = x_sample.shape
    assert d == D_MODEL and sp % CHUNK == 0 and ss % CHUNK == 0
    tp, ts = bp * sp, bs * ss
    t = tp + ts
    xp = x_prompt.reshape(tp, d)
    xs = x_sample.reshape(ts, d)

    tm = _tile(1024, sp, ss)
    cos_t, sin_t = _rope_tables(max(sp, ss))
    w_in_b = w_in[0].astype(BF16)
    gv_w = gmlp_norm_v.reshape(1, SECTION)
    proj = _proj(xp, norm_mix, w_in_b, cos_t, sin_t, gv_w, None, t, 0, tm, sp // tm)
    proj = _proj(xs, norm_mix, w_in_b, cos_t, sin_t, gv_w, proj, t, tp // tm, tm, ss // tm)

    lgf = jax.nn.log_sigmoid(ret_decay_fwd[0].astype(F32))
    lgb = jax.nn.log_sigmoid(ret_decay_bwd[0].astype(F32))
    group = _tile(4, sp // CHUNK, ss // CHUNK)
    sf, sb = _states(proj, lgf, lgb, group, tp // CHUNK, sp // CHUNK, ss // CHUNK)
    mixed = _mix(proj, sf, sb, lgf, lgb, gmlp_w_spatial[0].astype(BF16),
                 gmlp_b_spatial[0][:, :, None], ret_norm.reshape(1, SECTION), group)

    pad = LANES - N_EXPERTS - N_GROUPS
    wr = jnp.concatenate([w_router_expert[0], w_router_group[0], jnp.zeros((d, pad), F32)], axis=1)
    br = jnp.concatenate([b_router_expert[0], b_router_group[0], jnp.zeros((pad,), F32)])[None, :]
    x1, h3, route, route_t, cnt = _post(xp, xs, mixed, w_out[0].astype(BF16), norm_ffn,
                                        wr.astype(BF16), br, _tile(256, tp, ts))
    tm = _tile(256, tp, ts)

    n_blocks = -(-(TOP_K * t) // EXPERT_BLOCK) + N_EXPERTS
    dest, block_expert, n_used, pad_start, pad_count = _plan(
        cnt[0, :N_EXPERTS].astype(I32), route_t, n_blocks)
    slot_tok, slot_row = _invert(dest.reshape(TOP_K * t), pad_start, pad_count, t,
                                 n_blocks * EXPERT_BLOCK)
    y = _experts(h3.reshape(t, ROW_TILES, LANES), block_expert, n_used, slot_tok, slot_row,
                 w_expert_gate[0], w_expert_up[0], w_expert_down[0])

    out_p, out_s = _final(x1, route, norm_final[None, :], y.reshape(-1, LANES), tp, tm)
    return out_p.reshape(bp, sp, d), out_s.reshape(bs, ss, d)
```

```python
import functools
import math

import jax
import jax.numpy as jnp
from jax import lax
from jax.experimental import pallas as pl
from jax.experimental.pallas import tpu as pltpu

F32 = jnp.float32
BF16 = jnp.bfloat16
I32 = jnp.int32
U32 = jnp.uint32

D_MODEL = 2048
CHUNK = 128
HEADS = 8
HEAD_DIM = 128
SECTION = HEADS * HEAD_DIM
N_SECTIONS = 6
PROJ_WIDTH = N_SECTIONS * SECTION
ROPE_BASE = 10000.0
N_GROUPS = 4
EXPERTS_PER_GROUP = 8
N_EXPERTS = N_GROUPS * EXPERTS_PER_GROUP
TOP_K = 2
D_EXPERT = D_MODEL // 4
EPS = 1e-6
LANES = 128
SUBLANES = 8
MXU_COLS = 256
ROW_TILES = D_MODEL // LANES // 2
EXPERT_BLOCK = 256
DMA_UNROLL = 8
VMEM_LIMIT = 56 * 1024 * 1024


def _rms(x, w):
    ms = jnp.mean(x * x, axis=-1, keepdims=True)
    return x * lax.rsqrt(ms + EPS) * w


def _pack_row_tiles(x):
    def bits(c):
        return lax.bitcast_convert_type(x[:, c * LANES:(c + 1) * LANES].astype(BF16).astype(F32), U32)
    return [(bits(2 * c) >> 16) | bits(2 * c + 1) for c in range(ROW_TILES)]


def _unpack_row_tiles(words):
    tiles = []
    for w in words:
        tiles.append(lax.bitcast_convert_type(w << 16, F32))
        tiles.append(lax.bitcast_convert_type(w & jnp.uint32(0xFFFF0000), F32))
    return tiles


def _proj_kernel(x_ref, nw_ref, w_ref, cos_ref, sin_ref, gvw_ref, prev_ref, o_ref, h_ref):
    del prev_ref
    j = pl.program_id(1)

    @pl.when(j == 0)
    def _():
        h_ref[...] = _rms(x_ref[...], nw_ref[...]).astype(BF16)

    def rotary(seg):
        return seg * cos_ref[...] + pltpu.roll(seg, HEAD_DIM // 2, axis=1) * sin_ref[...]

    def section(epilogue):
        for c in range(SECTION // MXU_COLS):
            acc = jnp.dot(h_ref[...], w_ref[:, c * MXU_COLS:(c + 1) * MXU_COLS],
                          preferred_element_type=F32)
            for g in range(MXU_COLS // HEAD_DIM):
                sl = slice(c * MXU_COLS + g * HEAD_DIM, c * MXU_COLS + (g + 1) * HEAD_DIM)
                o_ref[:, sl] = epilogue(acc[:, g * HEAD_DIM:(g + 1) * HEAD_DIM], sl).astype(BF16)

    @pl.when(j == 0)
    def _():
        section(lambda seg, sl: jax.nn.gelu(seg))

    @pl.when(j == 1)
    def _():
        section(lambda seg, sl: _rms(jax.nn.gelu(seg), gvw_ref[:, sl]))

    @pl.when(j == 2)
    def _():
        section(lambda seg, sl: rotary(seg))

    @pl.when(j == 3)
    def _():
        section(lambda seg, sl: rotary(seg) * (HEAD_DIM ** -0.5))

    @pl.when(j == 4)
    def _():
        section(lambda seg, sl: seg)

    @pl.when(j == 5)
    def _():
        section(lambda seg, sl: jax.nn.silu(seg))


def _proj(x, norm_w, w_in, cos_t, sin_t, gv_w, prev, t_total, row_block0, tm, per_seq):
    if prev is None:
        prev = jnp.zeros((SUBLANES, LANES), BF16)
        aliases = {}
    else:
        aliases = {6: 0}
    return pl.pallas_call(
        _proj_kernel,
        grid=(x.shape[0] // tm, N_SECTIONS),
        in_specs=[
            pl.BlockSpec((tm, D_MODEL), lambda i, j: (i, 0)),
            pl.BlockSpec((1, D_MODEL), lambda i, j: (0, 0)),
            pl.BlockSpec((D_MODEL, SECTION), lambda i, j: (0, j)),
            pl.BlockSpec((tm, HEAD_DIM), lambda i, j: (i % per_seq, 0)),
            pl.BlockSpec((tm, HEAD_DIM), lambda i, j: (i % per_seq, 0)),
            pl.BlockSpec((1, SECTION), lambda i, j: (0, 0)),
            pl.BlockSpec(memory_space=pl.ANY),
        ],
        out_specs=pl.BlockSpec((None, tm, SECTION), lambda i, j: (j, i + row_block0, 0)),
        out_shape=jax.ShapeDtypeStruct((N_SECTIONS, t_total, SECTION), BF16),
        scratch_shapes=[pltpu.VMEM((tm, D_MODEL), BF16)],
        input_output_aliases=aliases,
        compiler_params=pltpu.CompilerParams(
            dimension_semantics=("arbitrary", "arbitrary"), vmem_limit_bytes=VMEM_LIMIT),
        name="proj",
    )(x, norm_w, w_in, cos_t, sin_t, gv_w, prev)


def _seq_edge(c, chunks_p, n_p, n_s, last):
    pos = jnp.where(c < chunks_p, c % n_p, (c - chunks_p) % n_s)
    edge = jnp.where(c < chunks_p, n_p - 1, n_s - 1) if last else 0
    return pos == edge


def _states_kernel(lgf_ref, lgb_ref, kf_ref, vf_ref, kb_ref, vb_ref, sf_ref, sb_ref, stf, stb,
                   *, group, n_chunks, chunks_p, n_p, n_s):
    s = pl.program_id(0)
    row = lax.broadcasted_iota(jnp.int32, (CHUNK, HEAD_DIM), 0).astype(F32)
    tn = (((0,), (0,)), ((), ()))

    def one_chunk(g, carry):
        cf = s * group + g
        gb = group - 1 - g

        @pl.when(_seq_edge(cf, chunks_p, n_p, n_s, last=False))
        def _():
            stf[...] = jnp.zeros_like(stf)

        @pl.when(_seq_edge(n_chunks - 1 - cf, chunks_p, n_p, n_s, last=True))
        def _():
            stb[...] = jnp.zeros_like(stb)

        rf = pl.ds(pl.multiple_of(g * CHUNK, CHUNK), CHUNK)
        rb = pl.ds(pl.multiple_of(gb * CHUNK, CHUNK), CHUNK)
        for h in range(HEADS):
            sl = slice(h * HEAD_DIM, (h + 1) * HEAD_DIM)
            lgf = lgf_ref[h]
            lgb = lgb_ref[h]
            kd = (kf_ref[rf, sl].astype(F32) * jnp.exp((CHUNK - 1.0 - row) * lgf)).astype(BF16)
            kv = lax.dot_general(kd, vf_ref[rf, sl], tn, preferred_element_type=F32)
            sf_ref[g, h] = stf[h].astype(BF16)
            stf[h] = stf[h] * jnp.exp(CHUNK * lgf) + kv
            kd = (kb_ref[rb, sl].astype(F32) * jnp.exp(row * lgb)).astype(BF16)
            kv = lax.dot_general(kd, vb_ref[rb, sl], tn, preferred_element_type=F32)
            sb_ref[gb, h] = stb[h].astype(BF16)
            stb[h] = stb[h] * jnp.exp(CHUNK * lgb) + kv
        return carry
    lax.fori_loop(0, group, one_chunk, 0)


def _states(proj, lgf, lgb, group, chunks_p, n_p, n_s):
    n_chunks = proj.shape[1] // CHUNK
    last = n_chunks // group - 1
    rows = group * CHUNK
    smem = pl.BlockSpec(memory_space=pltpu.SMEM)
    st_shape = jax.ShapeDtypeStruct((n_chunks, HEADS, HEAD_DIM, HEAD_DIM), BF16)
    return pl.pallas_call(
        functools.partial(_states_kernel, group=group, n_chunks=n_chunks, chunks_p=chunks_p,
                          n_p=n_p, n_s=n_s),
        grid=(n_chunks // group,),
        in_specs=[
            smem, smem,
            pl.BlockSpec((None, rows, SECTION), lambda s: (3, s, 0)),
            pl.BlockSpec((None, rows, SECTION), lambda s: (4, s, 0)),
            pl.BlockSpec((None, rows, SECTION), lambda s: (3, last - s, 0)),
            pl.BlockSpec((None, rows, SECTION), lambda s: (4, last - s, 0)),
        ],
        out_specs=[
            pl.BlockSpec((group, HEADS, HEAD_DIM, HEAD_DIM), lambda s: (s, 0, 0, 0)),
            pl.BlockSpec((group, HEADS, HEAD_DIM, HEAD_DIM), lambda s: (last - s, 0, 0, 0)),
        ],
        out_shape=[st_shape, st_shape],
        scratch_shapes=[pltpu.VMEM((HEADS, HEAD_DIM, HEAD_DIM), F32),
                        pltpu.VMEM((HEADS, HEAD_DIM, HEAD_DIM), F32)],
        compiler_params=pltpu.CompilerParams(dimension_semantics=("arbitrary",)),
        name="states",
    )(lgf, lgb, proj, proj, proj, proj)


def _mix_kernel(lgf_ref, lgb_ref, u_ref, gv_ref, q_ref, k_ref, v_ref, g_ref, sf_ref, sb_ref,
                ws_ref, bs_ref, rn_ref, o_ref, dec_ref, qdf_ref, qdb_ref):
    @pl.when(pl.program_id(0) == 0)
    def _():
        row = lax.broadcasted_iota(jnp.int32, (CHUNK, CHUNK), 0).astype(F32)
        col = lax.broadcasted_iota(jnp.int32, (CHUNK, CHUNK), 1).astype(F32)
        diff = row - col
        for h in range(HEADS):
            lgf = lgf_ref[h]
            lgb = lgb_ref[h]
            dec_ref[h] = jnp.where(diff >= 0, jnp.exp(jnp.maximum(diff, 0.0) * lgf),
                                   jnp.exp(jnp.maximum(-diff, 0.0) * lgb))
            qdf_ref[h] = jnp.exp((row + 1.0) * lgf)
            qdb_ref[h] = jnp.exp((CHUNK - row) * lgb)

    nt = (((1,), (1,)), ((), ()))

    def one_chunk(g, carry):
        rows = pl.ds(pl.multiple_of(g * CHUNK, CHUNK), CHUNK)
        for h in range(HEADS):
            sl = slice(h * HEAD_DIM, (h + 1) * HEAD_DIM)
            mixed = jnp.dot(ws_ref[h], gv_ref[rows, sl], preferred_element_type=F32) + bs_ref[h]
            o_ref[rows, sl] = (u_ref[rows, sl].astype(F32) * mixed).astype(BF16)
            q = q_ref[rows, sl]
            v = v_ref[rows, sl]
            scores = lax.dot_general(q, k_ref[rows, sl], nt, preferred_element_type=F32)
            ret = jnp.dot((scores * dec_ref[h]).astype(BF16), v, preferred_element_type=F32)
            ret += jnp.dot(q, sf_ref[g, h], preferred_element_type=F32) * qdf_ref[h]
            ret += jnp.dot(q, sb_ref[g, h], preferred_element_type=F32) * qdb_ref[h]
            out = _rms(ret, rn_ref[:, sl]) * g_ref[rows, sl].astype(F32)
            o_ref[rows, SECTION + h * HEAD_DIM:SECTION + (h + 1) * HEAD_DIM] = out.astype(BF16)
        return carry
    lax.fori_loop(0, u_ref.shape[0] // CHUNK, one_chunk, 0)


def _mix(proj, sf, sb, lgf, lgb, ws, bs, rn, group):
    t = proj.shape[1]
    rows = group * CHUNK
    smem = pl.BlockSpec(memory_space=pltpu.SMEM)

    def sec(j):
        return pl.BlockSpec((None, rows, SECTION), lambda c, j=j: (j, c, 0))

    st_spec = pl.BlockSpec((group, HEADS, HEAD_DIM, HEAD_DIM), lambda c: (c, 0, 0, 0))
    tab = pltpu.VMEM((HEADS, CHUNK, CHUNK), F32)
    return pl.pallas_call(
        _mix_kernel,
        grid=(t // rows,),
        in_specs=[
            smem, smem, sec(0), sec(1), sec(2), sec(3), sec(4), sec(5), st_spec, st_spec,
            pl.BlockSpec((HEADS, CHUNK, CHUNK), lambda c: (0, 0, 0)),
            pl.BlockSpec((HEADS, CHUNK, 1), lambda c: (0, 0, 0)),
            pl.BlockSpec((1, SECTION), lambda c: (0, 0)),
        ],
        out_specs=pl.BlockSpec((rows, 2 * SECTION), lambda c: (c, 0)),
        out_shape=jax.ShapeDtypeStruct((t, 2 * SECTION), BF16),
        scratch_shapes=[tab, tab, tab],
        compiler_params=pltpu.CompilerParams(dimension_semantics=("arbitrary",)),
        name="mix",
    )(lgf, lgb, proj, proj, proj, proj, proj, proj, sf, sb, ws, bs, rn)


def _post_kernel(xp_ref, xs_ref, m_ref, wo_ref, nw_ref, wr_ref, br_ref,
                 x1_ref, h3_ref, route_ref, rt_ref, cnt_ref, x1_prev, *, tm, blocks_p, blocks_t):
    i = pl.program_id(0)

    @pl.when(i == 0)
    def _():
        cnt_ref[...] = jnp.zeros_like(cnt_ref)
        x1_prev[...] = jnp.zeros_like(x1_prev)

    h = _rms(x1_prev[...], nw_ref[...])
    for c, words in enumerate(_pack_row_tiles(h)):
        h3_ref[pl.ds(c, tm, stride=ROW_TILES), :] = words
    logits = jnp.dot(h.astype(BF16), wr_ref[...], preferred_element_type=F32) + br_ref[...]

    x = jnp.where(jnp.minimum(i, blocks_t - 1) < blocks_p, xp_ref[...], xs_ref[...])
    x1 = x + jnp.dot(m_ref[...], wo_ref[...], preferred_element_type=F32)
    x1_ref[...] = x1
    x1_prev[...] = x1

    lane = lax.broadcasted_iota(jnp.int32, (tm, LANES), 1).astype(F32)
    neg = jnp.float32(-jnp.inf)

    def first_max(vals):
        m = jnp.max(vals, axis=-1, keepdims=True)
        idx = jnp.min(jnp.where(vals == m, lane, float(LANES)), axis=-1, keepdims=True)
        return m, idx

    lg = jnp.where((lane >= N_EXPERTS) & (lane < N_EXPERTS + N_GROUPS), logits, neg)
    gmax, gidx = first_max(lg)
    p_sel = 1.0 / jnp.sum(jnp.exp(lg - gmax), axis=-1, keepdims=True)
    lo = (gidx - N_EXPERTS) * EXPERTS_PER_GROUP
    le = jnp.where((lane >= lo) & (lane < lo + EXPERTS_PER_GROUP), logits, neg)
    top1, i1 = first_max(le)
    top2, i2 = first_max(jnp.where(lane == i1, neg, le))
    e2 = jnp.exp(top2 - top1)
    w1 = p_sel / (1.0 + e2)
    w2 = p_sel * e2 / (1.0 + e2)

    hit1 = lane == i1
    hit2 = lane == i2
    onehot = jnp.where((hit1 | hit2) & (i > 0), 1.0, 0.0).astype(BF16)
    r = lax.broadcasted_iota(jnp.int32, (tm, tm), 0)
    c = lax.broadcasted_iota(jnp.int32, (tm, tm), 1)
    lower = jnp.where(c < r, 1.0, 0.0).astype(BF16)
    before = jnp.dot(lower, onehot, preferred_element_type=F32) + cnt_ref[...]
    rank1 = jnp.sum(jnp.where(hit1, before, 0.0), axis=-1, keepdims=True)
    rank2 = jnp.sum(jnp.where(hit2, before, 0.0), axis=-1, keepdims=True)
    cnt_ref[...] += jnp.sum(onehot.astype(F32), axis=0, keepdims=True)

    route = jnp.where(lane == 0, i1, 0.0)
    route = jnp.where(lane == 1, i2, route)
    route = jnp.where(lane == 2, rank1, route)
    route = jnp.where(lane == 3, rank2, route)
    route = jnp.where(lane == 4, w1, route)
    route = jnp.where(lane == 5, w2, route)
    route_ref[...] = route
    rt_ref[...] = jnp.transpose(route)[0:SUBLANES, :]


def _post(xp, xs, mixed, w_out, norm_w, wr, br, tm):
    t = mixed.shape[0]
    blocks_p = xp.shape[0] // tm
    blocks_t = t // tm
    last = blocks_t - 1

    def cur(i):
        return jnp.minimum(i, last)

    def prev(i):
        return jnp.maximum(i - 1, 0)

    return pl.pallas_call(
        functools.partial(_post_kernel, tm=tm, blocks_p=blocks_p, blocks_t=blocks_t),
        grid=(blocks_t + 1,),
        in_specs=[
            pl.BlockSpec((tm, D_MODEL), lambda i: (jnp.minimum(i, blocks_p - 1), 0)),
            pl.BlockSpec((tm, D_MODEL), lambda i: (jnp.clip(i - blocks_p, 0, last - blocks_p), 0)),
            pl.BlockSpec((tm, D_MODEL), lambda i: (cur(i), 0)),
            pl.BlockSpec((D_MODEL, D_MODEL), lambda i: (0, 0), pipeline_mode=pl.Buffered(1)),
            pl.BlockSpec((1, D_MODEL), lambda i: (0, 0)),
            pl.BlockSpec((D_MODEL, LANES), lambda i: (0, 0)),
            pl.BlockSpec((1, LANES), lambda i: (0, 0)),
        ],
        out_specs=[
            pl.BlockSpec((tm, D_MODEL), lambda i: (cur(i), 0)),
            pl.BlockSpec((tm * ROW_TILES, LANES), lambda i: (prev(i), 0)),
            pl.BlockSpec((tm, LANES), lambda i: (prev(i), 0)),
            pl.BlockSpec((SUBLANES, tm), lambda i: (0, prev(i))),
            pl.BlockSpec((1, LANES), lambda i: (0, 0)),
        ],
        scratch_shapes=[pltpu.VMEM((tm, D_MODEL), F32)],
        out_shape=[
            jax.ShapeDtypeStruct((t, D_MODEL), F32),
            jax.ShapeDtypeStruct((t * ROW_TILES, LANES), U32),
            jax.ShapeDtypeStruct((t, LANES), F32),
            jax.ShapeDtypeStruct((SUBLANES, t), F32),
            jax.ShapeDtypeStruct((1, LANES), F32),
        ],
        compiler_params=pltpu.CompilerParams(
            dimension_semantics=("arbitrary",), vmem_limit_bytes=VMEM_LIMIT),
        name="post",
    )(xp, xs, mixed, w_out, norm_w, wr, br)


def _plan_kernel(cnt_ref, rt_ref, dest_ref, be_ref, nu_ref, zst_ref, zcn_ref, ps_ref, *, n_blocks):
    def per_expert(e, first):
        n = (cnt_ref[e] + (EXPERT_BLOCK - 1)) // EXPERT_BLOCK
        ps_ref[e] = first * EXPERT_BLOCK
        zst_ref[e] = first * EXPERT_BLOCK + cnt_ref[e]
        zcn_ref[e] = n * EXPERT_BLOCK - cnt_ref[e]

        def fill(j, carry):
            be_ref[first + j] = e
            return carry
        lax.fori_loop(0, n, fill, 0)
        return first + n
    n_used = lax.fori_loop(0, N_EXPERTS, per_expert, 0)
    nu_ref[0] = n_used

    def fill_tail(j, carry):
        be_ref[j] = N_EXPERTS - 1
        return carry
    lax.fori_loop(n_used, n_blocks, fill_tail, 0)

    for k in range(TOP_K):
        e = rt_ref[k:k + 1, :]
        base = jnp.zeros_like(e)
        for x in range(N_EXPERTS):
            base = jnp.where(e == float(x), ps_ref[x].astype(F32), base)
        dest_ref[k:k + 1, :] = (base + rt_ref[TOP_K + k:TOP_K + k + 1, :]).astype(I32)


def _plan(counts, route_t, n_blocks):
    t = route_t.shape[1]
    smem = pl.BlockSpec(memory_space=pltpu.SMEM)
    return pl.pallas_call(
        functools.partial(_plan_kernel, n_blocks=n_blocks),
        grid=(1,),
        in_specs=[smem, pl.BlockSpec((SUBLANES, t), lambda i: (0, 0))],
        out_specs=[pl.BlockSpec((TOP_K, t), lambda i: (0, 0)), smem, smem, smem, smem],
        out_shape=[
            jax.ShapeDtypeStruct((TOP_K, t), I32),
            jax.ShapeDtypeStruct((n_blocks,), I32),
            jax.ShapeDtypeStruct((1,), I32),
            jax.ShapeDtypeStruct((N_EXPERTS,), I32),
            jax.ShapeDtypeStruct((N_EXPERTS,), I32),
        ],
        scratch_shapes=[pltpu.SMEM((N_EXPERTS,), I32)],
        compiler_params=pltpu.CompilerParams(dimension_semantics=("arbitrary",)),
        name="plan",
    )(counts, route_t)


def _invert_kernel(dest_ref, zst_ref, zcn_ref, tok_ref, *, n_tok):
    def pad_expert(e, carry):
        start = zst_ref[e]

        def one(j, c):
            tok_ref[start + j] = 0
            return c
        lax.fori_loop(0, zcn_ref[e], one, 0)
        return carry
    lax.fori_loop(0, N_EXPERTS, pad_expert, 0)

    def fill(i, carry):
        for u in range(DMA_UNROLL):
            tok = i * DMA_UNROLL + u
            for k in range(TOP_K):
                tok_ref[dest_ref[k * n_tok + tok]] = tok
        return carry
    lax.fori_loop(0, n_tok // DMA_UNROLL, fill, 0)


def _invert(dest_flat, pad_start, pad_count, n_tok, n_slots):
    assert EXPERT_BLOCK & (EXPERT_BLOCK - 1) == 0
    smem = pl.BlockSpec(memory_space=pltpu.SMEM)
    return pl.pallas_call(
        functools.partial(_invert_kernel, n_tok=n_tok),
        grid_spec=pltpu.PrefetchScalarGridSpec(
            num_scalar_prefetch=3,
            grid=(1,),
            in_specs=[],
            out_specs=smem,
        ),
        out_shape=jax.ShapeDtypeStruct((n_slots,), I32),
        compiler_params=pltpu.CompilerParams(dimension_semantics=("arbitrary",)),
        name="invert",
    )(dest_flat, pad_start, pad_count)


def _row_slab(buf, r):
    start = r * ROW_TILES
    if not isinstance(r, int):
        start = pl.multiple_of(start, ROW_TILES)
    return buf.at[pl.ds(start, ROW_TILES)]


def _row_loop(n, fn):
    def body(i, carry):
        for u in range(DMA_UNROLL):
            fn(i * DMA_UNROLL + u)
        return carry
    lax.fori_loop(0, n // DMA_UNROLL, body, 0)


def _experts_kernel(be_ref, nu_ref, tok_ref, h3_ref, wg_ref, wu_ref, wd_ref, y_ref,
                    xb0, xb1, wgb, wub, wdb, gsem):
    b = pl.program_id(0)
    n_used = nu_ref[0]
    blk = EXPERT_BLOCK

    def gather_copy(tok, buf, r, par):
        return pltpu.make_async_copy(h3_ref.at[tok], _row_slab(buf, r), gsem.at[par])

    def gather_wait(buf, par):
        _row_loop(blk, lambda r: gather_copy(0, buf, r, par).wait())

    def step(par, x_cur, x_nxt):
        @pl.when(b == 0)
        def _():
            _row_loop(blk, lambda r: gather_copy(tok_ref[r], x_cur, r, par).start())

        @pl.when((b == 0) | (be_ref[b] != be_ref[jnp.maximum(b - 1, 0)]))
        def _():
            wgb[...] = wg_ref[0].astype(BF16)
            wub[...] = wu_ref[0].astype(BF16)
            wdb[...] = wd_ref[0].astype(BF16)

        gather_wait(x_cur, par)

        x = jnp.concatenate(_unpack_row_tiles(
            [x_cur[pl.ds(c, blk, stride=ROW_TILES), :] for c in range(ROW_TILES)]), axis=1
        ).astype(BF16)
        nxt = jnp.minimum(b + 1, n_used - 1) * blk
        for r in range(blk):
            gather_copy(tok_ref[nxt + r], x_nxt, r, 1 - par).start(priority=r % 2)
        gate = jnp.dot(x, wgb[...], preferred_element_type=F32)
        up = jnp.dot(x, wub[...], preferred_element_type=F32)
        hid = (jax.nn.silu(gate) * up).astype(BF16)
        y = jnp.dot(hid, wdb[...], preferred_element_type=F32)
        for c, words in enumerate(_pack_row_tiles(y)):
            y_ref[pl.ds(c, blk, stride=ROW_TILES), :] = words

        @pl.when(b == n_used - 1)
        def _():
            gather_wait(x_nxt, 1 - par)

    for par, bufs in ((0, (xb0, xb1)), (1, (xb1, xb0))):
        @pl.when((b < n_used) & (b % 2 == par))
        def _(par=par, bufs=bufs):
            step(par, *bufs)


def _experts(h3, block_expert, n_used, slot_tok, wg, wu, wd):
    n_blocks = block_expert.shape[0]

    def used(b, nu):
        return jnp.minimum(b, nu[0] - 1)

    def wmap(b, be, nu, st):
        return (be[used(b, nu)], 0, 0)

    rows = EXPERT_BLOCK * ROW_TILES
    buf = pltpu.VMEM((rows, LANES), U32)
    return pl.pallas_call(
        _experts_kernel,
        grid_spec=pltpu.PrefetchScalarGridSpec(
            num_scalar_prefetch=3,
            grid=(n_blocks,),
            in_specs=[
                pl.BlockSpec(memory_space=pl.ANY),
                pl.BlockSpec((1, D_MODEL, D_EXPERT), wmap),
                pl.BlockSpec((1, D_MODEL, D_EXPERT), wmap),
                pl.BlockSpec((1, D_EXPERT, D_MODEL), wmap),
            ],
            out_specs=pl.BlockSpec((rows, LANES), lambda b, be, nu, st: (used(b, nu), 0)),
            scratch_shapes=[
                buf, buf,
                pltpu.VMEM((D_MODEL, D_EXPERT), BF16),
                pltpu.VMEM((D_MODEL, D_EXPERT), BF16),
                pltpu.VMEM((D_EXPERT, D_MODEL), BF16),
                pltpu.SemaphoreType.DMA((2,)),
            ],
        ),
        out_shape=jax.ShapeDtypeStruct((n_blocks * rows, LANES), U32),
        compiler_params=pltpu.CompilerParams(
            dimension_semantics=("arbitrary",), vmem_limit_bytes=VMEM_LIMIT),
        name="experts",
    )(block_expert, n_used, slot_tok, h3, wg, wu, wd)


def _final_kernel(d_ref, x1_ref, route_ref, nw_ref, y_ref, op_ref, os_ref,
                  ya0, yb0, ya1, yb1, sem, *, blocks_p, blocks_t, n_tok):
    i = pl.program_id(0)
    tm = x1_ref.shape[0]

    def copy(slot, buf, r, par):
        return pltpu.make_async_copy(y_ref.at[slot], _row_slab(buf, r), sem.at[par])

    def gather_start(tile, bufs, par):
        def one(r):
            for k in range(TOP_K):
                copy(d_ref[k * n_tok + tile * tm + r], bufs[k], r, par).start()
        _row_loop(tm, one)

    def gather_wait(bufs, par):
        def one(r):
            for k in range(TOP_K):
                copy(0, bufs[k], r, par).wait()
        _row_loop(tm, one)

    def rows(buf):
        return jnp.concatenate(_unpack_row_tiles(
            [buf[pl.ds(c, tm, stride=ROW_TILES), :] for c in range(ROW_TILES)]), axis=1)

    def step(par, cur, nxt):
        @pl.when(i == 0)
        def _():
            gather_start(0, cur, par)

        @pl.when(i + 1 < blocks_t)
        def _():
            gather_start(i + 1, nxt, 1 - par)

        gather_wait(cur, par)
        w1 = route_ref[:, 4:5]
        w2 = route_ref[:, 5:6]
        x2 = x1_ref[...] + (rows(cur[0]) * w1 + rows(cur[1]) * w2)
        out = _rms(x2, nw_ref[...])

        @pl.when(i < blocks_p)
        def _():
            op_ref[...] = out

        @pl.when(i >= blocks_p)
        def _():
            os_ref[...] = out

    for par, bufs in ((0, ((ya0, yb0), (ya1, yb1))), (1, ((ya1, yb1), (ya0, yb0)))):
        @pl.when(i % 2 == par)
        def _(par=par, bufs=bufs):
            step(par, *bufs)


def _final(x1, route, norm_w, y, dest_flat, tp, tm):
    t = x1.shape[0]
    blocks_p = tp // tm
    blocks_t = t // tm
    buf = pltpu.VMEM((tm * ROW_TILES, LANES), U32)
    return pl.pallas_call(
        functools.partial(_final_kernel, blocks_p=blocks_p, blocks_t=blocks_t, n_tok=t),
        grid_spec=pltpu.PrefetchScalarGridSpec(
            num_scalar_prefetch=1,
            grid=(blocks_t,),
            in_specs=[
                pl.BlockSpec((tm, D_MODEL), lambda i, d: (i, 0)),
                pl.BlockSpec((tm, LANES), lambda i, d: (i, 0)),
                pl.BlockSpec((1, D_MODEL), lambda i, d: (0, 0)),
                pl.BlockSpec(memory_space=pl.ANY),
            ],
            out_specs=[
                pl.BlockSpec((tm, D_MODEL), lambda i, d: (jnp.minimum(i, blocks_p - 1), 0)),
                pl.BlockSpec((tm, D_MODEL), lambda i, d: (jnp.maximum(i - blocks_p, 0), 0)),
            ],
            scratch_shapes=[buf, buf, buf, buf, pltpu.SemaphoreType.DMA((2,))],
        ),
        out_shape=[
            jax.ShapeDtypeStruct((tp, D_MODEL), F32),
            jax.ShapeDtypeStruct((t - tp, D_MODEL), F32),
        ],
        compiler_params=pltpu.CompilerParams(dimension_semantics=("arbitrary",)),
        name="final",
    )(dest_flat, x1, route, norm_w, y)


def _rope_tables(seq_len):
    half = HEAD_DIM // 2
    inv = ROPE_BASE ** (-jnp.arange(half, dtype=F32) / half)
    ang = jnp.arange(seq_len, dtype=F32)[:, None] * inv[None, :]
    cos = jnp.cos(ang)
    sin = jnp.sin(ang)
    return jnp.concatenate([cos, cos], axis=1), jnp.concatenate([-sin, sin], axis=1)


def _tile(limit, *sizes):
    return min(limit, math.gcd(*sizes))


def kernel(x_prompt, x_sample, norm_mix, w_in, gmlp_norm_v, gmlp_w_spatial, gmlp_b_spatial, ret_decay_fwd, ret_decay_bwd, ret_norm, w_out, norm_ffn, w_router_group, b_router_group, w_router_expert, b_router_expert, w_expert_gate, w_expert_up, w_expert_down, norm_final):
    assert norm_mix.shape[0] == 1, "single-layer block"
    bp, sp, d = x_prompt.shape
    bs, ss, _ = x_sample.shape
    assert d == D_MODEL and sp % CHUNK == 0 and ss % CHUNK == 0
    tp, ts = bp * sp, bs * ss
    t = tp + ts
    xp = x_prompt.reshape(tp, d)
    xs = x_sample.reshape(ts, d)

    tm = _tile(1024, sp, ss)
    cos_t, sin_t = _rope_tables(max(sp, ss))
    w_in_b = w_in[0].astype(BF16)
    gv_w = gmlp_norm_v.reshape(1, SECTION)
    proj = _proj(xp, norm_mix, w_in_b, cos_t, sin_t, gv_w, None, t, 0, tm, sp // tm)
    proj = _proj(xs, norm_mix, w_in_b, cos_t, sin_t, gv_w, proj, t, tp // tm, tm, ss // tm)

    lgf = jax.nn.log_sigmoid(ret_decay_fwd[0].astype(F32))
    lgb = jax.nn.log_sigmoid(ret_decay_bwd[0].astype(F32))
    group = _tile(4, sp // CHUNK, ss // CHUNK)
    sf, sb = _states(proj, lgf, lgb, group, tp // CHUNK, sp // CHUNK, ss // CHUNK)
    mixed = _mix(proj, sf, sb, lgf, lgb, gmlp_w_spatial[0].astype(BF16),
                 gmlp_b_spatial[0][:, :, None], ret_norm.reshape(1, SECTION), group)

    pad = LANES - N_EXPERTS - N_GROUPS
    wr = jnp.concatenate([w_router_expert[0], w_router_group[0], jnp.zeros((d, pad), F32)], axis=1)
    br = jnp.concatenate([b_router_expert[0], b_router_group[0], jnp.zeros((pad,), F32)])[None, :]
    x1, h3, route, route_t, cnt = _post(xp, xs, mixed, w_out[0].astype(BF16), norm_ffn,
                                        wr.astype(BF16), br, _tile(256, tp, ts))
    tm = _tile(256, tp, ts)

    n_blocks = -(-(TOP_K * t) // EXPERT_BLOCK) + N_EXPERTS
    dest, block_expert, n_used, pad_start, pad_count = _plan(
        cnt[0, :N_EXPERTS].astype(I32), route_t, n_blocks)
    dest_flat = dest.reshape(TOP_K * t)
    slot_tok = _invert(dest_flat, pad_start, pad_count, t, n_blocks * EXPERT_BLOCK)
    y = _experts(h3.reshape(t, ROW_TILES, LANES), block_expert, n_used, slot_tok,
                 w_expert_gate[0], w_expert_up[0], w_expert_down[0])

    out_p, out_s = _final(x1, route, norm_final[None, :], y.reshape(-1, ROW_TILES, LANES),
                          dest_flat, tp, tm)
    return out_p.reshape(bp, sp, d), out_s.reshape(bs, ss, d)
```

```python
import functools
import math

import jax
import jax.numpy as jnp
from jax import lax
from jax.experimental import pallas as pl
from jax.experimental.pallas import tpu as pltpu

F32 = jnp.float32
BF16 = jnp.bfloat16
I32 = jnp.int32
U32 = jnp.uint32

D_MODEL = 2048
CHUNK = 128
HEADS = 8
HEAD_DIM = 128
SECTION = HEADS * HEAD_DIM
N_SECTIONS = 6
PROJ_WIDTH = N_SECTIONS * SECTION
ROPE_BASE = 10000.0
N_GROUPS = 4
EXPERTS_PER_GROUP = 8
N_EXPERTS = N_GROUPS * EXPERTS_PER_GROUP
TOP_K = 2
D_EXPERT = D_MODEL // 4
EPS = 1e-6
LANES = 128
SUBLANES = 8
MXU_COLS = 256
ROW_TILES = D_MODEL // LANES // 2
EXPERT_BLOCK = 256
DMA_UNROLL = 8
VMEM_LIMIT = 56 * 1024 * 1024


def _rms(x, w):
    ms = jnp.mean(x * x, axis=-1, keepdims=True)
    return x * lax.rsqrt(ms + EPS) * w


def _pack_row_tiles(x):
    def bits(c):
        return lax.bitcast_convert_type(x[:, c * LANES:(c + 1) * LANES].astype(BF16).astype(F32), U32)
    return [(bits(2 * c) >> 16) | bits(2 * c + 1) for c in range(ROW_TILES)]


def _unpack_row_tiles(words):
    tiles = []
    for w in words:
        tiles.append(lax.bitcast_convert_type(w << 16, F32))
        tiles.append(lax.bitcast_convert_type(w & jnp.uint32(0xFFFF0000), F32))
    return tiles


def _proj_kernel(x_ref, nw_ref, w_ref, cos_ref, sin_ref, gvw_ref, prev_ref, o_ref, h_ref):
    del prev_ref
    j = pl.program_id(1)

    @pl.when(j == 0)
    def _():
        h_ref[...] = _rms(x_ref[...], nw_ref[...]).astype(BF16)

    def rotary(seg):
        return seg * cos_ref[...] + pltpu.roll(seg, HEAD_DIM // 2, axis=1) * sin_ref[...]

    def section(epilogue):
        for c in range(SECTION // MXU_COLS):
            acc = jnp.dot(h_ref[...], w_ref[:, c * MXU_COLS:(c + 1) * MXU_COLS],
                          preferred_element_type=F32)
            for g in range(MXU_COLS // HEAD_DIM):
                sl = slice(c * MXU_COLS + g * HEAD_DIM, c * MXU_COLS + (g + 1) * HEAD_DIM)
                o_ref[:, sl] = epilogue(acc[:, g * HEAD_DIM:(g + 1) * HEAD_DIM], sl).astype(BF16)

    @pl.when(j == 0)
    def _():
        section(lambda seg, sl: jax.nn.gelu(seg))

    @pl.when(j == 1)
    def _():
        section(lambda seg, sl: _rms(jax.nn.gelu(seg), gvw_ref[:, sl]))

    @pl.when(j == 2)
    def _():
        section(lambda seg, sl: rotary(seg))

    @pl.when(j == 3)
    def _():
        section(lambda seg, sl: rotary(seg) * (HEAD_DIM ** -0.5))

    @pl.when(j == 4)
    def _():
        section(lambda seg, sl: seg)

    @pl.when(j == 5)
    def _():
        section(lambda seg, sl: jax.nn.silu(seg))


def _proj(x, norm_w, w_in, cos_t, sin_t, gv_w, prev, t_total, row_block0, tm, per_seq):
    if prev is None:
        prev = jnp.zeros((SUBLANES, LANES), BF16)
        aliases = {}
    else:
        aliases = {6: 0}
    return pl.pallas_call(
        _proj_kernel,
        grid=(x.shape[0] // tm, N_SECTIONS),
        in_specs=[
            pl.BlockSpec((tm, D_MODEL), lambda i, j: (i, 0)),
            pl.BlockSpec((1, D_MODEL), lambda i, j: (0, 0)),
            pl.BlockSpec((D_MODEL, SECTION), lambda i, j: (0, j)),
            pl.BlockSpec((tm, HEAD_DIM), lambda i, j: (i % per_seq, 0)),
            pl.BlockSpec((tm, HEAD_DIM), lambda i, j: (i % per_seq, 0)),
            pl.BlockSpec((1, SECTION), lambda i, j: (0, 0)),
            pl.BlockSpec(memory_space=pl.ANY),
        ],
        out_specs=pl.BlockSpec((None, tm, SECTION), lambda i, j: (j, i + row_block0, 0)),
        out_shape=jax.ShapeDtypeStruct((N_SECTIONS, t_total, SECTION), BF16),
        scratch_shapes=[pltpu.VMEM((tm, D_MODEL), BF16)],
        input_output_aliases=aliases,
        compiler_params=pltpu.CompilerParams(
            dimension_semantics=("arbitrary", "arbitrary"), vmem_limit_bytes=VMEM_LIMIT),
        name="proj",
    )(x, norm_w, w_in, cos_t, sin_t, gv_w, prev)


def _seq_edge(c, chunks_p, n_p, n_s, last):
    pos = jnp.where(c < chunks_p, c % n_p, (c - chunks_p) % n_s)
    edge = jnp.where(c < chunks_p, n_p - 1, n_s - 1) if last else 0
    return pos == edge


def _states_kernel(lgf_ref, lgb_ref, kf_ref, vf_ref, kb_ref, vb_ref, sf_ref, sb_ref, stf, stb,
                   *, group, n_chunks, chunks_p, n_p, n_s):
    s = pl.program_id(0)
    row = lax.broadcasted_iota(jnp.int32, (CHUNK, HEAD_DIM), 0).astype(F32)
    tn = (((0,), (0,)), ((), ()))

    def one_chunk(g, carry):
        cf = s * group + g
        gb = group - 1 - g

        @pl.when(_seq_edge(cf, chunks_p, n_p, n_s, last=False))
        def _():
            stf[...] = jnp.zeros_like(stf)

        @pl.when(_seq_edge(n_chunks - 1 - cf, chunks_p, n_p, n_s, last=True))
        def _():
            stb[...] = jnp.zeros_like(stb)

        rf = pl.ds(pl.multiple_of(g * CHUNK, CHUNK), CHUNK)
        rb = pl.ds(pl.multiple_of(gb * CHUNK, CHUNK), CHUNK)
        for h in range(HEADS):
            sl = slice(h * HEAD_DIM, (h + 1) * HEAD_DIM)
            lgf = lgf_ref[h]
            lgb = lgb_ref[h]
            kd = (kf_ref[rf, sl].astype(F32) * jnp.exp((CHUNK - 1.0 - row) * lgf)).astype(BF16)
            kv = lax.dot_general(kd, vf_ref[rf, sl], tn, preferred_element_type=F32)
            sf_ref[g, h] = stf[h].astype(BF16)
            stf[h] = stf[h] * jnp.exp(CHUNK * lgf) + kv
            kd = (kb_ref[rb, sl].astype(F32) * jnp.exp(row * lgb)).astype(BF16)
            kv = lax.dot_general(kd, vb_ref[rb, sl], tn, preferred_element_type=F32)
            sb_ref[gb, h] = stb[h].astype(BF16)
            stb[h] = stb[h] * jnp.exp(CHUNK * lgb) + kv
        return carry
    lax.fori_loop(0, group, one_chunk, 0)


def _states(proj, lgf, lgb, group, chunks_p, n_p, n_s):
    n_chunks = proj.shape[1] // CHUNK
    last = n_chunks // group - 1
    rows = group * CHUNK
    smem = pl.BlockSpec(memory_space=pltpu.SMEM)
    st_shape = jax.ShapeDtypeStruct((n_chunks, HEADS, HEAD_DIM, HEAD_DIM), BF16)
    return pl.pallas_call(
        functools.partial(_states_kernel, group=group, n_chunks=n_chunks, chunks_p=chunks_p,
                          n_p=n_p, n_s=n_s),
        grid=(n_chunks // group,),
        in_specs=[
            smem, smem,
            pl.BlockSpec((None, rows, SECTION), lambda s: (3, s, 0)),
            pl.BlockSpec((None, rows, SECTION), lambda s: (4, s, 0)),
            pl.BlockSpec((None, rows, SECTION), lambda s: (3, last - s, 0)),
            pl.BlockSpec((None, rows, SECTION), lambda s: (4, last - s, 0)),
        ],
        out_specs=[
            pl.BlockSpec((group, HEADS, HEAD_DIM, HEAD_DIM), lambda s: (s, 0, 0, 0)),
            pl.BlockSpec((group, HEADS, HEAD_DIM, HEAD_DIM), lambda s: (last - s, 0, 0, 0)),
        ],
        out_shape=[st_shape, st_shape],
        scratch_shapes=[pltpu.VMEM((HEADS, HEAD_DIM, HEAD_DIM), F32),
                        pltpu.VMEM((HEADS, HEAD_DIM, HEAD_DIM), F32)],
        compiler_params=pltpu.CompilerParams(dimension_semantics=("arbitrary",)),
        name="states",
    )(lgf, lgb, proj, proj, proj, proj)


def _mix_kernel(lgf_ref, lgb_ref, u_ref, gv_ref, q_ref, k_ref, v_ref, g_ref, sf_ref, sb_ref,
                ws_ref, bs_ref, rn_ref, o_ref, dec_ref, qdf_ref, qdb_ref):
    @pl.when(pl.program_id(0) == 0)
    def _():
        row = lax.broadcasted_iota(jnp.int32, (CHUNK, CHUNK), 0).astype(F32)
        col = lax.broadcasted_iota(jnp.int32, (CHUNK, CHUNK), 1).astype(F32)
        diff = row - col
        for h in range(HEADS):
            lgf = lgf_ref[h]
            lgb = lgb_ref[h]
            dec_ref[h] = jnp.where(diff >= 0, jnp.exp(jnp.maximum(diff, 0.0) * lgf),
                                   jnp.exp(jnp.maximum(-diff, 0.0) * lgb))
            qdf_ref[h] = jnp.exp((row + 1.0) * lgf)
            qdb_ref[h] = jnp.exp((CHUNK - row) * lgb)

    nt = (((1,), (1,)), ((), ()))

    def one_chunk(g, carry):
        rows = pl.ds(pl.multiple_of(g * CHUNK, CHUNK), CHUNK)
        for h in range(HEADS):
            sl = slice(h * HEAD_DIM, (h + 1) * HEAD_DIM)
            mixed = jnp.dot(ws_ref[h], gv_ref[rows, sl], preferred_element_type=F32) + bs_ref[h]
            o_ref[rows, sl] = (u_ref[rows, sl].astype(F32) * mixed).astype(BF16)
            q = q_ref[rows, sl]
            v = v_ref[rows, sl]
            scores = lax.dot_general(q, k_ref[rows, sl], nt, preferred_element_type=F32)
            ret = jnp.dot((scores * dec_ref[h]).astype(BF16), v, preferred_element_type=F32)
            ret += jnp.dot(q, sf_ref[g, h], preferred_element_type=F32) * qdf_ref[h]
            ret += jnp.dot(q, sb_ref[g, h], preferred_element_type=F32) * qdb_ref[h]
            out = _rms(ret, rn_ref[:, sl]) * g_ref[rows, sl].astype(F32)
            o_ref[rows, SECTION + h * HEAD_DIM:SECTION + (h + 1) * HEAD_DIM] = out.astype(BF16)
        return carry
    lax.fori_loop(0, u_ref.shape[0] // CHUNK, one_chunk, 0)


def _mix(proj, sf, sb, lgf, lgb, ws, bs, rn, group):
    t = proj.shape[1]
    rows = group * CHUNK
    smem = pl.BlockSpec(memory_space=pltpu.SMEM)

    def sec(j):
        return pl.BlockSpec((None, rows, SECTION), lambda c, j=j: (j, c, 0))

    st_spec = pl.BlockSpec((group, HEADS, HEAD_DIM, HEAD_DIM), lambda c: (c, 0, 0, 0))
    tab = pltpu.VMEM((HEADS, CHUNK, CHUNK), F32)
    return pl.pallas_call(
        _mix_kernel,
        grid=(t // rows,),
        in_specs=[
            smem, smem, sec(0), sec(1), sec(2), sec(3), sec(4), sec(5), st_spec, st_spec,
            pl.BlockSpec((HEADS, CHUNK, CHUNK), lambda c: (0, 0, 0)),
            pl.BlockSpec((HEADS, CHUNK, 1), lambda c: (0, 0, 0)),
            pl.BlockSpec((1, SECTION), lambda c: (0, 0)),
        ],
        out_specs=pl.BlockSpec((rows, 2 * SECTION), lambda c: (c, 0)),
        out_shape=jax.ShapeDtypeStruct((t, 2 * SECTION), BF16),
        scratch_shapes=[tab, tab, tab],
        compiler_params=pltpu.CompilerParams(dimension_semantics=("arbitrary",)),
        name="mix",
    )(lgf, lgb, proj, proj, proj, proj, proj, proj, sf, sb, ws, bs, rn)


def _post_kernel(xp_ref, xs_ref, m_ref, wo_ref, nw_ref, wr_ref, br_ref,
                 x1_ref, h3_ref, route_ref, rt_ref, cnt_ref, x1_prev, *, tm, blocks_p, blocks_t):
    i = pl.program_id(0)

    @pl.when(i == 0)
    def _():
        cnt_ref[...] = jnp.zeros_like(cnt_ref)
        x1_prev[...] = jnp.zeros_like(x1_prev)

    h = _rms(x1_prev[...], nw_ref[...])
    for c, words in enumerate(_pack_row_tiles(h)):
        h3_ref[pl.ds(c, tm, stride=ROW_TILES), :] = words
    logits = jnp.dot(h.astype(BF16), wr_ref[...], preferred_element_type=F32) + br_ref[...]

    x = jnp.where(jnp.minimum(i, blocks_t - 1) < blocks_p, xp_ref[...], xs_ref[...])
    x1 = x + jnp.dot(m_ref[...], wo_ref[...], preferred_element_type=F32)
    x1_ref[...] = x1
    x1_prev[...] = x1

    lane = lax.broadcasted_iota(jnp.int32, (tm, LANES), 1).astype(F32)
    neg = jnp.float32(-jnp.inf)

    def first_max(vals):
        m = jnp.max(vals, axis=-1, keepdims=True)
        idx = jnp.min(jnp.where(vals == m, lane, float(LANES)), axis=-1, keepdims=True)
        return m, idx

    lg = jnp.where((lane >= N_EXPERTS) & (lane < N_EXPERTS + N_GROUPS), logits, neg)
    gmax, gidx = first_max(lg)
    p_sel = 1.0 / jnp.sum(jnp.exp(lg - gmax), axis=-1, keepdims=True)
    lo = (gidx - N_EXPERTS) * EXPERTS_PER_GROUP
    le = jnp.where((lane >= lo) & (lane < lo + EXPERTS_PER_GROUP), logits, neg)
    top1, i1 = first_max(le)
    top2, i2 = first_max(jnp.where(lane == i1, neg, le))
    e2 = jnp.exp(top2 - top1)
    w1 = p_sel / (1.0 + e2)
    w2 = p_sel * e2 / (1.0 + e2)

    hit1 = lane == i1
    hit2 = lane == i2
    onehot = jnp.where((hit1 | hit2) & (i > 0), 1.0, 0.0).astype(BF16)
    r = lax.broadcasted_iota(jnp.int32, (tm, tm), 0)
    c = lax.broadcasted_iota(jnp.int32, (tm, tm), 1)
    lower = jnp.where(c < r, 1.0, 0.0).astype(BF16)
    before = jnp.dot(lower, onehot, preferred_element_type=F32) + cnt_ref[...]
    rank1 = jnp.sum(jnp.where(hit1, before, 0.0), axis=-1, keepdims=True)
    rank2 = jnp.sum(jnp.where(hit2, before, 0.0), axis=-1, keepdims=True)
    cnt_ref[...] += jnp.sum(onehot.astype(F32), axis=0, keepdims=True)

    route = jnp.where(lane == 0, i1, 0.0)
    route = jnp.where(lane == 1, i2, route)
    route = jnp.where(lane == 2, rank1, route)
    route = jnp.where(lane == 3, rank2, route)
    route = jnp.where(lane == 4, w1, route)
    route = jnp.where(lane == 5, w2, route)
    route_ref[...] = route
    rt_ref[...] = jnp.transpose(route)[0:SUBLANES, :]


def _post(xp, xs, mixed, w_out, norm_w, wr, br, tm):
    t = mixed.shape[0]
    blocks_p = xp.shape[0] // tm
    blocks_t = t // tm
    last = blocks_t - 1

    def cur(i):
        return jnp.minimum(i, last)

    def prev(i):
        return jnp.maximum(i - 1, 0)

    return pl.pallas_call(
        functools.partial(_post_kernel, tm=tm, blocks_p=blocks_p, blocks_t=blocks_t),
        grid=(blocks_t + 1,),
        in_specs=[
            pl.BlockSpec((tm, D_MODEL), lambda i: (jnp.minimum(i, blocks_p - 1), 0)),
            pl.BlockSpec((tm, D_MODEL), lambda i: (jnp.clip(i - blocks_p, 0, last - blocks_p), 0)),
            pl.BlockSpec((tm, D_MODEL), lambda i: (cur(i), 0)),
            pl.BlockSpec((D_MODEL, D_MODEL), lambda i: (0, 0), pipeline_mode=pl.Buffered(1)),
            pl.BlockSpec((1, D_MODEL), lambda i: (0, 0)),
            pl.BlockSpec((D_MODEL, LANES), lambda i: (0, 0)),
            pl.BlockSpec((1, LANES), lambda i: (0, 0)),
        ],
        out_specs=[
            pl.BlockSpec((tm, D_MODEL), lambda i: (cur(i), 0)),
            pl.BlockSpec((tm * ROW_TILES, LANES), lambda i: (prev(i), 0)),
            pl.BlockSpec((tm, LANES), lambda i: (prev(i), 0)),
            pl.BlockSpec((SUBLANES, tm), lambda i: (0, prev(i))),
            pl.BlockSpec((1, LANES), lambda i: (0, 0)),
        ],
        scratch_shapes=[pltpu.VMEM((tm, D_MODEL), F32)],
        out_shape=[
            jax.ShapeDtypeStruct((t, D_MODEL), F32),
            jax.ShapeDtypeStruct((t * ROW_TILES, LANES), U32),
            jax.ShapeDtypeStruct((t, LANES), F32),
            jax.ShapeDtypeStruct((SUBLANES, t), F32),
            jax.ShapeDtypeStruct((1, LANES), F32),
        ],
        compiler_params=pltpu.CompilerParams(
            dimension_semantics=("arbitrary",), vmem_limit_bytes=VMEM_LIMIT),
        name="post",
    )(xp, xs, mixed, w_out, norm_w, wr, br)


def _plan_kernel(cnt_ref, rt_ref, dest_ref, be_ref, nu_ref, zst_ref, zcn_ref, ps_ref, *, n_blocks):
    def per_expert(e, first):
        n = (cnt_ref[e] + (EXPERT_BLOCK - 1)) // EXPERT_BLOCK
        ps_ref[e] = first * EXPERT_BLOCK
        zst_ref[e] = first * EXPERT_BLOCK + cnt_ref[e]
        zcn_ref[e] = n * EXPERT_BLOCK - cnt_ref[e]

        def fill(j, carry):
            be_ref[first + j] = e
            return carry
        lax.fori_loop(0, n, fill, 0)
        return first + n
    n_used = lax.fori_loop(0, N_EXPERTS, per_expert, 0)
    nu_ref[0] = n_used

    def fill_tail(j, carry):
        be_ref[j] = N_EXPERTS - 1
        return carry
    lax.fori_loop(n_used, n_blocks, fill_tail, 0)

    for k in range(TOP_K):
        e = rt_ref[k:k + 1, :]
        base = jnp.zeros_like(e)
        for x in range(N_EXPERTS):
            base = jnp.where(e == float(x), ps_ref[x].astype(F32), base)
        dest_ref[k:k + 1, :] = (base + rt_ref[TOP_K + k:TOP_K + k + 1, :]).astype(I32)


def _plan(counts, route_t, n_blocks):
    t = route_t.shape[1]
    smem = pl.BlockSpec(memory_space=pltpu.SMEM)
    return pl.pallas_call(
        functools.partial(_plan_kernel, n_blocks=n_blocks),
        grid=(1,),
        in_specs=[smem, pl.BlockSpec((SUBLANES, t), lambda i: (0, 0))],
        out_specs=[pl.BlockSpec((TOP_K, t), lambda i: (0, 0)), smem, smem, smem, smem],
        out_shape=[
            jax.ShapeDtypeStruct((TOP_K, t), I32),
            jax.ShapeDtypeStruct((n_blocks,), I32),
            jax.ShapeDtypeStruct((1,), I32),
            jax.ShapeDtypeStruct((N_EXPERTS,), I32),
            jax.ShapeDtypeStruct((N_EXPERTS,), I32),
        ],
        scratch_shapes=[pltpu.SMEM((N_EXPERTS,), I32)],
        compiler_params=pltpu.CompilerParams(dimension_semantics=("arbitrary",)),
        name="plan",
    )(counts, route_t)


def _invert_kernel(dest_ref, zst_ref, zcn_ref, tok_ref, *, n_tok):
    def pad_expert(e, carry):
        start = zst_ref[e]

        def one(j, c):
            tok_ref[start + j] = 0
            return c
        lax.fori_loop(0, zcn_ref[e], one, 0)
        return carry
    lax.fori_loop(0, N_EXPERTS, pad_expert, 0)

    def fill(i, carry):
        for u in range(DMA_UNROLL):
            tok = i * DMA_UNROLL + u
            for k in range(TOP_K):
                tok_ref[dest_ref[k * n_tok + tok]] = tok
        return carry
    lax.fori_loop(0, n_tok // DMA_UNROLL, fill, 0)


def _invert(dest_flat, pad_start, pad_count, n_tok, n_slots):
    assert EXPERT_BLOCK & (EXPERT_BLOCK - 1) == 0
    smem = pl.BlockSpec(memory_space=pltpu.SMEM)
    return pl.pallas_call(
        functools.partial(_invert_kernel, n_tok=n_tok),
        grid_spec=pltpu.PrefetchScalarGridSpec(
            num_scalar_prefetch=3,
            grid=(1,),
            in_specs=[],
            out_specs=smem,
        ),
        out_shape=jax.ShapeDtypeStruct((n_slots,), I32),
        compiler_params=pltpu.CompilerParams(dimension_semantics=("arbitrary",)),
        name="invert",
    )(dest_flat, pad_start, pad_count)


def _row_slab(buf, r):
    start = r * ROW_TILES
    if not isinstance(r, int):
        start = pl.multiple_of(start, ROW_TILES)
    return buf.at[pl.ds(start, ROW_TILES)]


def _row_loop(n, fn):
    def body(i, carry):
        for u in range(DMA_UNROLL):
            fn(i * DMA_UNROLL + u)
        return carry
    lax.fori_loop(0, n // DMA_UNROLL, body, 0)


def _experts_kernel(be_ref, nu_ref, tok_ref, h3_ref, wg_ref, wu_ref, wd_ref, y_ref,
                    xb0, xb1, wgs, wus, wds, wgb, wub, wdb, wslot, gsem, wsem):
    b = pl.program_id(0)
    n_used = nu_ref[0]
    blk = EXPERT_BLOCK

    def gather_copy(tok, buf, r, par):
        return pltpu.make_async_copy(h3_ref.at[tok], _row_slab(buf, r), gsem.at[par])

    def gather_wait(buf, par):
        _row_loop(blk, lambda r: gather_copy(0, buf, r, par).wait())

    def weight_copies(e, slot):
        return [pltpu.make_async_copy(src.at[e], dst.at[slot], wsem.at[slot])
                for src, dst in ((wg_ref, wgs), (wu_ref, wus), (wd_ref, wds))]

    def step(par, x_cur, x_nxt):
        @pl.when(b == 0)
        def _():
            _row_loop(blk, lambda r: gather_copy(tok_ref[r], x_cur, r, par).start())
            wslot[0] = 0
            for c in weight_copies(be_ref[0], 0):
                c.start()

        e = be_ref[b]

        @pl.when((b == 0) | (e != be_ref[jnp.maximum(b - 1, 0)]))
        def _():
            slot = wslot[0]
            for c in weight_copies(e, slot):
                c.wait()
            wgb[...] = wgs[slot].astype(BF16)
            wub[...] = wus[slot].astype(BF16)
            wdb[...] = wds[slot].astype(BF16)
            nb = lax.while_loop(lambda j: (j < n_used) & (be_ref[jnp.minimum(j, n_used - 1)] == e),
                                lambda j: j + 1, b + 1)

            @pl.when(nb < n_used)
            def _():
                for c in weight_copies(be_ref[nb], 1 - slot):
                    c.start()
            wslot[0] = 1 - slot

        gather_wait(x_cur, par)

        x = jnp.concatenate(_unpack_row_tiles(
            [x_cur[pl.ds(c, blk, stride=ROW_TILES), :] for c in range(ROW_TILES)]), axis=1
        ).astype(BF16)
        nxt = jnp.minimum(b + 1, n_used - 1) * blk
        for r in range(blk):
            gather_copy(tok_ref[nxt + r], x_nxt, r, 1 - par).start(priority=r % 2)
        gate = jnp.dot(x, wgb[...], preferred_element_type=F32)
        up = jnp.dot(x, wub[...], preferred_element_type=F32)
        hid = (jax.nn.silu(gate) * up).astype(BF16)
        y = jnp.dot(hid, wdb[...], preferred_element_type=F32)
        for c, words in enumerate(_pack_row_tiles(y)):
            y_ref[pl.ds(c, blk, stride=ROW_TILES), :] = words

        @pl.when(b == n_used - 1)
        def _():
            gather_wait(x_nxt, 1 - par)

    for par, bufs in ((0, (xb0, xb1)), (1, (xb1, xb0))):
        @pl.when((b < n_used) & (b % 2 == par))
        def _(par=par, bufs=bufs):
            step(par, *bufs)


def _experts(h3, block_expert, n_used, slot_tok, wg, wu, wd):
    n_blocks = block_expert.shape[0]

    def used(b, nu):
        return jnp.minimum(b, nu[0] - 1)

    rows = EXPERT_BLOCK * ROW_TILES
    buf = pltpu.VMEM((rows, LANES), U32)
    hbm = pl.BlockSpec(memory_space=pl.ANY)
    return pl.pallas_call(
        _experts_kernel,
        grid_spec=pltpu.PrefetchScalarGridSpec(
            num_scalar_prefetch=3,
            grid=(n_blocks,),
            in_specs=[hbm, hbm, hbm, hbm],
            out_specs=pl.BlockSpec((rows, LANES), lambda b, be, nu, st: (used(b, nu), 0)),
            scratch_shapes=[
                buf, buf,
                pltpu.VMEM((2, D_MODEL, D_EXPERT), F32),
                pltpu.VMEM((2, D_MODEL, D_EXPERT), F32),
                pltpu.VMEM((2, D_EXPERT, D_MODEL), F32),
                pltpu.VMEM((D_MODEL, D_EXPERT), BF16),
                pltpu.VMEM((D_MODEL, D_EXPERT), BF16),
                pltpu.VMEM((D_EXPERT, D_MODEL), BF16),
                pltpu.SMEM((1,), I32),
                pltpu.SemaphoreType.DMA((2,)),
                pltpu.SemaphoreType.DMA((2,)),
            ],
        ),
        out_shape=jax.ShapeDtypeStruct((n_blocks * rows, LANES), U32),
        compiler_params=pltpu.CompilerParams(
            dimension_semantics=("arbitrary",), vmem_limit_bytes=VMEM_LIMIT),
        name="experts",
    )(block_expert, n_used, slot_tok, h3, wg, wu, wd)


def _final_kernel(d_ref, x1_ref, route_ref, nw_ref, y_ref, op_ref, os_ref,
                  ya0, yb0, ya1, yb1, sem, *, blocks_p, blocks_t, n_tok):
    i = pl.program_id(0)
    tm = x1_ref.shape[0]

    def copy(slot, buf, r, par):
        return pltpu.make_async_copy(y_ref.at[slot], _row_slab(buf, r), sem.at[par])

    def gather_start(tile, bufs, par):
        def one(r):
            for k in range(TOP_K):
                copy(d_ref[k * n_tok + tile * tm + r], bufs[k], r, par).start()
        _row_loop(tm, one)

    def gather_wait(bufs, par):
        def one(r):
            for k in range(TOP_K):
                copy(0, bufs[k], r, par).wait()
        _row_loop(tm, one)

    def rows(buf):
        return jnp.concatenate(_unpack_row_tiles(
            [buf[pl.ds(c, tm, stride=ROW_TILES), :] for c in range(ROW_TILES)]), axis=1)

    def step(par, cur, nxt):
        @pl.when(i == 0)
        def _():
            gather_start(0, cur, par)

        @pl.when(i + 1 < blocks_t)
        def _():
            gather_start(i + 1, nxt, 1 - par)

        gather_wait(cur, par)
        w1 = route_ref[:, 4:5]
        w2 = route_ref[:, 5:6]
        x2 = x1_ref[...] + (rows(cur[0]) * w1 + rows(cur[1]) * w2)
        out = _rms(x2, nw_ref[...])

        @pl.when(i < blocks_p)
        def _():
            op_ref[...] = out

        @pl.when(i >= blocks_p)
        def _():
            os_ref[...] = out

    for par, bufs in ((0, ((ya0, yb0), (ya1, yb1))), (1, ((ya1, yb1), (ya0, yb0)))):
        @pl.when(i % 2 == par)
        def _(par=par, bufs=bufs):
            step(par, *bufs)


def _final(x1, route, norm_w, y, dest_flat, tp, tm):
    t = x1.shape[0]
    blocks_p = tp // tm
    blocks_t = t // tm
    buf = pltpu.VMEM((tm * ROW_TILES, LANES), U32)
    return pl.pallas_call(
        functools.partial(_final_kernel, blocks_p=blocks_p, blocks_t=blocks_t, n_tok=t),
        grid_spec=pltpu.PrefetchScalarGridSpec(
            num_scalar_prefetch=1,
            grid=(blocks_t,),
            in_specs=[
                pl.BlockSpec((tm, D_MODEL), lambda i, d: (i, 0)),
                pl.BlockSpec((tm, LANES), lambda i, d: (i, 0)),
                pl.BlockSpec((1, D_MODEL), lambda i, d: (0, 0)),
                pl.BlockSpec(memory_space=pl.ANY),
            ],
            out_specs=[
                pl.BlockSpec((tm, D_MODEL), lambda i, d: (jnp.minimum(i, blocks_p - 1), 0)),
                pl.BlockSpec((tm, D_MODEL), lambda i, d: (jnp.maximum(i - blocks_p, 0), 0)),
            ],
            scratch_shapes=[buf, buf, buf, buf, pltpu.SemaphoreType.DMA((2,))],
        ),
        out_shape=[
            jax.ShapeDtypeStruct((tp, D_MODEL), F32),
            jax.ShapeDtypeStruct((t - tp, D_MODEL), F32),
        ],
        compiler_params=pltpu.CompilerParams(dimension_semantics=("arbitrary",)),
        name="final",
    )(dest_flat, x1, route, norm_w, y)


def _rope_tables(seq_len):
    half = HEAD_DIM // 2
    inv = ROPE_BASE ** (-jnp.arange(half, dtype=F32) / half)
    ang = jnp.arange(seq_len, dtype=F32)[:, None] * inv[None, :]
    cos = jnp.cos(ang)
    sin = jnp.sin(ang)
    return jnp.concatenate([cos, cos], axis=1), jnp.concatenate([-sin, sin], axis=1)


def _tile(limit, *sizes):
    return min(limit, math.gcd(*sizes))


def kernel(x_prompt, x_sample, norm_mix, w_in, gmlp_norm_v, gmlp_w_spatial, gmlp_b_spatial, ret_decay_fwd, ret_decay_bwd, ret_norm, w_out, norm_ffn, w_router_group, b_router_group, w_router_expert, b_router_expert, w_expert_gate, w_expert_up, w_expert_down, norm_final):
    assert norm_mix.shape[0] == 1, "single-layer block"
    bp, sp, d = x_prompt.shape
    bs, ss, _ = x_sample.shape
    assert d == D_MODEL and sp % CHUNK == 0 and ss % CHUNK == 0
    tp, ts = bp * sp, bs * ss
    t = tp + ts
    xp = x_prompt.reshape(tp, d)
    xs = x_sample.reshape(ts, d)

    tm = _tile(1024, sp, ss)
    cos_t, sin_t = _rope_tables(max(sp, ss))
    w_in_b = w_in[0].astype(BF16)
    gv_w = gmlp_norm_v.reshape(1, SECTION)
    proj = _proj(xp, norm_mix, w_in_b, cos_t, sin_t, gv_w, None, t, 0, tm, sp // tm)
    proj = _proj(xs, norm_mix, w_in_b, cos_t, sin_t, gv_w, proj, t, tp // tm, tm, ss // tm)

    lgf = jax.nn.log_sigmoid(ret_decay_fwd[0].astype(F32))
    lgb = jax.nn.log_sigmoid(ret_decay_bwd[0].astype(F32))
    group = _tile(4, sp // CHUNK, ss // CHUNK)
    sf, sb = _states(proj, lgf, lgb, group, tp // CHUNK, sp // CHUNK, ss // CHUNK)
    mixed = _mix(proj, sf, sb, lgf, lgb, gmlp_w_spatial[0].astype(BF16),
                 gmlp_b_spatial[0][:, :, None], ret_norm.reshape(1, SECTION), group)

    pad = LANES - N_EXPERTS - N_GROUPS
    wr = jnp.concatenate([w_router_expert[0], w_router_group[0], jnp.zeros((d, pad), F32)], axis=1)
    br = jnp.concatenate([b_router_expert[0], b_router_group[0], jnp.zeros((pad,), F32)])[None, :]
    x1, h3, route, route_t, cnt = _post(xp, xs, mixed, w_out[0].astype(BF16), norm_ffn,
                                        wr.astype(BF16), br, _tile(256, tp, ts))
    tm = _tile(256, tp, ts)

    n_blocks = -(-(TOP_K * t) // EXPERT_BLOCK) + N_EXPERTS
    dest, block_expert, n_used, pad_start, pad_count = _plan(
        cnt[0, :N_EXPERTS].astype(I32), route_t, n_blocks)
    dest_flat = dest.reshape(TOP_K * t)
    slot_tok = _invert(dest_flat, pad_start, pad_count, t, n_blocks * EXPERT_BLOCK)
    y = _experts(h3.reshape(t, ROW_TILES, LANES), block_expert, n_used, slot_tok,
                 w_expert_gate[0], w_expert_up[0], w_expert_down[0])

    out_p, out_s = _final(x1, route, norm_final[None, :], y.reshape(-1, ROW_TILES, LANES),
                          dest_flat, tp, tm)
    return out_p.reshape(bp, sp, d), out_s.reshape(bs, ss, d)
```

```python
import functools
import math

import jax
import jax.numpy as jnp
from jax import lax
from jax.experimental import pallas as pl
from jax.experimental.pallas import tpu as pltpu

F32 = jnp.float32
BF16 = jnp.bfloat16
I32 = jnp.int32
U32 = jnp.uint32

D_MODEL = 2048
CHUNK = 128
HEADS = 8
HEAD_DIM = 128
SECTION = HEADS * HEAD_DIM
N_SECTIONS = 6
PROJ_WIDTH = N_SECTIONS * SECTION
ROPE_BASE = 10000.0
N_GROUPS = 4
EXPERTS_PER_GROUP = 8
N_EXPERTS = N_GROUPS * EXPERTS_PER_GROUP
TOP_K = 2
D_EXPERT = D_MODEL // 4
EPS = 1e-6
LANES = 128
SUBLANES = 8
MXU_COLS = 256
ROW_TILES = D_MODEL // LANES // 2
EXPERT_BLOCK = 256
DMA_UNROLL = 8
VMEM_LIMIT = 56 * 1024 * 1024


def _rms(x, w):
    ms = jnp.mean(x * x, axis=-1, keepdims=True)
    return x * lax.rsqrt(ms + EPS) * w


def _pack_row_tiles(x):
    def bits(c):
        return lax.bitcast_convert_type(x[:, c * LANES:(c + 1) * LANES].astype(BF16).astype(F32), U32)
    return [(bits(2 * c) >> 16) | bits(2 * c + 1) for c in range(ROW_TILES)]


def _unpack_row_tiles(words):
    tiles = []
    for w in words:
        tiles.append(lax.bitcast_convert_type(w << 16, F32))
        tiles.append(lax.bitcast_convert_type(w & jnp.uint32(0xFFFF0000), F32))
    return tiles


def _proj_kernel(x_ref, nw_ref, w_ref, cos_ref, sin_ref, gvw_ref, prev_ref, o_ref, h_ref):
    del prev_ref
    j = pl.program_id(1)

    @pl.when(j == 0)
    def _():
        h_ref[...] = _rms(x_ref[...], nw_ref[...]).astype(BF16)

    def rotary(seg):
        return seg * cos_ref[...] + pltpu.roll(seg, HEAD_DIM // 2, axis=1) * sin_ref[...]

    def section(epilogue):
        for c in range(SECTION // MXU_COLS):
            acc = jnp.dot(h_ref[...], w_ref[:, c * MXU_COLS:(c + 1) * MXU_COLS],
                          preferred_element_type=F32)
            for g in range(MXU_COLS // HEAD_DIM):
                sl = slice(c * MXU_COLS + g * HEAD_DIM, c * MXU_COLS + (g + 1) * HEAD_DIM)
                o_ref[:, sl] = epilogue(acc[:, g * HEAD_DIM:(g + 1) * HEAD_DIM], sl).astype(BF16)

    @pl.when(j == 0)
    def _():
        section(lambda seg, sl: jax.nn.gelu(seg))

    @pl.when(j == 1)
    def _():
        section(lambda seg, sl: _rms(jax.nn.gelu(seg), gvw_ref[:, sl]))

    @pl.when(j == 2)
    def _():
        section(lambda seg, sl: rotary(seg))

    @pl.when(j == 3)
    def _():
        section(lambda seg, sl: rotary(seg) * (HEAD_DIM ** -0.5))

    @pl.when(j == 4)
    def _():
        section(lambda seg, sl: seg)

    @pl.when(j == 5)
    def _():
        section(lambda seg, sl: jax.nn.silu(seg))


def _proj(x, norm_w, w_in, cos_t, sin_t, gv_w, prev, t_total, row_block0, tm, per_seq):
    if prev is None:
        prev = jnp.zeros((SUBLANES, LANES), BF16)
        aliases = {}
    else:
        aliases = {6: 0}
    return pl.pallas_call(
        _proj_kernel,
        grid=(x.shape[0] // tm, N_SECTIONS),
        in_specs=[
            pl.BlockSpec((tm, D_MODEL), lambda i, j: (i, 0)),
            pl.BlockSpec((1, D_MODEL), lambda i, j: (0, 0)),
            pl.BlockSpec((D_MODEL, SECTION), lambda i, j: (0, j)),
            pl.BlockSpec((tm, HEAD_DIM), lambda i, j: (i % per_seq, 0)),
            pl.BlockSpec((tm, HEAD_DIM), lambda i, j: (i % per_seq, 0)),
            pl.BlockSpec((1, SECTION), lambda i, j: (0, 0)),
            pl.BlockSpec(memory_space=pl.ANY),
        ],
        out_specs=pl.BlockSpec((None, tm, SECTION), lambda i, j: (j, i + row_block0, 0)),
        out_shape=jax.ShapeDtypeStruct((N_SECTIONS, t_total, SECTION), BF16),
        scratch_shapes=[pltpu.VMEM((tm, D_MODEL), BF16)],
        input_output_aliases=aliases,
        compiler_params=pltpu.CompilerParams(
            dimension_semantics=("arbitrary", "arbitrary"), vmem_limit_bytes=VMEM_LIMIT),
        name="proj",
    )(x, norm_w, w_in, cos_t, sin_t, gv_w, prev)


def _seq_edge(c, chunks_p, n_p, n_s, last):
    pos = jnp.where(c < chunks_p, c % n_p, (c - chunks_p) % n_s)
    edge = jnp.where(c < chunks_p, n_p - 1, n_s - 1) if last else 0
    return pos == edge


def _states_kernel(lgf_ref, lgb_ref, kf_ref, vf_ref, kb_ref, vb_ref, sf_ref, sb_ref, stf, stb,
                   *, group, n_chunks, chunks_p, n_p, n_s):
    s = pl.program_id(0)
    row = lax.broadcasted_iota(jnp.int32, (CHUNK, HEAD_DIM), 0).astype(F32)
    tn = (((0,), (0,)), ((), ()))

    def one_chunk(g, carry):
        cf = s * group + g
        gb = group - 1 - g

        @pl.when(_seq_edge(cf, chunks_p, n_p, n_s, last=False))
        def _():
            stf[...] = jnp.zeros_like(stf)

        @pl.when(_seq_edge(n_chunks - 1 - cf, chunks_p, n_p, n_s, last=True))
        def _():
            stb[...] = jnp.zeros_like(stb)

        rf = pl.ds(pl.multiple_of(g * CHUNK, CHUNK), CHUNK)
        rb = pl.ds(pl.multiple_of(gb * CHUNK, CHUNK), CHUNK)
        for h in range(HEADS):
            sl = slice(h * HEAD_DIM, (h + 1) * HEAD_DIM)
            lgf = lgf_ref[h]
            lgb = lgb_ref[h]
            kd = (kf_ref[rf, sl].astype(F32) * jnp.exp((CHUNK - 1.0 - row) * lgf)).astype(BF16)
            kv = lax.dot_general(kd, vf_ref[rf, sl], tn, preferred_element_type=F32)
            sf_ref[g, h] = stf[h].astype(BF16)
            stf[h] = stf[h] * jnp.exp(CHUNK * lgf) + kv
            kd = (kb_ref[rb, sl].astype(F32) * jnp.exp(row * lgb)).astype(BF16)
            kv = lax.dot_general(kd, vb_ref[rb, sl], tn, preferred_element_type=F32)
            sb_ref[gb, h] = stb[h].astype(BF16)
            stb[h] = stb[h] * jnp.exp(CHUNK * lgb) + kv
        return carry
    lax.fori_loop(0, group, one_chunk, 0)


def _states(proj, lgf, lgb, group, chunks_p, n_p, n_s):
    n_chunks = proj.shape[1] // CHUNK
    last = n_chunks // group - 1
    rows = group * CHUNK
    smem = pl.BlockSpec(memory_space=pltpu.SMEM)
    st_shape = jax.ShapeDtypeStruct((n_chunks, HEADS, HEAD_DIM, HEAD_DIM), BF16)
    return pl.pallas_call(
        functools.partial(_states_kernel, group=group, n_chunks=n_chunks, chunks_p=chunks_p,
                          n_p=n_p, n_s=n_s),
        grid=(n_chunks // group,),
        in_specs=[
            smem, smem,
            pl.BlockSpec((None, rows, SECTION), lambda s: (3, s, 0)),
            pl.BlockSpec((None, rows, SECTION), lambda s: (4, s, 0)),
            pl.BlockSpec((None, rows, SECTION), lambda s: (3, last - s, 0)),
            pl.BlockSpec((None, rows, SECTION), lambda s: (4, last - s, 0)),
        ],
        out_specs=[
            pl.BlockSpec((group, HEADS, HEAD_DIM, HEAD_DIM), lambda s: (s, 0, 0, 0)),
            pl.BlockSpec((group, HEADS, HEAD_DIM, HEAD_DIM), lambda s: (last - s, 0, 0, 0)),
        ],
        out_shape=[st_shape, st_shape],
        scratch_shapes=[pltpu.VMEM((HEADS, HEAD_DIM, HEAD_DIM), F32),
                        pltpu.VMEM((HEADS, HEAD_DIM, HEAD_DIM), F32)],
        compiler_params=pltpu.CompilerParams(
            dimension_semantics=("arbitrary",), vmem_limit_bytes=VMEM_LIMIT),
        name="states",
    )(lgf, lgb, proj, proj, proj, proj)


def _mix_kernel(lgf_ref, lgb_ref, u_ref, gv_ref, q_ref, k_ref, v_ref, g_ref, sf_ref, sb_ref,
                ws_ref, bs_ref, rn_ref, o_ref, dec_ref, qdf_ref, qdb_ref):
    @pl.when(pl.program_id(0) == 0)
    def _():
        row = lax.broadcasted_iota(jnp.int32, (CHUNK, CHUNK), 0).astype(F32)
        col = lax.broadcasted_iota(jnp.int32, (CHUNK, CHUNK), 1).astype(F32)
        diff = row - col
        for h in range(HEADS):
            lgf = lgf_ref[h]
            lgb = lgb_ref[h]
            dec_ref[h] = jnp.where(diff >= 0, jnp.exp(jnp.maximum(diff, 0.0) * lgf),
                                   jnp.exp(jnp.maximum(-diff, 0.0) * lgb))
            qdf_ref[h] = jnp.exp((row + 1.0) * lgf)
            qdb_ref[h] = jnp.exp((CHUNK - row) * lgb)

    nt = (((1,), (1,)), ((), ()))

    def one_chunk(g, carry):
        rows = pl.ds(pl.multiple_of(g * CHUNK, CHUNK), CHUNK)
        for h in range(HEADS):
            sl = slice(h * HEAD_DIM, (h + 1) * HEAD_DIM)
            mixed = jnp.dot(ws_ref[h], gv_ref[rows, sl], preferred_element_type=F32) + bs_ref[h]
            o_ref[rows, sl] = (u_ref[rows, sl].astype(F32) * mixed).astype(BF16)
            q = q_ref[rows, sl]
            v = v_ref[rows, sl]
            scores = lax.dot_general(q, k_ref[rows, sl], nt, preferred_element_type=F32)
            ret = jnp.dot((scores * dec_ref[h]).astype(BF16), v, preferred_element_type=F32)
            ret += jnp.dot(q, sf_ref[g, h], preferred_element_type=F32) * qdf_ref[h]
            ret += jnp.dot(q, sb_ref[g, h], preferred_element_type=F32) * qdb_ref[h]
            out = _rms(ret, rn_ref[:, sl]) * g_ref[rows, sl].astype(F32)
            o_ref[rows, SECTION + h * HEAD_DIM:SECTION + (h + 1) * HEAD_DIM] = out.astype(BF16)
        return carry
    lax.fori_loop(0, u_ref.shape[0] // CHUNK, one_chunk, 0)


def _mix(proj, sf, sb, lgf, lgb, ws, bs, rn, group):
    t = proj.shape[1]
    rows = group * CHUNK
    smem = pl.BlockSpec(memory_space=pltpu.SMEM)

    def sec(j):
        return pl.BlockSpec((None, rows, SECTION), lambda c, j=j: (j, c, 0))

    st_spec = pl.BlockSpec((group, HEADS, HEAD_DIM, HEAD_DIM), lambda c: (c, 0, 0, 0))
    tab = pltpu.VMEM((HEADS, CHUNK, CHUNK), F32)
    return pl.pallas_call(
        _mix_kernel,
        grid=(t // rows,),
        in_specs=[
            smem, smem, sec(0), sec(1), sec(2), sec(3), sec(4), sec(5), st_spec, st_spec,
            pl.BlockSpec((HEADS, CHUNK, CHUNK), lambda c: (0, 0, 0)),
            pl.BlockSpec((HEADS, CHUNK, 1), lambda c: (0, 0, 0)),
            pl.BlockSpec((1, SECTION), lambda c: (0, 0)),
        ],
        out_specs=pl.BlockSpec((rows, 2 * SECTION), lambda c: (c, 0)),
        out_shape=jax.ShapeDtypeStruct((t, 2 * SECTION), BF16),
        scratch_shapes=[tab, tab, tab],
        compiler_params=pltpu.CompilerParams(
            dimension_semantics=("arbitrary",), vmem_limit_bytes=VMEM_LIMIT),
        name="mix",
    )(lgf, lgb, proj, proj, proj, proj, proj, proj, sf, sb, ws, bs, rn)


def _post_kernel(xp_ref, xs_ref, m_ref, wo_ref, nw_ref, wr_ref, br_ref,
                 x1_ref, h3_ref, route_ref, rt_ref, cnt_ref, x1_prev, *, tm, blocks_p, blocks_t):
    i = pl.program_id(0)

    @pl.when(i == 0)
    def _():
        cnt_ref[...] = jnp.zeros_like(cnt_ref)
        x1_prev[...] = jnp.zeros_like(x1_prev)

    h = _rms(x1_prev[...], nw_ref[...])
    for c, words in enumerate(_pack_row_tiles(h)):
        h3_ref[pl.ds(c, tm, stride=ROW_TILES), :] = words

    half = D_MODEL // 2
    is_p = jnp.minimum(i, blocks_t - 1) < blocks_p
    logits = None
    for cols in (slice(0, half), slice(half, D_MODEL)):
        x = jnp.where(is_p, xp_ref[:, cols], xs_ref[:, cols])
        x1 = x + jnp.dot(m_ref[...], wo_ref[:, cols], preferred_element_type=F32)
        x1_ref[:, cols] = x1
        x1_prev[:, cols] = x1
        if logits is None:
            logits = jnp.dot(h.astype(BF16), wr_ref[...], preferred_element_type=F32) + br_ref[...]

    lane = lax.broadcasted_iota(jnp.int32, (tm, LANES), 1).astype(F32)
    neg = jnp.float32(-jnp.inf)

    def first_max(vals):
        m = jnp.max(vals, axis=-1, keepdims=True)
        idx = jnp.min(jnp.where(vals == m, lane, float(LANES)), axis=-1, keepdims=True)
        return m, idx

    lg = jnp.where((lane >= N_EXPERTS) & (lane < N_EXPERTS + N_GROUPS), logits, neg)
    gmax, gidx = first_max(lg)
    p_sel = 1.0 / jnp.sum(jnp.exp(lg - gmax), axis=-1, keepdims=True)
    lo = (gidx - N_EXPERTS) * EXPERTS_PER_GROUP
    le = jnp.where((lane >= lo) & (lane < lo + EXPERTS_PER_GROUP), logits, neg)
    top1, i1 = first_max(le)
    top2, i2 = first_max(jnp.where(lane == i1, neg, le))
    e2 = jnp.exp(top2 - top1)
    w1 = p_sel / (1.0 + e2)
    w2 = p_sel * e2 / (1.0 + e2)

    hit1 = lane == i1
    hit2 = lane == i2
    onehot = jnp.where((hit1 | hit2) & (i > 0), 1.0, 0.0).astype(BF16)
    r = lax.broadcasted_iota(jnp.int32, (tm, tm), 0)
    c = lax.broadcasted_iota(jnp.int32, (tm, tm), 1)
    lower = jnp.where(c < r, 1.0, 0.0).astype(BF16)
    before = jnp.dot(lower, onehot, preferred_element_type=F32) + cnt_ref[...]
    rank1 = jnp.sum(jnp.where(hit1, before, 0.0), axis=-1, keepdims=True)
    rank2 = jnp.sum(jnp.where(hit2, before, 0.0), axis=-1, keepdims=True)
    cnt_ref[...] += jnp.sum(onehot.astype(F32), axis=0, keepdims=True)

    route = jnp.where(lane == 0, i1, 0.0)
    route = jnp.where(lane == 1, i2, route)
    route = jnp.where(lane == 2, rank1, route)
    route = jnp.where(lane == 3, rank2, route)
    route = jnp.where(lane == 4, w1, route)
    route = jnp.where(lane == 5, w2, route)
    route_ref[...] = route
    rt_ref[...] = jnp.transpose(route)[0:SUBLANES, :]


def _post(xp, xs, mixed, w_out, norm_w, wr, br, tm):
    t = mixed.shape[0]
    blocks_p = xp.shape[0] // tm
    blocks_t = t // tm
    last = blocks_t - 1

    def cur(i):
        return jnp.minimum(i, last)

    def prev(i):
        return jnp.maximum(i - 1, 0)

    return pl.pallas_call(
        functools.partial(_post_kernel, tm=tm, blocks_p=blocks_p, blocks_t=blocks_t),
        grid=(blocks_t + 1,),
        in_specs=[
            pl.BlockSpec((tm, D_MODEL), lambda i: (jnp.minimum(i, blocks_p - 1), 0)),
            pl.BlockSpec((tm, D_MODEL), lambda i: (jnp.clip(i - blocks_p, 0, last - blocks_p), 0)),
            pl.BlockSpec((tm, D_MODEL), lambda i: (cur(i), 0)),
            pl.BlockSpec((D_MODEL, D_MODEL), lambda i: (0, 0), pipeline_mode=pl.Buffered(1)),
            pl.BlockSpec((1, D_MODEL), lambda i: (0, 0)),
            pl.BlockSpec((D_MODEL, LANES), lambda i: (0, 0)),
            pl.BlockSpec((1, LANES), lambda i: (0, 0)),
        ],
        out_specs=[
            pl.BlockSpec((tm, D_MODEL), lambda i: (cur(i), 0)),
            pl.BlockSpec((tm * ROW_TILES, LANES), lambda i: (prev(i), 0)),
            pl.BlockSpec((tm, LANES), lambda i: (prev(i), 0)),
            pl.BlockSpec((SUBLANES, tm), lambda i: (0, prev(i))),
            pl.BlockSpec((1, LANES), lambda i: (0, 0)),
        ],
        scratch_shapes=[pltpu.VMEM((tm, D_MODEL), F32)],
        out_shape=[
            jax.ShapeDtypeStruct((t, D_MODEL), F32),
            jax.ShapeDtypeStruct((t * ROW_TILES, LANES), U32),
            jax.ShapeDtypeStruct((t, LANES), F32),
            jax.ShapeDtypeStruct((SUBLANES, t), F32),
            jax.ShapeDtypeStruct((1, LANES), F32),
        ],
        compiler_params=pltpu.CompilerParams(
            dimension_semantics=("arbitrary",), vmem_limit_bytes=VMEM_LIMIT),
        name="post",
    )(xp, xs, mixed, w_out, norm_w, wr, br)


def _plan_kernel(cnt_ref, rt_ref, dest_ref, be_ref, nu_ref, zst_ref, zcn_ref, ps_ref, *, n_blocks):
    def per_expert(e, first):
        n = (cnt_ref[e] + (EXPERT_BLOCK - 1)) // EXPERT_BLOCK
        ps_ref[e] = first * EXPERT_BLOCK
        zst_ref[e] = first * EXPERT_BLOCK + cnt_ref[e]
        zcn_ref[e] = n * EXPERT_BLOCK - cnt_ref[e]

        def fill(j, carry):
            be_ref[first + j] = e
            return carry
        lax.fori_loop(0, n, fill, 0)
        return first + n
    n_used = lax.fori_loop(0, N_EXPERTS, per_expert, 0)
    nu_ref[0] = n_used

    def fill_tail(j, carry):
        be_ref[j] = N_EXPERTS - 1
        return carry
    lax.fori_loop(n_used, n_blocks, fill_tail, 0)

    for k in range(TOP_K):
        e = rt_ref[k:k + 1, :]
        base = jnp.zeros_like(e)
        for x in range(N_EXPERTS):
            base = jnp.where(e == float(x), ps_ref[x].astype(F32), base)
        dest_ref[k:k + 1, :] = (base + rt_ref[TOP_K + k:TOP_K + k + 1, :]).astype(I32)


def _plan(counts, route_t, n_blocks):
    t = route_t.shape[1]
    smem = pl.BlockSpec(memory_space=pltpu.SMEM)
    return pl.pallas_call(
        functools.partial(_plan_kernel, n_blocks=n_blocks),
        grid=(1,),
        in_specs=[smem, pl.BlockSpec((SUBLANES, t), lambda i: (0, 0))],
        out_specs=[pl.BlockSpec((TOP_K, t), lambda i: (0, 0)), smem, smem, smem, smem],
        out_shape=[
            jax.ShapeDtypeStruct((TOP_K, t), I32),
            jax.ShapeDtypeStruct((n_blocks,), I32),
            jax.ShapeDtypeStruct((1,), I32),
            jax.ShapeDtypeStruct((N_EXPERTS,), I32),
            jax.ShapeDtypeStruct((N_EXPERTS,), I32),
        ],
        scratch_shapes=[pltpu.SMEM((N_EXPERTS,), I32)],
        compiler_params=pltpu.CompilerParams(dimension_semantics=("arbitrary",)),
        name="plan",
    )(counts, route_t)


def _invert_kernel(dest_ref, zst_ref, zcn_ref, tok_ref, *, n_tok):
    def pad_expert(e, carry):
        start = zst_ref[e]

        def one(j, c):
            tok_ref[start + j] = 0
            return c
        lax.fori_loop(0, zcn_ref[e], one, 0)
        return carry
    lax.fori_loop(0, N_EXPERTS, pad_expert, 0)

    def fill(i, carry):
        for u in range(DMA_UNROLL):
            tok = i * DMA_UNROLL + u
            for k in range(TOP_K):
                tok_ref[dest_ref[k * n_tok + tok]] = tok
        return carry
    lax.fori_loop(0, n_tok // DMA_UNROLL, fill, 0)


def _invert(dest_flat, pad_start, pad_count, n_tok, n_slots):
    assert EXPERT_BLOCK & (EXPERT_BLOCK - 1) == 0
    smem = pl.BlockSpec(memory_space=pltpu.SMEM)
    return pl.pallas_call(
        functools.partial(_invert_kernel, n_tok=n_tok),
        grid_spec=pltpu.PrefetchScalarGridSpec(
            num_scalar_prefetch=3,
            grid=(1,),
            in_specs=[],
            out_specs=smem,
        ),
        out_shape=jax.ShapeDtypeStruct((n_slots,), I32),
        compiler_params=pltpu.CompilerParams(dimension_semantics=("arbitrary",)),
        name="invert",
    )(dest_flat, pad_start, pad_count)


def _row_slab(buf, r):
    start = r * ROW_TILES
    if not isinstance(r, int):
        start = pl.multiple_of(start, ROW_TILES)
    return buf.at[pl.ds(start, ROW_TILES)]


def _row_loop(n, fn):
    def body(i, carry):
        for u in range(DMA_UNROLL):
            fn(i * DMA_UNROLL + u)
        return carry
    lax.fori_loop(0, n // DMA_UNROLL, body, 0)


def _experts_kernel(be_ref, nu_ref, tok_ref, h3_ref, wg_ref, wu_ref, wd_ref, y_ref,
                    xb0, xb1, wgb, wub, wdb, gsem):
    b = pl.program_id(0)
    n_used = nu_ref[0]
    blk = EXPERT_BLOCK

    def gather_copy(tok, buf, r, par):
        return pltpu.make_async_copy(h3_ref.at[tok], _row_slab(buf, r), gsem.at[par])

    def gather_wait(buf, par):
        _row_loop(blk, lambda r: gather_copy(0, buf, r, par).wait())

    def step(par, x_cur, x_nxt):
        @pl.when(b == 0)
        def _():
            _row_loop(blk, lambda r: gather_copy(tok_ref[r], x_cur, r, par).start())

        @pl.when((b == 0) | (be_ref[b] != be_ref[jnp.maximum(b - 1, 0)]))
        def _():
            wgb[...] = wg_ref[0].astype(BF16)
            wub[...] = wu_ref[0].astype(BF16)
            wdb[...] = wd_ref[0].astype(BF16)

        gather_wait(x_cur, par)

        x = jnp.concatenate(_unpack_row_tiles(
            [x_cur[pl.ds(c, blk, stride=ROW_TILES), :] for c in range(ROW_TILES)]), axis=1
        ).astype(BF16)
        nxt = jnp.minimum(b + 1, n_used - 1) * blk
        for r in range(blk):
            gather_copy(tok_ref[nxt + r], x_nxt, r, 1 - par).start(priority=1)
        gate = jnp.dot(x, wgb[...], preferred_element_type=F32)
        up = jnp.dot(x, wub[...], preferred_element_type=F32)
        hid = (jax.nn.silu(gate) * up).astype(BF16)
        y = jnp.dot(hid, wdb[...], preferred_element_type=F32)
        for c, words in enumerate(_pack_row_tiles(y)):
            y_ref[pl.ds(c, blk, stride=ROW_TILES), :] = words

        @pl.when(b == n_used - 1)
        def _():
            gather_wait(x_nxt, 1 - par)

    for par, bufs in ((0, (xb0, xb1)), (1, (xb1, xb0))):
        @pl.when((b < n_used) & (b % 2 == par))
        def _(par=par, bufs=bufs):
            step(par, *bufs)


def _experts(h3, block_expert, n_used, slot_tok, wg, wu, wd):
    n_blocks = block_expert.shape[0]

    def used(b, nu):
        return jnp.minimum(b, nu[0] - 1)

    def wmap(b, be, nu, st):
        return (be[used(b, nu)], 0, 0)

    rows = EXPERT_BLOCK * ROW_TILES
    buf = pltpu.VMEM((rows, LANES), U32)
    return pl.pallas_call(
        _experts_kernel,
        grid_spec=pltpu.PrefetchScalarGridSpec(
            num_scalar_prefetch=3,
            grid=(n_blocks,),
            in_specs=[
                pl.BlockSpec(memory_space=pl.ANY),
                pl.BlockSpec((1, D_MODEL, D_EXPERT), wmap),
                pl.BlockSpec((1, D_MODEL, D_EXPERT), wmap),
                pl.BlockSpec((1, D_EXPERT, D_MODEL), wmap),
            ],
            out_specs=pl.BlockSpec((rows, LANES), lambda b, be, nu, st: (used(b, nu), 0)),
            scratch_shapes=[
                buf, buf,
                pltpu.VMEM((D_MODEL, D_EXPERT), BF16),
                pltpu.VMEM((D_MODEL, D_EXPERT), BF16),
                pltpu.VMEM((D_EXPERT, D_MODEL), BF16),
                pltpu.SemaphoreType.DMA((2,)),
            ],
        ),
        out_shape=jax.ShapeDtypeStruct((n_blocks * rows, LANES), U32),
        compiler_params=pltpu.CompilerParams(
            dimension_semantics=("arbitrary",), vmem_limit_bytes=VMEM_LIMIT),
        name="experts",
    )(block_expert, n_used, slot_tok, h3, wg, wu, wd)


def _final_kernel(d_ref, x1_ref, route_ref, nw_ref, y_ref, op_ref, os_ref,
                  ya0, yb0, ya1, yb1, sem, *, blocks_p, blocks_t, n_tok):
    i = pl.program_id(0)
    tm = x1_ref.shape[0]

    def copy(slot, buf, r, par):
        return pltpu.make_async_copy(y_ref.at[slot], _row_slab(buf, r), sem.at[par])

    def gather_start(tile, bufs, par):
        def one(r):
            for k in range(TOP_K):
                copy(d_ref[k * n_tok + tile * tm + r], bufs[k], r, par).start()
        _row_loop(tm, one)

    def gather_wait(bufs, par):
        def one(r):
            for k in range(TOP_K):
                copy(0, bufs[k], r, par).wait()
        _row_loop(tm, one)

    def rows(buf):
        return jnp.concatenate(_unpack_row_tiles(
            [buf[pl.ds(c, tm, stride=ROW_TILES), :] for c in range(ROW_TILES)]), axis=1)

    def step(par, cur, nxt):
        @pl.when(i == 0)
        def _():
            gather_start(0, cur, par)

        @pl.when(i + 1 < blocks_t)
        def _():
            gather_start(i + 1, nxt, 1 - par)

        gather_wait(cur, par)
        w1 = route_ref[:, 4:5]
        w2 = route_ref[:, 5:6]
        x2 = x1_ref[...] + (rows(cur[0]) * w1 + rows(cur[1]) * w2)
        out = _rms(x2, nw_ref[...])

        @pl.when(i < blocks_p)
        def _():
            op_ref[...] = out

        @pl.when(i >= blocks_p)
        def _():
            os_ref[...] = out

    for par, bufs in ((0, ((ya0, yb0), (ya1, yb1))), (1, ((ya1, yb1), (ya0, yb0)))):
        @pl.when(i % 2 == par)
        def _(par=par, bufs=bufs):
            step(par, *bufs)


def _final(x1, route, norm_w, y, dest_flat, tp, tm):
    t = x1.shape[0]
    blocks_p = tp // tm
    blocks_t = t // tm
    buf = pltpu.VMEM((tm * ROW_TILES, LANES), U32)
    return pl.pallas_call(
        functools.partial(_final_kernel, blocks_p=blocks_p, blocks_t=blocks_t, n_tok=t),
        grid_spec=pltpu.PrefetchScalarGridSpec(
            num_scalar_prefetch=1,
            grid=(blocks_t,),
            in_specs=[
                pl.BlockSpec((tm, D_MODEL), lambda i, d: (i, 0)),
                pl.BlockSpec((tm, LANES), lambda i, d: (i, 0)),
                pl.BlockSpec((1, D_MODEL), lambda i, d: (0, 0)),
                pl.BlockSpec(memory_space=pl.ANY),
            ],
            out_specs=[
                pl.BlockSpec((tm, D_MODEL), lambda i, d: (jnp.minimum(i, blocks_p - 1), 0)),
                pl.BlockSpec((tm, D_MODEL), lambda i, d: (jnp.maximum(i - blocks_p, 0), 0)),
            ],
            scratch_shapes=[buf, buf, buf, buf, pltpu.SemaphoreType.DMA((2,))],
        ),
        out_shape=[
            jax.ShapeDtypeStruct((tp, D_MODEL), F32),
            jax.ShapeDtypeStruct((t - tp, D_MODEL), F32),
        ],
        compiler_params=pltpu.CompilerParams(dimension_semantics=("arbitrary",)),
        name="final",
    )(dest_flat, x1, route, norm_w, y)


def _rope_tables(seq_len):
    half = HEAD_DIM // 2
    inv = ROPE_BASE ** (-jnp.arange(half, dtype=F32) / half)
    ang = jnp.arange(seq_len, dtype=F32)[:, None] * inv[None, :]
    cos = jnp.cos(ang)
    sin = jnp.sin(ang)
    return jnp.concatenate([cos, cos], axis=1), jnp.concatenate([-sin, sin], axis=1)


def _tile(limit, *sizes):
    return min(limit, math.gcd(*sizes))


def kernel(x_prompt, x_sample, norm_mix, w_in, gmlp_norm_v, gmlp_w_spatial, gmlp_b_spatial, ret_decay_fwd, ret_decay_bwd, ret_norm, w_out, norm_ffn, w_router_group, b_router_group, w_router_expert, b_router_expert, w_expert_gate, w_expert_up, w_expert_down, norm_final):
    assert norm_mix.shape[0] == 1, "single-layer block"
    bp, sp, d = x_prompt.shape
    bs, ss, _ = x_sample.shape
    assert d == D_MODEL and sp % CHUNK == 0 and ss % CHUNK == 0
    tp, ts = bp * sp, bs * ss
    t = tp + ts
    xp = x_prompt.reshape(tp, d)
    xs = x_sample.reshape(ts, d)

    tm = _tile(1024, sp, ss)
    cos_t, sin_t = _rope_tables(max(sp, ss))
    w_in_b = w_in[0].astype(BF16)
    gv_w = gmlp_norm_v.reshape(1, SECTION)
    proj = _proj(xp, norm_mix, w_in_b, cos_t, sin_t, gv_w, None, t, 0, tm, sp // tm)
    proj = _proj(xs, norm_mix, w_in_b, cos_t, sin_t, gv_w, proj, t, tp // tm, tm, ss // tm)

    lgf = jax.nn.log_sigmoid(ret_decay_fwd[0].astype(F32))
    lgb = jax.nn.log_sigmoid(ret_decay_bwd[0].astype(F32))
    group = _tile(8, sp // CHUNK, ss // CHUNK)
    sf, sb = _states(proj, lgf, lgb, group, tp // CHUNK, sp // CHUNK, ss // CHUNK)
    mixed = _mix(proj, sf, sb, lgf, lgb, gmlp_w_spatial[0].astype(BF16),
                 gmlp_b_spatial[0][:, :, None], ret_norm.reshape(1, SECTION), group)

    pad = LANES - N_EXPERTS - N_GROUPS
    wr = jnp.concatenate([w_router_expert[0], w_router_group[0], jnp.zeros((d, pad), F32)], axis=1)
    br = jnp.concatenate([b_router_expert[0], b_router_group[0], jnp.zeros((pad,), F32)])[None, :]
    x1, h3, route, route_t, cnt = _post(xp, xs, mixed, w_out[0].astype(BF16), norm_ffn,
                                        wr.astype(BF16), br, _tile(256, tp, ts))
    tm = _tile(256, tp, ts)

    n_blocks = -(-(TOP_K * t) // EXPERT_BLOCK) + N_EXPERTS
    dest, block_expert, n_used, pad_start, pad_count = _plan(
        cnt[0, :N_EXPERTS].astype(I32), route_t, n_blocks)
    dest_flat = dest.reshape(TOP_K * t)
    slot_tok = _invert(dest_flat, pad_start, pad_count, t, n_blocks * EXPERT_BLOCK)
    y = _experts(h3.reshape(t, ROW_TILES, LANES), block_expert, n_used, slot_tok,
                 w_expert_gate[0], w_expert_up[0], w_expert_down[0])

    out_p, out_s = _final(x1, route, norm_final[None, :], y.reshape(-1, ROW_TILES, LANES),
                          dest_flat, tp, tm)
    return out_p.reshape(bp, sp, d), out_s.reshape(bs, ss, d)
```

```python
import functools
import math

import jax
import jax.numpy as jnp
from jax import lax
from jax.experimental import pallas as pl
from jax.experimental.pallas import tpu as pltpu

F32 = jnp.float32
BF16 = jnp.bfloat16
I32 = jnp.int32
U32 = jnp.uint32

D_MODEL = 2048
CHUNK = 128
HEADS = 8
HEAD_DIM = 128
SECTION = HEADS * HEAD_DIM
N_SECTIONS = 6
PROJ_WIDTH = N_SECTIONS * SECTION
ROPE_BASE = 10000.0
N_GROUPS = 4
EXPERTS_PER_GROUP = 8
N_EXPERTS = N_GROUPS * EXPERTS_PER_GROUP
TOP_K = 2
D_EXPERT = D_MODEL // 4
EPS = 1e-6
LANES = 128
SUBLANES = 8
MXU_COLS = 256
ROW_TILES = D_MODEL // LANES // 2
EXPERT_BLOCK = 256
DMA_UNROLL = 8
VMEM_LIMIT = 56 * 1024 * 1024


def _rms(x, w):
    ms = jnp.mean(x * x, axis=-1, keepdims=True)
    return x * lax.rsqrt(ms + EPS) * w


def _pack_row_tiles(x):
    def bits(c):
        return lax.bitcast_convert_type(x[:, c * LANES:(c + 1) * LANES].astype(BF16).astype(F32), U32)
    return [(bits(2 * c) >> 16) | bits(2 * c + 1) for c in range(ROW_TILES)]


def _unpack_row_tiles(words):
    tiles = []
    for w in words:
        tiles.append(lax.bitcast_convert_type(w << 16, F32))
        tiles.append(lax.bitcast_convert_type(w & jnp.uint32(0xFFFF0000), F32))
    return tiles


def _proj_kernel(x_ref, nw_ref, w_ref, cos_ref, sin_ref, gvw_ref, prev_ref, o_ref, h_ref):
    del prev_ref
    j = pl.program_id(1)

    @pl.when(j == 0)
    def _():
        h_ref[...] = _rms(x_ref[...], nw_ref[...]).astype(BF16)

    def rotary(seg):
        return seg * cos_ref[...] + pltpu.roll(seg, HEAD_DIM // 2, axis=1) * sin_ref[...]

    def section(epilogue):
        for c in range(SECTION // MXU_COLS):
            acc = jnp.dot(h_ref[...], w_ref[:, c * MXU_COLS:(c + 1) * MXU_COLS],
                          preferred_element_type=F32)
            for g in range(MXU_COLS // HEAD_DIM):
                sl = slice(c * MXU_COLS + g * HEAD_DIM, c * MXU_COLS + (g + 1) * HEAD_DIM)
                o_ref[:, sl] = epilogue(acc[:, g * HEAD_DIM:(g + 1) * HEAD_DIM], sl).astype(BF16)

    @pl.when(j == 0)
    def _():
        section(lambda seg, sl: jax.nn.gelu(seg))

    @pl.when(j == 1)
    def _():
        section(lambda seg, sl: _rms(jax.nn.gelu(seg), gvw_ref[:, sl]))

    @pl.when(j == 2)
    def _():
        section(lambda seg, sl: rotary(seg))

    @pl.when(j == 3)
    def _():
        section(lambda seg, sl: rotary(seg) * (HEAD_DIM ** -0.5))

    @pl.when(j == 4)
    def _():
        section(lambda seg, sl: seg)

    @pl.when(j == 5)
    def _():
        section(lambda seg, sl: jax.nn.silu(seg))


def _proj(x, norm_w, w_in, cos_t, sin_t, gv_w, prev, t_total, row_block0, tm, per_seq):
    if prev is None:
        prev = jnp.zeros((SUBLANES, LANES), BF16)
        aliases = {}
    else:
        aliases = {6: 0}
    return pl.pallas_call(
        _proj_kernel,
        grid=(x.shape[0] // tm, N_SECTIONS),
        in_specs=[
            pl.BlockSpec((tm, D_MODEL), lambda i, j: (i, 0)),
            pl.BlockSpec((1, D_MODEL), lambda i, j: (0, 0)),
            pl.BlockSpec((D_MODEL, SECTION), lambda i, j: (0, j)),
            pl.BlockSpec((tm, HEAD_DIM), lambda i, j: (i % per_seq, 0)),
            pl.BlockSpec((tm, HEAD_DIM), lambda i, j: (i % per_seq, 0)),
            pl.BlockSpec((1, SECTION), lambda i, j: (0, 0)),
            pl.BlockSpec(memory_space=pl.ANY),
        ],
        out_specs=pl.BlockSpec((None, tm, SECTION), lambda i, j: (j, i + row_block0, 0)),
        out_shape=jax.ShapeDtypeStruct((N_SECTIONS, t_total, SECTION), BF16),
        scratch_shapes=[pltpu.VMEM((tm, D_MODEL), BF16)],
        input_output_aliases=aliases,
        compiler_params=pltpu.CompilerParams(
            dimension_semantics=("arbitrary", "arbitrary"), vmem_limit_bytes=VMEM_LIMIT),
        name="proj",
    )(x, norm_w, w_in, cos_t, sin_t, gv_w, prev)


def _seq_edge(c, chunks_p, n_p, n_s, last):
    pos = jnp.where(c < chunks_p, c % n_p, (c - chunks_p) % n_s)
    edge = jnp.where(c < chunks_p, n_p - 1, n_s - 1) if last else 0
    return pos == edge


def _states_kernel(lgf_ref, lgb_ref, kf_ref, vf_ref, kb_ref, vb_ref, sf_ref, sb_ref, stf, stb,
                   *, group, n_chunks, chunks_p, n_p, n_s):
    s = pl.program_id(0)
    row = lax.broadcasted_iota(jnp.int32, (CHUNK, HEAD_DIM), 0).astype(F32)
    tn = (((0,), (0,)), ((), ()))

    def one_chunk(g, carry):
        cf = s * group + g
        gb = group - 1 - g

        @pl.when(_seq_edge(cf, chunks_p, n_p, n_s, last=False))
        def _():
            stf[...] = jnp.zeros_like(stf)

        @pl.when(_seq_edge(n_chunks - 1 - cf, chunks_p, n_p, n_s, last=True))
        def _():
            stb[...] = jnp.zeros_like(stb)

        rf = pl.ds(pl.multiple_of(g * CHUNK, CHUNK), CHUNK)
        rb = pl.ds(pl.multiple_of(gb * CHUNK, CHUNK), CHUNK)
        for h in range(HEADS):
            sl = slice(h * HEAD_DIM, (h + 1) * HEAD_DIM)
            lgf = lgf_ref[h]
            lgb = lgb_ref[h]
            kd = (kf_ref[rf, sl].astype(F32) * jnp.exp((CHUNK - 1.0 - row) * lgf)).astype(BF16)
            kv = lax.dot_general(kd, vf_ref[rf, sl], tn, preferred_element_type=F32)
            sf_ref[g, h] = stf[h].astype(BF16)
            stf[h] = stf[h] * jnp.exp(CHUNK * lgf) + kv
            kd = (kb_ref[rb, sl].astype(F32) * jnp.exp(row * lgb)).astype(BF16)
            kv = lax.dot_general(kd, vb_ref[rb, sl], tn, preferred_element_type=F32)
            sb_ref[gb, h] = stb[h].astype(BF16)
            stb[h] = stb[h] * jnp.exp(CHUNK * lgb) + kv
        return carry
    lax.fori_loop(0, group, one_chunk, 0)


def _states(proj, lgf, lgb, group, chunks_p, n_p, n_s):
    n_chunks = proj.shape[1] // CHUNK
    last = n_chunks // group - 1
    rows = group * CHUNK
    smem = pl.BlockSpec(memory_space=pltpu.SMEM)
    st_shape = jax.ShapeDtypeStruct((n_chunks, HEADS, HEAD_DIM, HEAD_DIM), BF16)
    return pl.pallas_call(
        functools.partial(_states_kernel, group=group, n_chunks=n_chunks, chunks_p=chunks_p,
                          n_p=n_p, n_s=n_s),
        grid=(n_chunks // group,),
        in_specs=[
            smem, smem,
            pl.BlockSpec((None, rows, SECTION), lambda s: (3, s, 0)),
            pl.BlockSpec((None, rows, SECTION), lambda s: (4, s, 0)),
            pl.BlockSpec((None, rows, SECTION), lambda s: (3, last - s, 0)),
            pl.BlockSpec((None, rows, SECTION), lambda s: (4, last - s, 0)),
        ],
        out_specs=[
            pl.BlockSpec((group, HEADS, HEAD_DIM, HEAD_DIM), lambda s: (s, 0, 0, 0)),
            pl.BlockSpec((group, HEADS, HEAD_DIM, HEAD_DIM), lambda s: (last - s, 0, 0, 0)),
        ],
        out_shape=[st_shape, st_shape],
        scratch_shapes=[pltpu.VMEM((HEADS, HEAD_DIM, HEAD_DIM), F32),
                        pltpu.VMEM((HEADS, HEAD_DIM, HEAD_DIM), F32)],
        compiler_params=pltpu.CompilerParams(
            dimension_semantics=("arbitrary",), vmem_limit_bytes=VMEM_LIMIT),
        name="states",
    )(lgf, lgb, proj, proj, proj, proj)


def _mix_kernel(lgf_ref, lgb_ref, u_ref, gv_ref, q_ref, k_ref, v_ref, g_ref, sf_ref, sb_ref,
                ws_ref, bs_ref, rn_ref, o_ref, dec_ref, qdf_ref, qdb_ref):
    @pl.when(pl.program_id(0) == 0)
    def _():
        row = lax.broadcasted_iota(jnp.int32, (CHUNK, CHUNK), 0).astype(F32)
        col = lax.broadcasted_iota(jnp.int32, (CHUNK, CHUNK), 1).astype(F32)
        diff = row - col
        for h in range(HEADS):
            lgf = lgf_ref[h]
            lgb = lgb_ref[h]
            dec_ref[h] = jnp.where(diff >= 0, jnp.exp(jnp.maximum(diff, 0.0) * lgf),
                                   jnp.exp(jnp.maximum(-diff, 0.0) * lgb))
            qdf_ref[h] = jnp.exp((row + 1.0) * lgf)
            qdb_ref[h] = jnp.exp((CHUNK - row) * lgb)

    nt = (((1,), (1,)), ((), ()))

    def one_chunk(g, carry):
        rows = pl.ds(pl.multiple_of(g * CHUNK, CHUNK), CHUNK)
        for h in range(HEADS):
            sl = slice(h * HEAD_DIM, (h + 1) * HEAD_DIM)
            mixed = jnp.dot(ws_ref[h], gv_ref[rows, sl], preferred_element_type=F32) + bs_ref[h]
            o_ref[rows, sl] = (u_ref[rows, sl].astype(F32) * mixed).astype(BF16)
            q = q_ref[rows, sl]
            v = v_ref[rows, sl]
            scores = lax.dot_general(q, k_ref[rows, sl], nt, preferred_element_type=F32)
            ret = jnp.dot((scores * dec_ref[h]).astype(BF16), v, preferred_element_type=F32)
            ret += jnp.dot(q, sf_ref[g, h], preferred_element_type=F32) * qdf_ref[h]
            ret += jnp.dot(q, sb_ref[g, h], preferred_element_type=F32) * qdb_ref[h]
            out = _rms(ret, rn_ref[:, sl]) * g_ref[rows, sl].astype(F32)
            o_ref[rows, SECTION + h * HEAD_DIM:SECTION + (h + 1) * HEAD_DIM] = out.astype(BF16)
        return carry
    lax.fori_loop(0, u_ref.shape[0] // CHUNK, one_chunk, 0)


def _mix(proj, sf, sb, lgf, lgb, ws, bs, rn, group):
    t = proj.shape[1]
    rows = group * CHUNK
    smem = pl.BlockSpec(memory_space=pltpu.SMEM)

    def sec(j):
        return pl.BlockSpec((None, rows, SECTION), lambda c, j=j: (j, c, 0))

    st_spec = pl.BlockSpec((group, HEADS, HEAD_DIM, HEAD_DIM), lambda c: (c, 0, 0, 0))
    tab = pltpu.VMEM((HEADS, CHUNK, CHUNK), F32)
    return pl.pallas_call(
        _mix_kernel,
        grid=(t // rows,),
        in_specs=[
            smem, smem, sec(0), sec(1), sec(2), sec(3), sec(4), sec(5), st_spec, st_spec,
            pl.BlockSpec((HEADS, CHUNK, CHUNK), lambda c: (0, 0, 0)),
            pl.BlockSpec((HEADS, CHUNK, 1), lambda c: (0, 0, 0)),
            pl.BlockSpec((1, SECTION), lambda c: (0, 0)),
        ],
        out_specs=pl.BlockSpec((rows, 2 * SECTION), lambda c: (c, 0)),
        out_shape=jax.ShapeDtypeStruct((t, 2 * SECTION), BF16),
        scratch_shapes=[tab, tab, tab],
        compiler_params=pltpu.CompilerParams(
            dimension_semantics=("arbitrary",), vmem_limit_bytes=VMEM_LIMIT),
        name="mix",
    )(lgf, lgb, proj, proj, proj, proj, proj, proj, sf, sb, ws, bs, rn)


def _post_kernel(xp_ref, xs_ref, m_ref, wo_ref, nw_ref, wr_ref, br_ref,
                 x1_ref, h3_ref, route_ref, rt_ref, cnt_ref, x1_prev, *, tm, blocks_p, blocks_t):
    i = pl.program_id(0)

    @pl.when(i == 0)
    def _():
        cnt_ref[...] = jnp.zeros_like(cnt_ref)
        x1_prev[...] = jnp.zeros_like(x1_prev)

    h = _rms(x1_prev[...], nw_ref[...])
    for c, words in enumerate(_pack_row_tiles(h)):
        h3_ref[pl.ds(c, tm, stride=ROW_TILES), :] = words

    logits = jnp.dot(h.astype(BF16), wr_ref[...], preferred_element_type=F32) + br_ref[...]

    x = jnp.where(jnp.minimum(i, blocks_t - 1) < blocks_p, xp_ref[...], xs_ref[...])
    x1 = x + jnp.dot(m_ref[...], wo_ref[...], preferred_element_type=F32)
    x1_ref[...] = x1
    x1_prev[...] = x1

    lane = lax.broadcasted_iota(jnp.int32, (tm, LANES), 1).astype(F32)
    neg = jnp.float32(-jnp.inf)

    def first_max(vals):
        m = jnp.max(vals, axis=-1, keepdims=True)
        idx = jnp.min(jnp.where(vals == m, lane, float(LANES)), axis=-1, keepdims=True)
        return m, idx

    lg = jnp.where((lane >= N_EXPERTS) & (lane < N_EXPERTS + N_GROUPS), logits, neg)
    gmax, gidx = first_max(lg)
    p_sel = 1.0 / jnp.sum(jnp.exp(lg - gmax), axis=-1, keepdims=True)
    lo = (gidx - N_EXPERTS) * EXPERTS_PER_GROUP
    le = jnp.where((lane >= lo) & (lane < lo + EXPERTS_PER_GROUP), logits, neg)
    top1, i1 = first_max(le)
    top2, i2 = first_max(jnp.where(lane == i1, neg, le))
    e2 = jnp.exp(top2 - top1)
    w1 = p_sel / (1.0 + e2)
    w2 = p_sel * e2 / (1.0 + e2)

    hit1 = lane == i1
    hit2 = lane == i2
    onehot = jnp.where((hit1 | hit2) & (i > 0), 1.0, 0.0).astype(BF16)
    r = lax.broadcasted_iota(jnp.int32, (tm, tm), 0)
    c = lax.broadcasted_iota(jnp.int32, (tm, tm), 1)
    lower = jnp.where(c < r, 1.0, 0.0).astype(BF16)
    before = jnp.dot(lower, onehot, preferred_element_type=F32) + cnt_ref[...]
    rank1 = jnp.sum(jnp.where(hit1, before, 0.0), axis=-1, keepdims=True)
    rank2 = jnp.sum(jnp.where(hit2, before, 0.0), axis=-1, keepdims=True)
    cnt_ref[...] += jnp.sum(onehot.astype(F32), axis=0, keepdims=True)

    route = jnp.where(lane == 0, i1, 0.0)
    route = jnp.where(lane == 1, i2, route)
    route = jnp.where(lane == 2, rank1, route)
    route = jnp.where(lane == 3, rank2, route)
    route = jnp.where(lane == 4, w1, route)
    route = jnp.where(lane == 5, w2, route)
    route_ref[...] = route
    rt_ref[...] = jnp.transpose(route)[0:SUBLANES, :]


def _post(xp, xs, mixed, w_out, norm_w, wr, br, tm):
    t = mixed.shape[0]
    blocks_p = xp.shape[0] // tm
    blocks_t = t // tm
    last = blocks_t - 1

    def cur(i):
        return jnp.minimum(i, last)

    def prev(i):
        return jnp.maximum(i - 1, 0)

    return pl.pallas_call(
        functools.partial(_post_kernel, tm=tm, blocks_p=blocks_p, blocks_t=blocks_t),
        grid=(blocks_t + 1,),
        in_specs=[
            pl.BlockSpec((tm, D_MODEL), lambda i: (jnp.minimum(i, blocks_p - 1), 0)),
            pl.BlockSpec((tm, D_MODEL), lambda i: (jnp.clip(i - blocks_p, 0, last - blocks_p), 0)),
            pl.BlockSpec((tm, D_MODEL), lambda i: (cur(i), 0)),
            pl.BlockSpec((D_MODEL, D_MODEL), lambda i: (0, 0), pipeline_mode=pl.Buffered(1)),
            pl.BlockSpec((1, D_MODEL), lambda i: (0, 0)),
            pl.BlockSpec((D_MODEL, LANES), lambda i: (0, 0)),
            pl.BlockSpec((1, LANES), lambda i: (0, 0)),
        ],
        out_specs=[
            pl.BlockSpec((tm, D_MODEL), lambda i: (cur(i), 0)),
            pl.BlockSpec((tm * ROW_TILES, LANES), lambda i: (prev(i), 0)),
            pl.BlockSpec((tm, LANES), lambda i: (prev(i), 0)),
            pl.BlockSpec((SUBLANES, tm), lambda i: (0, prev(i))),
            pl.BlockSpec((1, LANES), lambda i: (0, 0)),
        ],
        scratch_shapes=[pltpu.VMEM((tm, D_MODEL), F32)],
        out_shape=[
            jax.ShapeDtypeStruct((t, D_MODEL), F32),
            jax.ShapeDtypeStruct((t * ROW_TILES, LANES), U32),
            jax.ShapeDtypeStruct((t, LANES), F32),
            jax.ShapeDtypeStruct((SUBLANES, t), F32),
            jax.ShapeDtypeStruct((1, LANES), F32),
        ],
        compiler_params=pltpu.CompilerParams(
            dimension_semantics=("arbitrary",), vmem_limit_bytes=VMEM_LIMIT),
        name="post",
    )(xp, xs, mixed, w_out, norm_w, wr, br)


def _plan_kernel(cnt_ref, rt_ref, dest_ref, be_ref, nu_ref, zst_ref, zcn_ref, ps_ref, *, n_blocks):
    def per_expert(e, first):
        n = (cnt_ref[e] + (EXPERT_BLOCK - 1)) // EXPERT_BLOCK
        ps_ref[e] = first * EXPERT_BLOCK
        zst_ref[e] = first * EXPERT_BLOCK + cnt_ref[e]
        zcn_ref[e] = n * EXPERT_BLOCK - cnt_ref[e]

        def fill(j, carry):
            be_ref[first + j] = e
            return carry
        lax.fori_loop(0, n, fill, 0)
        return first + n
    n_used = lax.fori_loop(0, N_EXPERTS, per_expert, 0)
    nu_ref[0] = n_used

    def fill_tail(j, carry):
        be_ref[j] = N_EXPERTS - 1
        return carry
    lax.fori_loop(n_used, n_blocks, fill_tail, 0)

    for k in range(TOP_K):
        e = rt_ref[k:k + 1, :]
        base = jnp.zeros_like(e)
        for x in range(N_EXPERTS):
            base = jnp.where(e == float(x), ps_ref[x].astype(F32), base)
        dest_ref[k:k + 1, :] = (base + rt_ref[TOP_K + k:TOP_K + k + 1, :]).astype(I32)


def _plan(counts, route_t, n_blocks):
    t = route_t.shape[1]
    smem = pl.BlockSpec(memory_space=pltpu.SMEM)
    return pl.pallas_call(
        functools.partial(_plan_kernel, n_blocks=n_blocks),
        grid=(1,),
        in_specs=[smem, pl.BlockSpec((SUBLANES, t), lambda i: (0, 0))],
        out_specs=[pl.BlockSpec((TOP_K, t), lambda i: (0, 0)), smem, smem, smem, smem],
        out_shape=[
            jax.ShapeDtypeStruct((TOP_K, t), I32),
            jax.ShapeDtypeStruct((n_blocks,), I32),
            jax.ShapeDtypeStruct((1,), I32),
            jax.ShapeDtypeStruct((N_EXPERTS,), I32),
            jax.ShapeDtypeStruct((N_EXPERTS,), I32),
        ],
        scratch_shapes=[pltpu.SMEM((N_EXPERTS,), I32)],
        compiler_params=pltpu.CompilerParams(dimension_semantics=("arbitrary",)),
        name="plan",
    )(counts, route_t)


def _invert_kernel(dest_ref, zst_ref, zcn_ref, tok_ref, *, n_tok):
    def pad_expert(e, carry):
        start = zst_ref[e]

        def one(j, c):
            tok_ref[start + j] = 0
            return c
        lax.fori_loop(0, zcn_ref[e], one, 0)
        return carry
    lax.fori_loop(0, N_EXPERTS, pad_expert, 0)

    def fill(i, carry):
        for u in range(DMA_UNROLL):
            tok = i * DMA_UNROLL + u
            for k in range(TOP_K):
                tok_ref[dest_ref[k * n_tok + tok]] = tok
        return carry
    lax.fori_loop(0, n_tok // DMA_UNROLL, fill, 0)


def _invert(dest_flat, pad_start, pad_count, n_tok, n_slots):
    assert EXPERT_BLOCK & (EXPERT_BLOCK - 1) == 0
    smem = pl.BlockSpec(memory_space=pltpu.SMEM)
    return pl.pallas_call(
        functools.partial(_invert_kernel, n_tok=n_tok),
        grid_spec=pltpu.PrefetchScalarGridSpec(
            num_scalar_prefetch=3,
            grid=(1,),
            in_specs=[],
            out_specs=smem,
        ),
        out_shape=jax.ShapeDtypeStruct((n_slots,), I32),
        compiler_params=pltpu.CompilerParams(dimension_semantics=("arbitrary",)),
        name="invert",
    )(dest_flat, pad_start, pad_count)


def _row_slab(buf, r):
    start = r * ROW_TILES
    if not isinstance(r, int):
        start = pl.multiple_of(start, ROW_TILES)
    return buf.at[pl.ds(start, ROW_TILES)]


def _row_loop(n, fn):
    def body(i, carry):
        for u in range(DMA_UNROLL):
            fn(i * DMA_UNROLL + u)
        return carry
    lax.fori_loop(0, n // DMA_UNROLL, body, 0)


def _experts_kernel(be_ref, nu_ref, tok_ref, h3_ref, wg_ref, wu_ref, wd_ref, y_ref,
                    xb0, xb1, wgb, wub, wdb, gsem):
    b = pl.program_id(0)
    n_used = nu_ref[0]
    blk = EXPERT_BLOCK

    def gather_copy(tok, buf, r, par):
        return pltpu.make_async_copy(h3_ref.at[tok], _row_slab(buf, r), gsem.at[par])

    def gather_wait(buf, par):
        _row_loop(blk, lambda r: gather_copy(0, buf, r, par).wait())

    def step(par, x_cur, x_nxt):
        @pl.when(b == 0)
        def _():
            _row_loop(blk, lambda r: gather_copy(tok_ref[r], x_cur, r, par).start())

        @pl.when((b == 0) | (be_ref[b] != be_ref[jnp.maximum(b - 1, 0)]))
        def _():
            wgb[...] = wg_ref[0].astype(BF16)
            wub[...] = wu_ref[0].astype(BF16)
            wdb[...] = wd_ref[0].astype(BF16)

        gather_wait(x_cur, par)

        x = jnp.concatenate(_unpack_row_tiles(
            [x_cur[pl.ds(c, blk, stride=ROW_TILES), :] for c in range(ROW_TILES)]), axis=1
        ).astype(BF16)
        nxt = jnp.minimum(b + 1, n_used - 1) * blk
        for r in range(blk):
            gather_copy(tok_ref[nxt + r], x_nxt, r, 1 - par).start()
        gate = jnp.dot(x, wgb[...], preferred_element_type=F32)
        up = jnp.dot(x, wub[...], preferred_element_type=F32)
        hid = (jax.nn.silu(gate) * up).astype(BF16)
        y = jnp.dot(hid, wdb[...], preferred_element_type=F32)
        for c, words in enumerate(_pack_row_tiles(y)):
            y_ref[pl.ds(c, blk, stride=ROW_TILES), :] = words

        @pl.when(b == n_used - 1)
        def _():
            gather_wait(x_nxt, 1 - par)

    for par, bufs in ((0, (xb0, xb1)), (1, (xb1, xb0))):
        @pl.when((b < n_used) & (b % 2 == par))
        def _(par=par, bufs=bufs):
            step(par, *bufs)


def _experts(h3, block_expert, n_used, slot_tok, wg, wu, wd):
    n_blocks = block_expert.shape[0]

    def used(b, nu):
        return jnp.minimum(b, nu[0] - 1)

    def wmap(b, be, nu, st):
        return (be[used(b, nu)], 0, 0)

    rows = EXPERT_BLOCK * ROW_TILES
    buf = pltpu.VMEM((rows, LANES), U32)
    return pl.pallas_call(
        _experts_kernel,
        grid_spec=pltpu.PrefetchScalarGridSpec(
            num_scalar_prefetch=3,
            grid=(n_blocks,),
            in_specs=[
                pl.BlockSpec(memory_space=pl.ANY),
                pl.BlockSpec((1, D_MODEL, D_EXPERT), wmap),
                pl.BlockSpec((1, D_MODEL, D_EXPERT), wmap),
                pl.BlockSpec((1, D_EXPERT, D_MODEL), wmap),
            ],
            out_specs=pl.BlockSpec((rows, LANES), lambda b, be, nu, st: (used(b, nu), 0)),
            scratch_shapes=[
                buf, buf,
                pltpu.VMEM((D_MODEL, D_EXPERT), BF16),
                pltpu.VMEM((D_MODEL, D_EXPERT), BF16),
                pltpu.VMEM((D_EXPERT, D_MODEL), BF16),
                pltpu.SemaphoreType.DMA((2,)),
            ],
        ),
        out_shape=jax.ShapeDtypeStruct((n_blocks * rows, LANES), U32),
        compiler_params=pltpu.CompilerParams(
            dimension_semantics=("arbitrary",), vmem_limit_bytes=VMEM_LIMIT),
        name="experts",
    )(block_expert, n_used, slot_tok, h3, wg, wu, wd)


def _final_kernel(d_ref, x1_ref, route_ref, nw_ref, y_ref, op_ref, os_ref,
                  ya0, yb0, ya1, yb1, sem, *, blocks_p, blocks_t, n_tok):
    i = pl.program_id(0)
    tm = x1_ref.shape[0]

    def copy(slot, buf, r, par):
        return pltpu.make_async_copy(y_ref.at[slot], _row_slab(buf, r), sem.at[par])

    def gather_start(tile, bufs, par):
        def one(r):
            for k in range(TOP_K):
                copy(d_ref[k * n_tok + tile * tm + r], bufs[k], r, par).start()
        _row_loop(tm, one)

    def gather_wait(bufs, par):
        def one(r):
            for k in range(TOP_K):
                copy(0, bufs[k], r, par).wait()
        _row_loop(tm, one)

    def rows(buf):
        return jnp.concatenate(_unpack_row_tiles(
            [buf[pl.ds(c, tm, stride=ROW_TILES), :] for c in range(ROW_TILES)]), axis=1)

    def step(par, cur, nxt):
        @pl.when(i == 0)
        def _():
            gather_start(0, cur, par)

        @pl.when(i + 1 < blocks_t)
        def _():
            gather_start(i + 1, nxt, 1 - par)

        gather_wait(cur, par)
        w1 = route_ref[:, 4:5]
        w2 = route_ref[:, 5:6]
        x2 = x1_ref[...] + (rows(cur[0]) * w1 + rows(cur[1]) * w2)
        out = _rms(x2, nw_ref[...])

        @pl.when(i < blocks_p)
        def _():
            op_ref[...] = out

        @pl.when(i >= blocks_p)
        def _():
            os_ref[...] = out

    for par, bufs in ((0, ((ya0, yb0), (ya1, yb1))), (1, ((ya1, yb1), (ya0, yb0)))):
        @pl.when(i % 2 == par)
        def _(par=par, bufs=bufs):
            step(par, *bufs)


def _final(x1, route, norm_w, y, dest_flat, tp, tm):
    t = x1.shape[0]
    blocks_p = tp // tm
    blocks_t = t // tm
    buf = pltpu.VMEM((tm * ROW_TILES, LANES), U32)
    return pl.pallas_call(
        functools.partial(_final_kernel, blocks_p=blocks_p, blocks_t=blocks_t, n_tok=t),
        grid_spec=pltpu.PrefetchScalarGridSpec(
            num_scalar_prefetch=1,
            grid=(blocks_t,),
            in_specs=[
                pl.BlockSpec((tm, D_MODEL), lambda i, d: (i, 0)),
                pl.BlockSpec((tm, LANES), lambda i, d: (i, 0)),
                pl.BlockSpec((1, D_MODEL), lambda i, d: (0, 0)),
                pl.BlockSpec(memory_space=pl.ANY),
            ],
            out_specs=[
                pl.BlockSpec((tm, D_MODEL), lambda i, d: (jnp.minimum(i, blocks_p - 1), 0)),
                pl.BlockSpec((tm, D_MODEL), lambda i, d: (jnp.maximum(i - blocks_p, 0), 0)),
            ],
            scratch_shapes=[buf, buf, buf, buf, pltpu.SemaphoreType.DMA((2,))],
        ),
        out_shape=[
            jax.ShapeDtypeStruct((tp, D_MODEL), F32),
            jax.ShapeDtypeStruct((t - tp, D_MODEL), F32),
        ],
        compiler_params=pltpu.CompilerParams(dimension_semantics=("arbitrary",)),
        name="final",
    )(dest_flat, x1, route, norm_w, y)


def _rope_tables(seq_len):
    half = HEAD_DIM // 2
    inv = ROPE_BASE ** (-jnp.arange(half, dtype=F32) / half)
    ang = jnp.arange(seq_len, dtype=F32)[:, None] * inv[None, :]
    cos = jnp.cos(ang)
    sin = jnp.sin(ang)
    return jnp.concatenate([cos, cos], axis=1), jnp.concatenate([-sin, sin], axis=1)


def _tile(limit, *sizes):
    return min(limit, math.gcd(*sizes))


def kernel(x_prompt, x_sample, norm_mix, w_in, gmlp_norm_v, gmlp_w_spatial, gmlp_b_spatial, ret_decay_fwd, ret_decay_bwd, ret_norm, w_out, norm_ffn, w_router_group, b_router_group, w_router_expert, b_router_expert, w_expert_gate, w_expert_up, w_expert_down, norm_final):
    assert norm_mix.shape[0] == 1, "single-layer block"
    bp, sp, d = x_prompt.shape
    bs, ss, _ = x_sample.shape
    assert d == D_MODEL and sp % CHUNK == 0 and ss % CHUNK == 0
    tp, ts = bp * sp, bs * ss
    t = tp + ts
    xp = x_prompt.reshape(tp, d)
    xs = x_sample.reshape(ts, d)

    tm = _tile(1024, sp, ss)
    cos_t, sin_t = _rope_tables(max(sp, ss))
    w_in_b = w_in[0].astype(BF16)
    gv_w = gmlp_norm_v.reshape(1, SECTION)
    proj = _proj(xp, norm_mix, w_in_b, cos_t, sin_t, gv_w, None, t, 0, tm, sp // tm)
    proj = _proj(xs, norm_mix, w_in_b, cos_t, sin_t, gv_w, proj, t, tp // tm, tm, ss // tm)

    lgf = jax.nn.log_sigmoid(ret_decay_fwd[0].astype(F32))
    lgb = jax.nn.log_sigmoid(ret_decay_bwd[0].astype(F32))
    group = _tile(8, sp // CHUNK, ss // CHUNK)
    sf, sb = _states(proj, lgf, lgb, group, tp // CHUNK, sp // CHUNK, ss // CHUNK)
    mixed = _mix(proj, sf, sb, lgf, lgb, gmlp_w_spatial[0].astype(BF16),
                 gmlp_b_spatial[0][:, :, None], ret_norm.reshape(1, SECTION), group)

    pad = LANES - N_EXPERTS - N_GROUPS
    wr = jnp.concatenate([w_router_expert[0], w_router_group[0], jnp.zeros((d, pad), F32)], axis=1)
    br = jnp.concatenate([b_router_expert[0], b_router_group[0], jnp.zeros((pad,), F32)])[None, :]
    x1, h3, route, route_t, cnt = _post(xp, xs, mixed, w_out[0].astype(BF16), norm_ffn,
                                        wr.astype(BF16), br, _tile(256, tp, ts))
    tm = _tile(256, tp, ts)

    n_blocks = -(-(TOP_K * t) // EXPERT_BLOCK) + N_EXPERTS
    dest, block_expert, n_used, pad_start, pad_count = _plan(
        cnt[0, :N_EXPERTS].astype(I32), route_t, n_blocks)
    dest_flat = dest.reshape(TOP_K * t)
    slot_tok = _invert(dest_flat, pad_start, pad_count, t, n_blocks * EXPERT_BLOCK)
    y = _experts(h3.reshape(t, ROW_TILES, LANES), block_expert, n_used, slot_tok,
                 w_expert_gate[0], w_expert_up[0], w_expert_down[0])

    out_p, out_s = _final(x1, route, norm_final[None, :], y.reshape(-1, ROW_TILES, LANES),
                          dest_flat, tp, tm)
    return out_p.reshape(bp, sp, d), out_s.reshape(bs, ss, d)
```
